```python
import math
import jax, jax.numpy as jnp
from jax import lax
import numpy as np

D_MODEL = 1024
BATCH = 2
SEQ = 8192
DEPTH = 4
DEC_BATCH = 128
DEC_SEQ = 1
PAST_LEN = 2048
PAGE_SIZE = 128

N_A_LAYERS = (DEPTH + 1) // 2
N_C_LAYERS = DEPTH // 2
NSA_HEADS = 8
NSA_KV_HEADS = 2
NSA_GROUP = NSA_HEADS // NSA_KV_HEADS
HEAD_DIM = 64
BLOCK = 64
N_SEL = 8
N_LOCAL_BLOCKS = 2
WINDOW = 512
CMP_HIDDEN = 128
Q_BLOCK = 128
FORCE_SCORE = 1.0e4
NSA_Q_W = NSA_HEADS * HEAD_DIM
NSA_KV_W = NSA_KV_HEADS * HEAD_DIM
D_RNN = D_MODEL // 2
RNN_BLOCKS = 8
RNN_BW = D_RNN // RNN_BLOCKS
RNN_CONV = 4
RG_C = 8.0
D_CONV = D_MODEL
CONF_K = 31
MEM_LEN = 256
X_HEADS = 4
X_HEAD_DIM = 128
X_W = X_HEADS * X_HEAD_DIM
D_FF = 2816
N_EXPERTS = 8
TOP_K = 2
D_FF_EXPERT = 2816
LN_EPS = 1e-5
DN_ALPHA = (2.0 * DEPTH) ** 0.25
DN_BETA = (8.0 * DEPTH) ** -0.25
IN_A_WIDTH = NSA_Q_W + 6 * NSA_KV_W + 3 * NSA_HEADS + 2 * D_RNN
MIX_A_WIDTH = NSA_Q_W + D_RNN

kernel_name = 'nsa_rglru_conformer_deepnorm_step'


def layer_norm(x, g, b):
    xf = x.astype(jnp.float32)
    mu = jnp.mean(xf, axis=-1, keepdims=True)
    var = jnp.mean(jnp.square(xf - mu), axis=-1, keepdims=True)
    y = (xf - mu) * lax.rsqrt(var + LN_EPS)
    return (y * g.astype(jnp.float32) + b.astype(jnp.float32)).astype(x.dtype)


def post_norm(x, y, g, b):
    return layer_norm(DN_ALPHA * x + y, g, b)


def masked_softmax(s, mask, axis):
    s = jnp.where(mask, s, -jnp.inf)
    m = jnp.max(s, axis=axis, keepdims=True)
    m = jnp.where(jnp.isfinite(m), m, 0.0)
    e = jnp.exp(s - m)
    return e / jnp.maximum(jnp.sum(e, axis=axis, keepdims=True), 1e-30)


def causal_dwconv(x, prev, w, b):
    xp = jnp.concatenate([prev.astype(x.dtype), x], axis=1)
    y = lax.conv_general_dilated(xp, w[:, None, :].astype(x.dtype), window_strides=(1,), padding='VALID',
                                 dimension_numbers=('NWC', 'WIO', 'NWC'), feature_group_count=x.shape[-1])
    return y + b.astype(x.dtype), xp[:, xp.shape[1] - (w.shape[0] - 1):]


def compress_blocks(k, pos_emb, w1, w2):
    B, T = k.shape[:2]
    nb = T // BLOCK
    kb = k.reshape(B, nb, BLOCK, NSA_KV_HEADS, HEAD_DIM) + pos_emb[:, None, :].astype(k.dtype)
    kb = kb.transpose(0, 1, 3, 2, 4).reshape(B, nb, NSA_KV_HEADS, BLOCK * HEAD_DIM)
    return jax.nn.gelu(kb @ w1) @ w2


def nsa_context(kv4, cmp_pos, cmp_w1, cmp_w2):
    B, T = kv4.shape[:2]
    pad = (-T) % BLOCK
    kv4 = jnp.pad(kv4, ((0, 0), (0, pad), (0, 0), (0, 0), (0, 0)))
    nb = (T + pad) // BLOCK
    ck = compress_blocks(kv4[:, :, 0], cmp_pos[0], cmp_w1[0], cmp_w2[0])
    cv = compress_blocks(kv4[:, :, 1], cmp_pos[1], cmp_w1[1], cmp_w2[1])
    sel = kv4[:, :, 2:4].reshape(B, nb, BLOCK, 2, NSA_KV_HEADS, HEAD_DIM).transpose(3, 0, 4, 1, 2, 5)
    return ck, cv, sel[0], sel[1]


def nsa_queries(q, gates, pos, ck, cv, ksb, vsb, kw, vw, wpos, slopes):
    B, Q = q.shape[:2]
    nb = ck.shape[1]
    f32 = jnp.float32
    scale = HEAD_DIM ** -0.5
    qg = q.reshape(B, Q, NSA_KV_HEADS, NSA_GROUP, HEAD_DIM)
    sl = slopes.reshape(NSA_KV_HEADS, NSA_GROUP, 1, 1)
    qpos = pos.astype(f32)
    blk = jnp.arange(nb)
    blk_end = (blk + 1) * BLOCK - 1
    s_c = (jnp.einsum('bqkgd,bnkd->bkgqn', qg, ck).astype(f32) * scale
           - sl * (qpos[:, None] - blk_end.astype(f32)[None, :]))
    p_c = masked_softmax(s_c, blk_end[None, :] <= pos[:, None], axis=-1)
    o_c = jnp.einsum('bkgqn,bnkd->bqkgd', p_c.astype(cv.dtype), cv)
    imp = jnp.sum(p_c, axis=2)
    cur = (pos // BLOCK)[:, None]
    forced = (blk[None, :] == 0) | ((blk[None, :] <= cur) & (blk[None, :] > cur - N_LOCAL_BLOCKS))
    score = jnp.where(forced, FORCE_SCORE, imp)
    score = jnp.where(blk[None, :] <= cur, score, -jnp.inf)
    top_s, idx = lax.top_k(score, min(N_SEL, nb))
    sel_ok = jnp.isfinite(top_s)
    bi = jnp.arange(B)[:, None, None, None]
    hi = jnp.arange(NSA_KV_HEADS)[None, :, None, None]
    gk = ksb[bi, hi, idx]
    gv = vsb[bi, hi, idx]
    tok = idx[..., None] * BLOCK + jnp.arange(BLOCK)
    s_s = (jnp.einsum('bqkgd,bkqnld->bkgqnl', qg, gk).astype(f32) * scale
           - sl[..., None] * (qpos[:, None, None] - tok.astype(f32))[:, :, None])
    m_s = (sel_ok[..., None] & (tok <= pos[:, None, None]))[:, :, None]
    p_s = masked_softmax(s_s, m_s, axis=(-2, -1))
    o_s = jnp.einsum('bkgqnl,bkqnld->bqkgd', p_s.astype(gv.dtype), gv)
    dist = pos[:, None] - wpos[None, :]
    s_w = (jnp.einsum('bqkgd,btkd->bkgqt', qg, kw).astype(f32) * scale - sl * dist.astype(f32))
    p_w = masked_softmax(s_w, (dist >= 0) & (dist < WINDOW) & (wpos[None, :] >= 0), axis=-1)
    o_w = jnp.einsum('bkgqt,btkd->bqkgd', p_w.astype(vw.dtype), vw)
    g = jax.nn.sigmoid(gates.astype(f32)).reshape(B, Q, NSA_KV_HEADS, NSA_GROUP, 3).astype(q.dtype)
    o = g[..., 0:1] * o_c + g[..., 1:2] * o_s + g[..., 2:3] * o_w
    return o.reshape(B, Q, NSA_Q_W)


def _lin_combine(left, right):
    a1, b1 = left
    a2, b2 = right
    return a1 * a2, a2 * b1 + b2


def rglru_branch(xr, gr, pos, h0, conv_prev, pa):
    B, T, _ = xr.shape
    f32 = jnp.float32
    xc, conv_new = causal_dwconv(xr, conv_prev, pa['conv_w'], pa['conv_b'])
    xb = xc.reshape(B, T, RNN_BLOCKS, RNN_BW)
    r = jax.nn.sigmoid((jnp.einsum('btnc,ncd->btnd', xb, pa['wa']).reshape(B, T, D_RNN) + pa['ba']).astype(f32))
    i = jax.nn.sigmoid((jnp.einsum('btnc,ncd->btnd', xb, pa['wx']).reshape(B, T, D_RNN) + pa['bx']).astype(f32))
    log_a = -RG_C * r * jax.nn.softplus(-pa['lam'].astype(f32))
    a = jnp.exp(log_a)
    mult = jnp.where((pos == 0)[None, :, None], 1.0, jnp.sqrt(-jnp.expm1(2.0 * log_a)))
    b = xc.astype(f32) * i * mult
    b = b.at[:, 0].add(a[:, 0] * h0.astype(f32))
    _, h = lax.associative_scan(_lin_combine, (a, b), axis=1)
    y = h.astype(xr.dtype) * jax.nn.gelu(gr)
    return y, h[:, -1].astype(h0.dtype), conv_new


def mixer_a_project(x, w_in):
    B, T, _ = x.shape
    h = x @ w_in
    o = 0
    q = h[..., o:o + NSA_Q_W].reshape(B, T, NSA_HEADS, HEAD_DIM)
    o += NSA_Q_W
    kvs = h[..., o:o + 6 * NSA_KV_W].reshape(B, T, 6, NSA_KV_HEADS, HEAD_DIM)
    o += 6 * NSA_KV_W
    gates = h[..., o:o + 3 * NSA_HEADS].reshape(B, T, NSA_HEADS, 3)
    o += 3 * NSA_HEADS
    xr = h[..., o:o + D_RNN]
    o += D_RNN
    gr = h[..., o:o + D_RNN]
    return q, kvs, gates, xr, gr


def mixer_a_prompt(x, w_in, w_out, pa, slopes):
    B, T, _ = x.shape
    q, kvs, gates, xr, gr = mixer_a_project(x, w_in)
    ck, cv, ksb, vsb = nsa_context(kvs[:, :, :4], pa['cmp_pos'], pa['cmp_w1'], pa['cmp_w2'])
    n_chunks = T // Q_BLOCK
    qc = q.reshape(B, n_chunks, Q_BLOCK, NSA_HEADS, HEAD_DIM).swapaxes(0, 1)
    gc = gates.reshape(B, n_chunks, Q_BLOCK, NSA_HEADS, 3).swapaxes(0, 1)
    win_pad = jnp.pad(kvs[:, :, 4:6], ((0, 0), (WINDOW, 0), (0, 0), (0, 0), (0, 0)))

    def per_chunk(args):
        qi, gi, c = args
        start = c * Q_BLOCK
        pos = start + jnp.arange(Q_BLOCK)
        wkv = lax.dynamic_slice_in_dim(win_pad, start, WINDOW + Q_BLOCK, axis=1)
        wpos = start - WINDOW + jnp.arange(WINDOW + Q_BLOCK)
        return nsa_queries(qi, gi, pos, ck, cv, ksb, vsb, wkv[:, :, 0], wkv[:, :, 1], wpos, slopes)

    o_nsa = lax.map(per_chunk, (qc, gc, jnp.arange(n_chunks))).swapaxes(0, 1).reshape(B, T, NSA_Q_W)
    o_rnn, h_last, conv_buf = rglru_branch(xr, gr, jnp.arange(T), jnp.zeros((B, D_RNN), x.dtype),
                                           jnp.zeros((B, RNN_CONV - 1, D_RNN), x.dtype), pa)
    out = jnp.concatenate([o_nsa, o_rnn], axis=-1) @ w_out
    n_keep = min(WINDOW, T)
    return out, kvs[:, :, :4], kvs[:, T - n_keep:, 4:6], h_last, conv_buf


def mixer_a_sample(x, kv_pool, page_table, win_buf, h0, conv_prev, w_in, w_out, pa, slopes):
    B, T, _ = x.shape
    past = page_table.shape[1] * kv_pool.shape[1]
    q, kvs, gates, xr, gr = mixer_a_project(x, w_in)
    past_kv = kv_pool[page_table].reshape(B, past, 4, NSA_KV_HEADS, HEAD_DIM).astype(kvs.dtype)
    ck, cv, ksb, vsb = nsa_context(jnp.concatenate([past_kv, kvs[:, :, :4]], axis=1),
                                   pa['cmp_pos'], pa['cmp_w1'], pa['cmp_w2'])
    n_buf = win_buf.shape[1]
    wkv = jnp.concatenate([win_buf.astype(kvs.dtype), kvs[:, :, 4:6]], axis=1)
    wpos = past - n_buf + jnp.arange(n_buf + T)
    pos = past + jnp.arange(T)
    o_nsa = nsa_queries(q, gates, pos, ck, cv, ksb, vsb, wkv[:, :, 0], wkv[:, :, 1], wpos, slopes)
    o_rnn, h_last, conv_buf = rglru_branch(xr, gr, pos, h0, conv_prev, pa)
    out = jnp.concatenate([o_nsa, o_rnn], axis=-1) @ w_out
    return out, kvs[:, :, :4], wkv[:, T:], h_last, conv_buf


def conformer_conv(x, prev, w_glu, b_glu, dw_w, dw_b, g, bn, w_pw, b_pw):
    u = x @ w_glu + b_glu
    gl = u[..., :D_CONV] * jax.nn.sigmoid(u[..., D_CONV:])
    c, new_prev = causal_dwconv(gl, prev, dw_w, dw_b)
    c = layer_norm(c, g, bn)
    return jax.nn.silu(c) @ w_pw + b_pw, new_prev


def memory_kv(mem, w_kv):
    B, M, _ = mem.shape
    return jnp.einsum('bmd,dcf->bmcf', mem, w_kv).reshape(B, M, 2, X_HEADS, X_HEAD_DIM)


def cross_attn(x, kv, wq, wo):
    B, T, _ = x.shape
    q = (x @ wq).reshape(B, T, X_HEADS, X_HEAD_DIM)
    kv = kv.astype(x.dtype)
    s = jnp.einsum('bthd,bmhd->bhtm', q, kv[:, :, 0]).astype(jnp.float32) * (X_HEAD_DIM ** -0.5)
    p = jax.nn.softmax(s, axis=-1).astype(x.dtype)
    o = jnp.einsum('bhtm,bmhd->bthd', p, kv[:, :, 1]).reshape(B, T, X_W)
    return o @ wo


def swiglu(x, w_gu, w_down):
    h = jnp.einsum('btd,dcf->btcf', x, w_gu)
    return (jax.nn.silu(h[..., 0, :]) * h[..., 1, :]) @ w_down


def moe_swiglu(x, router, w_gu, w_down):
    logits = (x @ router).astype(jnp.float32)
    top_v, top_i = lax.top_k(logits, TOP_K)
    w = jax.nn.softmax(top_v, axis=-1)
    gate = jnp.sum(jax.nn.one_hot(top_i, N_EXPERTS, dtype=jnp.float32) * w[..., None], axis=-2).astype(x.dtype)
    out = jnp.zeros_like(x)
    for e in range(N_EXPERTS):
        out = out + gate[..., e:e + 1] * swiglu(x, w_gu[e], w_down[e])
    return out


def setup_inputs(seed: int = 0) -> dict:
    key = jax.random.key(seed)
    keys = iter(jax.random.split(key, 64))

    def nrm(shape, scale):
        return jax.random.normal(next(keys), shape, jnp.float32) * scale

    n_pages = PAST_LEN // PAGE_SIZE
    n_used = DEC_BATCH * n_pages
    n_pool = (5 * n_used + 3) // 4
    w_buf = min(WINDOW, PAST_LEN)
    page_table = jax.random.permutation(next(keys), n_pool)[:n_used].reshape(DEC_BATCH, n_pages).astype(jnp.int32)
    u = jax.random.uniform(next(keys), (N_A_LAYERS, D_RNN), jnp.float32, 0.9, 0.999)
    sg = u ** (1.0 / RG_C)
    a_lambda = jnp.log(sg) - jnp.log1p(-sg)
    D = D_MODEL
    return {
        'x_prompt': nrm((BATCH, SEQ, D), 1.0),
        'x_sample': nrm((DEC_BATCH, DEC_SEQ, D), 1.0),
        'cache_nsa_kv': nrm((N_A_LAYERS, n_pool, PAGE_SIZE, 4, NSA_KV_HEADS, HEAD_DIM), 1.0),
        'cache_nsa_win': nrm((N_A_LAYERS, DEC_BATCH, w_buf, 2, NSA_KV_HEADS, HEAD_DIM), 1.0),
        'state_rglru_h': nrm((N_A_LAYERS, DEC_BATCH, D_RNN), 0.5),
        'state_rglru_conv': nrm((N_A_LAYERS, DEC_BATCH, RNN_CONV - 1, D_RNN), 1.0),
        'state_conv': nrm((N_C_LAYERS, DEC_BATCH, CONF_K - 1, D_CONV), 1.0),
        'cache_mem_kv': nrm((DEPTH, DEC_BATCH, MEM_LEN, 2, X_HEADS, X_HEAD_DIM), 1.0),
        'page_table': page_table,
        'mem_prompt': nrm((BATCH, MEM_LEN, D), 1.0),
        'ln_g': 1.0 + nrm((DEPTH, 3, D), 0.02),
        'ln_b': nrm((DEPTH, 3, D), 0.02),
        'a_w_in': nrm((N_A_LAYERS, D, IN_A_WIDTH), D ** -0.5),
        'a_cmp_pos': nrm((N_A_LAYERS, 2, BLOCK, HEAD_DIM), 0.1),
        'a_cmp_w1': nrm((N_A_LAYERS, 2, BLOCK * HEAD_DIM, CMP_HIDDEN), (BLOCK * HEAD_DIM) ** -0.5),
        'a_cmp_w2': nrm((N_A_LAYERS, 2, CMP_HIDDEN, HEAD_DIM), CMP_HIDDEN ** -0.5),
        'a_conv_w': nrm((N_A_LAYERS, RNN_CONV, D_RNN), RNN_CONV ** -0.5),
        'a_conv_b': nrm((N_A_LAYERS, D_RNN), 0.01),
        'a_gate_a_w': nrm((N_A_LAYERS, RNN_BLOCKS, RNN_BW, RNN_BW), RNN_BW ** -0.5),
        'a_gate_a_b': nrm((N_A_LAYERS, D_RNN), 0.01),
        'a_gate_x_w': nrm((N_A_LAYERS, RNN_BLOCKS, RNN_BW, RNN_BW), RNN_BW ** -0.5),
        'a_gate_x_b': nrm((N_A_LAYERS, D_RNN), 0.01),
        'a_lambda': a_lambda,
        'a_w_out': nrm((N_A_LAYERS, MIX_A_WIDTH, D), DN_BETA * MIX_A_WIDTH ** -0.5),
        'c_w_glu': nrm((N_C_LAYERS, D, 2 * D_CONV), D ** -0.5),
        'c_b_glu': nrm((N_C_LAYERS, 2 * D_CONV), 0.01),
        'c_dw_w': nrm((N_C_LAYERS, CONF_K, D_CONV), CONF_K ** -0.5),
        'c_dw_b': nrm((N_C_LAYERS, D_CONV), 0.01),
        'c_ln_g': 1.0 + nrm((N_C_LAYERS, D_CONV), 0.02),
        'c_ln_b': nrm((N_C_LAYERS, D_CONV), 0.02),
        'c_w_pw': nrm((N_C_LAYERS, D_CONV, D), DN_BETA * D_CONV ** -0.5),
        'c_b_pw': nrm((N_C_LAYERS, D), 0.01),
        'x_wq': nrm((DEPTH, D, X_W), D ** -0.5),
        'x_wkv': nrm((DEPTH, D, 2, X_W), D ** -0.5),
        'x_wo': nrm((DEPTH, X_W, D), DN_BETA * X_W ** -0.5),
        'f_w_gu': nrm((N_A_LAYERS, D, 2, D_FF), D ** -0.5),
        'f_w_down': nrm((N_A_LAYERS, D_FF, D), DN_BETA * D_FF ** -0.5),
        'm_router': nrm((N_C_LAYERS, D, N_EXPERTS), D ** -0.5),
        'm_w_gu': nrm((N_C_LAYERS, N_EXPERTS, D, 2, D_FF_EXPERT), D ** -0.5),
        'm_w_down': nrm((N_C_LAYERS, N_EXPERTS, D_FF_EXPERT, D), DN_BETA * D_FF_EXPERT ** -0.5),
    }


def reference(x_prompt, x_sample, cache_nsa_kv, cache_nsa_win, state_rglru_h, state_rglru_conv, state_conv,
              cache_mem_kv, page_table, mem_prompt, ln_g, ln_b, a_w_in, a_cmp_pos, a_cmp_w1, a_cmp_w2,
              a_conv_w, a_conv_b, a_gate_a_w, a_gate_a_b, a_gate_x_w, a_gate_x_b, a_lambda, a_w_out,
              c_w_glu, c_b_glu, c_dw_w, c_dw_b, c_ln_g, c_ln_b, c_w_pw, c_b_pw, x_wq, x_wkv, x_wo,
              f_w_gu, f_w_down, m_router, m_w_gu, m_w_down):
    slopes = jnp.exp2(-8.0 * jnp.arange(1, NSA_HEADS + 1, dtype=jnp.float32) / NSA_HEADS)
    xp, xs = x_prompt, x_sample
    pk, pw, ph, pcv, pc, pm = [], [], [], [], [], []
    sk, sw, sh, scv, sc = [], [], [], [], []
    for l in range(DEPTH):
        if l % 2 == 0:
            i = l // 2
            pa = {'cmp_pos': a_cmp_pos[i], 'cmp_w1': a_cmp_w1[i], 'cmp_w2': a_cmp_w2[i],
                  'conv_w': a_conv_w[i], 'conv_b': a_conv_b[i], 'wa': a_gate_a_w[i], 'ba': a_gate_a_b[i],
                  'wx': a_gate_x_w[i], 'bx': a_gate_x_b[i], 'lam': a_lambda[i]}
            mp, kv_p, win_p, h_p, cb_p = mixer_a_prompt(xp, a_w_in[i], a_w_out[i], pa, slopes)
            ms, kv_s, win_s, h_s, cb_s = mixer_a_sample(xs, cache_nsa_kv[i], page_table, cache_nsa_win[i],
                                                        state_rglru_h[i], state_rglru_conv[i],
                                                        a_w_in[i], a_w_out[i], pa, slopes)
            pk.append(kv_p); pw.append(win_p); ph.append(h_p); pcv.append(cb_p)
            sk.append(kv_s); sw.append(win_s); sh.append(h_s); scv.append(cb_s)
        else:
            j = l // 2
            cp = (c_w_glu[j], c_b_glu[j], c_dw_w[j], c_dw_b[j], c_ln_g[j], c_ln_b[j], c_w_pw[j], c_b_pw[j])
            mp, cbuf_p = conformer_conv(xp, jnp.zeros((xp.shape[0], CONF_K - 1, D_CONV), xp.dtype), *cp)
            ms, cbuf_s = conformer_conv(xs, state_conv[j], *cp)
            pc.append(cbuf_p); sc.append(cbuf_s)
        xp = post_norm(xp, mp, ln_g[l, 0], ln_b[l, 0])
        xs = post_norm(xs, ms, ln_g[l, 0], ln_b[l, 0])
        mkv_p = memory_kv(mem_prompt, x_wkv[l])
        pm.append(mkv_p)
        xp = post_norm(xp, cross_attn(xp, mkv_p, x_wq[l], x_wo[l]), ln_g[l, 1], ln_b[l, 1])
        xs = post_norm(xs, cross_attn(xs, cache_mem_kv[l], x_wq[l], x_wo[l]), ln_g[l, 1], ln_b[l, 1])
        if l % 2 == 0:
            i = l // 2
            fp = swiglu(xp, f_w_gu[i], f_w_down[i])
            fs = swiglu(xs, f_w_gu[i], f_w_down[i])
        else:
            j = l // 2
            fp = moe_swiglu(xp, m_router[j], m_w_gu[j], m_w_down[j])
            fs = moe_swiglu(xs, m_router[j], m_w_gu[j], m_w_down[j])
        xp = post_norm(xp, fp, ln_g[l, 2], ln_b[l, 2])
        xs = post_norm(xs, fs, ln_g[l, 2], ln_b[l, 2])
    return (xp, xs, jnp.stack(pk), jnp.stack(pw), jnp.stack(ph), jnp.stack(pcv), jnp.stack(pc), jnp.stack(pm),
            jnp.stack(sk), jnp.stack(sw), jnp.stack(sh), jnp.stack(scv), jnp.stack(sc))
```

```python
import functools

import jax
import jax.numpy as jnp
from jax import lax
from jax.experimental import pallas as pl
from jax.experimental.pallas import tpu as pltpu

F32 = jnp.float32
BF = jnp.bfloat16

D_MODEL = 1024
NSA_HEADS = 8
NSA_KV_HEADS = 2
NSA_GROUP = NSA_HEADS // NSA_KV_HEADS
HEAD_DIM = 64
BLOCK = 64
N_SEL = 8
N_LOCAL_BLOCKS = 2
WINDOW = 512
Q_BLOCK = 128
FORCE_SCORE = 1.0e4
D_RNN = 512
RNN_CONV = 4
RG_C = 8.0
CONF_K = 31
MEM_LEN = 256
X_HEADS = 4
X_HEAD_DIM = 128
X_W = X_HEADS * X_HEAD_DIM
D_FF = 2816
N_EXPERTS = 8
TOP_K = 2
LN_EPS = 1e-5
DEPTH = 4
DN_ALPHA = (2.0 * DEPTH) ** 0.25

LANES = 128
SEL_KEY_TILE = 512
WIN_KEYS = WINDOW + Q_BLOCK
FF_CHUNK = 256
MOE_TILE = 256
SCAN_TILE = 256
HALO = 32
VMEM_LIMIT = 56 * 1024 * 1024
NEG = -1.0e30
ALIBI_SLOPES = tuple(2.0 ** (-8.0 * (i + 1) / NSA_HEADS) for i in range(NSA_HEADS))


def _dot(a, b):
    return jnp.dot(a, b, preferred_element_type=F32)


def _dot_nt(a, b):
    return lax.dot_general(a, b, (((1,), (1,)), ((), ())), preferred_element_type=F32)


def _ln(z, g, b):
    mu = jnp.mean(z, axis=-1, keepdims=True)
    zc = z - mu
    var = jnp.mean(zc * zc, axis=-1, keepdims=True)
    return zc * lax.rsqrt(var + LN_EPS) * g + b


def _gelu(x):
    return 0.5 * x * (1.0 + jnp.tanh(0.7978845608028654 * (x + 0.044715 * (x * x * x))))


def _silu(x):
    return x * jax.nn.sigmoid(x)


def _params(sem):
    return pltpu.CompilerParams(dimension_semantics=sem, vmem_limit_bytes=VMEM_LIMIT)


def _row_tile(m, pref=512):
    return pref if m % pref == 0 else m


def _full(shape):
    n = len(shape)
    return pl.BlockSpec(shape, lambda *_: (0,) * n)


Q_PAD_W = NSA_HEADS * LANES
KV_W = 6 * NSA_KV_HEADS * HEAD_DIM
PROJ_W = Q_PAD_W + KV_W + 2 * D_RNN + LANES


def _proj_a_kernel(x_ref, w_ref, q_ref, kv4_ref, kwin_ref, kvb_ref, xr_ref, gr_ref, gate_ref):
    xb = x_ref[...].astype(BF)
    q = _dot(xb, w_ref[:, 0:Q_PAD_W])
    q_ref[...] = (q * (HEAD_DIM ** -0.5)).astype(BF)
    o = Q_PAD_W
    kv = _dot(xb, w_ref[:, o:o + KV_W])
    kv4_ref[...] = kv[:, 0:512]
    kwin_ref[...] = kv[:, 512:768]
    kvb_ref[...] = kv.astype(BF)
    o += KV_W
    xr_ref[...] = _dot(xb, w_ref[:, o:o + D_RNN])
    o += D_RNN
    gr_ref[...] = _dot(xb, w_ref[:, o:o + D_RNN])
    o += D_RNN
    gate_ref[...] = _dot(xb, w_ref[:, o:o + LANES])


def _proj_a(x, w):
    m = x.shape[0]
    tm = _row_tile(m)
    widths = [(Q_PAD_W, BF), (512, F32), (256, F32), (KV_W, BF), (D_RNN, F32), (D_RNN, F32), (LANES, F32)]
    return pl.pallas_call(
        _proj_a_kernel,
        out_shape=[jax.ShapeDtypeStruct((m, w_), dt) for w_, dt in widths],
        grid=(m // tm,),
        in_specs=[pl.BlockSpec((tm, D_MODEL), lambda i: (i, 0)), _full((D_MODEL, PROJ_W))],
        out_specs=[pl.BlockSpec((tm, w_), lambda i: (i, 0)) for w_, _ in widths],
        compiler_params=_params(("parallel",)),
        name="proj_a",
    )(x, w)


def _prep_w_in(w_in):
    wq = w_in[:, :512].reshape(D_MODEL, NSA_HEADS, HEAD_DIM)
    z = jnp.zeros_like(wq)
    lo = jnp.concatenate([wq, z], axis=2)
    hi = jnp.concatenate([z, wq], axis=2)
    wq_p = jnp.concatenate([lo[:, :NSA_GROUP], hi[:, NSA_GROUP:]], axis=1).reshape(D_MODEL, Q_PAD_W)
    wkv = w_in[:, 512:1280]
    wg = jnp.pad(w_in[:, 1280:1304], ((0, 0), (0, LANES - 3 * NSA_HEADS)))
    wxr = w_in[:, 1304:1816]
    wgr = w_in[:, 1816:2328]
    return jnp.concatenate([wq_p, wkv, wxr, wgr, wg], axis=1).astype(BF)


def _prep_w_out(w_out):
    wn = w_out[:512].reshape(NSA_HEADS, HEAD_DIM, D_MODEL)
    z = jnp.zeros_like(wn)
    lo = jnp.concatenate([wn, z], axis=1)
    hi = jnp.concatenate([z, wn], axis=1)
    wn_p = jnp.concatenate([lo[:NSA_GROUP], hi[NSA_GROUP:]], axis=0).reshape(Q_PAD_W, D_MODEL)
    return wn_p.astype(BF), w_out[512:].astype(BF)


def _compress_kernel(x_ref, pos_ref, w1_ref, w2_ref, o_ref):
    xb = (x_ref[0] + pos_ref[0]).astype(BF)
    h = _gelu(_dot(xb, w1_ref[0]))
    o_ref[0] = _dot(h.astype(BF), w2_ref[0])


def _compress(xblk, pos, w1, w2):
    r = xblk.shape[1]
    tr = _row_tile(r, 256)
    kdim = BLOCK * HEAD_DIM
    return pl.pallas_call(
        _compress_kernel,
        out_shape=jax.ShapeDtypeStruct((2, r, HEAD_DIM), F32),
        grid=(2, r // tr),
        in_specs=[pl.BlockSpec((1, tr, kdim), lambda c, i: (c, i, 0)),
                  pl.BlockSpec((1, 1, kdim), lambda c, i: (c, 0, 0)),
                  pl.BlockSpec((1, kdim, 128), lambda c, i: (c, 0, 0)),
                  pl.BlockSpec((1, 128, HEAD_DIM), lambda c, i: (c, 0, 0))],
        out_specs=pl.BlockSpec((1, tr, HEAD_DIM), lambda c, i: (c, i, 0)),
        compiler_params=_params(("parallel", "parallel")),
        name="compress",
    )(xblk, pos.reshape(2, 1, kdim), w1.astype(BF), w2.astype(BF))


def _head_slopes(rows, rows_per_head, heads):
    r = lax.broadcasted_iota(jnp.int32, (rows, 1), 0)
    s = jnp.full((rows, 1), 0.0, F32)
    for g, h in enumerate(heads):
        s = jnp.where((r >= g * rows_per_head) & (r < (g + 1) * rows_per_head), ALIBI_SLOPES[h], s)
    return s


def _masked_softmax(s, valid):
    s = jnp.where(valid, s, NEG)
    m = jnp.max(s, axis=-1, keepdims=True)
    e = jnp.where(valid, jnp.exp(s - m), 0.0)
    return e / jnp.maximum(jnp.sum(e, axis=-1, keepdims=True), 1e-30)


def _select_blocks(score, blk_f, n_pick):
    sel = jnp.zeros(score.shape, F32)
    for _ in range(n_pick):
        m = jnp.max(score, axis=-1, keepdims=True)
        idx = jnp.min(jnp.where(score == m, blk_f, 1.0e9), axis=-1, keepdims=True)
        hit = blk_f == idx
        sel = jnp.where(hit & (m > -jnp.inf), 1.0, sel)
        score = jnp.where(hit, -jnp.inf, score)
    return sel


def _nsa_prompt_kernel(q_ref, gate_ref, ck_ref, cv_ref, ks_ref, vs_ref, kw_ref, vw_ref, o_ref, *, nb):
    c = pl.program_id(1)
    start = c * Q_BLOCK
    rows = NSA_GROUP * Q_BLOCK
    sig = jax.nn.sigmoid(gate_ref[...])
    row = lax.broadcasted_iota(jnp.int32, (rows, 1), 0)
    qpos = start + (row & (Q_BLOCK - 1))
    qpos_f = qpos.astype(F32)
    blk = lax.broadcasted_iota(jnp.int32, (1, nb), 1)
    blk_f = blk.astype(F32)
    blk_end = blk * BLOCK + (BLOCK - 1)
    ck = ck_ref[0]
    cv = cv_ref[0]
    qpos1 = qpos[0:Q_BLOCK]
    cur = qpos1 >> 6
    forced = (blk == 0) | ((blk <= cur) & (blk > cur - N_LOCAL_BLOCKS))
    n_tiles = (start + Q_BLOCK + SEL_KEY_TILE - 1) // SEL_KEY_TILE
    kcol = lax.broadcasted_iota(jnp.int32, (1, SEL_KEY_TILE), 1)
    e_row = lax.broadcasted_iota(jnp.int32, (nb, SEL_KEY_TILE), 0)
    e_col = lax.broadcasted_iota(jnp.int32, (nb, SEL_KEY_TILE), 1) >> 6
    w0 = pl.multiple_of(jnp.maximum(start - WINDOW, 0), Q_BLOCK)
    wpos = w0 + lax.broadcasted_iota(jnp.int32, (1, WIN_KEYS), 1)

    for kh in range(NSA_KV_HEADS):
        heads = [kh * NSA_GROUP + g for g in range(NSA_GROUP)]
        slope = _head_slopes(rows, Q_BLOCK, heads)
        qs = jnp.concatenate([q_ref[:, h * LANES:(h + 1) * LANES] for h in heads], axis=0)

        s_c = _dot_nt(qs, ck) - slope * (qpos_f - blk_end.astype(F32))
        p_c = _masked_softmax(s_c, blk_end <= qpos)
        o_c = _dot(p_c.astype(BF), cv)
        imp = p_c[0:Q_BLOCK]
        for g in range(1, NSA_GROUP):
            imp = imp + p_c[g * Q_BLOCK:(g + 1) * Q_BLOCK]
        score = jnp.where(forced, FORCE_SCORE, imp)
        score = jnp.where(blk <= cur, score, -jnp.inf)
        sel = _select_blocks(score, blk_f, min(N_SEL, nb)).astype(BF)

        def sel_step(j, carry, qs=qs, slope=slope, sel=sel):
            m_i, l_i, acc = carry
            k0 = pl.multiple_of(j * SEL_KEY_TILE, SEL_KEY_TILE)
            kt = ks_ref[pl.ds(k0, SEL_KEY_TILE), :]
            vt = vs_ref[pl.ds(k0, SEL_KEY_TILE), :]
            expand = jnp.where(e_row == e_col + j * (SEL_KEY_TILE // BLOCK), 1.0, 0.0).astype(BF)
            selx = _dot(sel, expand)
            selx = jnp.concatenate([selx] * NSA_GROUP, axis=0)
            kpos = k0 + kcol
            valid = (selx > 0.5) & (kpos <= qpos)
            s = _dot_nt(qs, kt) + slope * (kpos.astype(F32) - qpos_f)
            s = jnp.where(valid, s, NEG)
            m_new = jnp.maximum(m_i, jnp.max(s, axis=-1, keepdims=True))
            p = jnp.where(valid, jnp.exp(s - m_new), 0.0)
            alpha = jnp.exp(m_i - m_new)
            l_new = alpha * l_i + jnp.sum(p, axis=-1, keepdims=True)
            acc = alpha * acc + _dot(p.astype(BF), vt)
            return m_new, l_new, acc

        init = (jnp.full((rows, 1), NEG, F32), jnp.zeros((rows, 1), F32), jnp.zeros((rows, LANES), F32))
        _, l_s, acc_s = lax.fori_loop(0, n_tiles, sel_step, init)
        o_s = acc_s / jnp.maximum(l_s, 1e-30)

        kwt = kw_ref[pl.ds(w0, WIN_KEYS), :]
        vwt = vw_ref[pl.ds(w0, WIN_KEYS), :]
        dist = qpos - wpos
        s_w = _dot_nt(qs, kwt) - slope * dist.astype(F32)
        valid_w = (dist >= 0) & (dist < WINDOW)
        s_w = jnp.where(valid_w, s_w, NEG)
        m_w = jnp.max(s_w, axis=-1, keepdims=True)
        e_w = jnp.where(valid_w, jnp.exp(s_w - m_w), 0.0)
        o_w = _dot(e_w.astype(BF), vwt) / jnp.maximum(jnp.sum(e_w, axis=-1, keepdims=True), 1e-30)

        for g, h in enumerate(heads):
            rs = slice(g * Q_BLOCK, (g + 1) * Q_BLOCK)
            o = (sig[:, 3 * h:3 * h + 1] * o_c[rs] + sig[:, 3 * h + 1:3 * h + 2] * o_s[rs]
                 + sig[:, 3 * h + 2:3 * h + 3] * o_w[rs])
            o_ref[:, h * LANES:(h + 1) * LANES] = o.astype(BF)


def _nsa_prompt(q, gates, ck, cv, kvb, batch, t):
    nb = t // BLOCK
    nc = t // Q_BLOCK
    m = batch * t
    kv_spec = lambda col: pl.BlockSpec((t, LANES), lambda b, c: (b, col))
    return pl.pallas_call(
        functools.partial(_nsa_prompt_kernel, nb=nb),
        out_shape=jax.ShapeDtypeStruct((m, Q_PAD_W), BF),
        grid=(batch, nc),
        in_specs=[pl.BlockSpec((Q_BLOCK, Q_PAD_W), lambda b, c: (b * nc + c, 0)),
                  pl.BlockSpec((Q_BLOCK, LANES), lambda b, c: (b * nc + c, 0)),
                  pl.BlockSpec((1, nb, LANES), lambda b, c: (b, 0, 0)),
                  pl.BlockSpec((1, nb, LANES), lambda b, c: (b, 0, 0)),
                  kv_spec(2), kv_spec(3), kv_spec(4), kv_spec(5)],
        out_specs=pl.BlockSpec((Q_BLOCK, Q_PAD_W), lambda b, c: (b * nc + c, 0)),
        compiler_params=_params(("parallel", "arbitrary")),
        name="nsa_prompt",
    )(q, gates, ck, cv, kvb, kvb, kvb, kvb)


SAMPLE_ROWS = 16


def _nsa_sample_kernel(q_ref, gate_ref, ck_ref, cv_ref, past_ref, new_ref, win_ref, wnew_ref, o_ref, *, past):
    rows = SAMPLE_ROWS
    nbp = ck_ref.shape[1]
    pos = past
    q = q_ref[0].astype(BF)
    qf = q.astype(F32)
    row = lax.broadcasted_iota(jnp.int32, (rows, 1), 0)
    slope = _head_slopes(rows, 1, list(range(NSA_HEADS)))
    sig = jax.nn.sigmoid(gate_ref[0])
    blk = lax.broadcasted_iota(jnp.int32, (1, nbp), 1)
    blk_f = blk.astype(F32)
    blk_end = blk * BLOCK + (BLOCK - 1)

    s_c = _dot_nt(q, ck_ref[0]) - slope * (float(pos) - blk_end.astype(F32))
    p_c = _masked_softmax(s_c, blk_end <= pos)
    o_c = _dot(p_c.astype(BF), cv_ref[0])
    grp0 = row < NSA_GROUP
    imp0 = jnp.sum(jnp.where(grp0, p_c, 0.0), axis=0, keepdims=True)
    imp1 = jnp.sum(jnp.where((row >= NSA_GROUP) & (row < NSA_HEADS), p_c, 0.0), axis=0, keepdims=True)
    imp = jnp.where(grp0, imp0, imp1)
    cur = pos // BLOCK
    forced = (blk == 0) | ((blk <= cur) & (blk > cur - N_LOCAL_BLOCKS))
    score = jnp.where(forced, FORCE_SCORE, imp)
    score = jnp.where(blk <= cur, score, -jnp.inf)
    sel = _select_blocks(score, blk_f, N_SEL)

    def self_key(col0):
        k_new = new_ref[0, :, col0:col0 + LANES].astype(BF).astype(F32)
        return jnp.sum(qf * k_new, axis=-1, keepdims=True)

    def attend(s, valid, v, s_self, valid_self, v_self):
        s = jnp.where(valid, s, NEG)
        s_self = jnp.where(valid_self, s_self, NEG)
        m = jnp.maximum(jnp.max(s, axis=-1, keepdims=True), s_self)
        e = jnp.where(valid, jnp.exp(s - m), 0.0)
        e_self = jnp.where(valid_self, jnp.exp(s_self - m), 0.0)
        den = jnp.maximum(jnp.sum(e, axis=-1, keepdims=True) + e_self, 1e-30)
        num = _dot(e.astype(BF), v) + e_self.astype(BF).astype(F32) * v_self.astype(BF).astype(F32)
        return num / den

    ks = past_ref[0, :, 2 * LANES:3 * LANES].astype(BF)
    vs = past_ref[0, :, 3 * LANES:4 * LANES].astype(BF)
    e_row = lax.broadcasted_iota(jnp.int32, (nbp, past), 0)
    e_col = lax.broadcasted_iota(jnp.int32, (nbp, past), 1) >> 6
    selx = _dot(sel.astype(BF), jnp.where(e_row == e_col, 1.0, 0.0).astype(BF))
    kpos = lax.broadcasted_iota(jnp.int32, (1, past), 1)
    s_s = _dot_nt(q, ks) + slope * (kpos.astype(F32) - float(pos))
    sel_self = jnp.sum(jnp.where(blk == cur, sel, 0.0), axis=-1, keepdims=True) > 0.5
    o_s = attend(s_s, selx > 0.5, vs, self_key(2 * LANES), sel_self, new_ref[0, :, 3 * LANES:4 * LANES])

    n_buf = win_ref.shape[1]
    kw = win_ref[0, :, 0:LANES].astype(BF)
    vw = win_ref[0, :, LANES:2 * LANES].astype(BF)
    dist = pos - (pos - n_buf + lax.broadcasted_iota(jnp.int32, (1, n_buf), 1))
    s_w = _dot_nt(q, kw) - slope * dist.astype(F32)
    k_wn = wnew_ref[0, :, 0:LANES].astype(BF).astype(F32)
    s_wn = jnp.sum(qf * k_wn, axis=-1, keepdims=True)
    o_w = attend(s_w, (dist >= 0) & (dist < WINDOW), vw, s_wn, row >= 0, wnew_ref[0, :, LANES:2 * LANES])

    o_ref[0] = sig[:, 0:1] * o_c + sig[:, 1:2] * o_s + sig[:, 2:3] * o_w


def _nsa_sample(q, gates3, ck, cv, past_kv, kv_new, win_buf, win_new):
    b, past = past_kv.shape[0], past_kv.shape[1]
    nbp = ck.shape[1]
    n_buf = win_buf.shape[1]
    per = lambda shape: pl.BlockSpec((1,) + shape, lambda i: (i, 0, 0))
    return pl.pallas_call(
        functools.partial(_nsa_sample_kernel, past=past),
        out_shape=jax.ShapeDtypeStruct((b, SAMPLE_ROWS, LANES), F32),
        grid=(b,),
        in_specs=[per((SAMPLE_ROWS, LANES)), per((SAMPLE_ROWS, 3)), per((nbp, LANES)), per((nbp, LANES)),
                  per((past, 4 * LANES)), per((1, 4 * LANES)), per((n_buf, 2 * LANES)), per((1, 2 * LANES))],
        out_specs=per((SAMPLE_ROWS, LANES)),
        compiler_params=_params(("parallel",)),
        name="nsa_sample",
    )(q, gates3, ck, cv, past_kv, kv_new, win_buf, win_new)


def _rglru_gates(xc, gr_unused, wa_ref, ba_ref, wx_ref, bx_ref, lam_ref):
    xb = xc.astype(BF)
    r = jax.nn.sigmoid(_dot(xb, wa_ref[...]) + ba_ref[...])
    i = jax.nn.sigmoid(_dot(xb, wx_ref[...]) + bx_ref[...])
    z = -lam_ref[...]
    softplus = jnp.maximum(z, 0.0) + jnp.log1p(jnp.exp(-jnp.abs(z)))
    log_a = -RG_C * r * softplus
    th = jnp.tanh(log_a)
    mult = jnp.sqrt(-2.0 * th / (1.0 - th))
    return jnp.exp(log_a), i, mult


def _shift_rows(x, s, fill):
    n = x.shape[0]
    if s % 8 == 0:
        return jnp.concatenate([jnp.full((s, x.shape[1]), fill, x.dtype), x[:n - s]], axis=0)
    r = lax.broadcasted_iota(jnp.int32, (n, 1), 0)
    return jnp.where(r < s, fill, pltpu.roll(x, s, 0))


def _rglru_prompt_kernel(xr_ref, gr_ref, cw_ref, cb_ref, wa_ref, ba_ref, wx_ref, bx_ref, lam_ref,
                         y_ref, hl_ref, buf_ref, h_ref):
    t = pl.program_id(1)
    tt = xr_ref.shape[0]

    @pl.when(t == 0)
    def _():
        buf_ref[0:HALO, :] = jnp.zeros((HALO, D_RNN), F32)
        h_ref[...] = jnp.zeros_like(h_ref)

    buf_ref[HALO:HALO + tt, :] = xr_ref[...]
    xc = cb_ref[...] + cw_ref[RNN_CONV - 1:RNN_CONV, :] * xr_ref[...]
    for j in range(1, RNN_CONV):
        xc = xc + cw_ref[RNN_CONV - 1 - j:RNN_CONV - j, :] * buf_ref[HALO - j:HALO - j + tt, :]
    a, i, mult = _rglru_gates(xc, None, wa_ref, ba_ref, wx_ref, bx_ref, lam_ref)
    row =lax.broadcasted_iota(jnp.int32, (tt, 1), 0)
    mult = jnp.where(row + t * tt == 0, 1.0, mult)
    b = xc * i * mult
    s = 1
    while s < tt:
        b = a * _shift_rows(b, s, 0.0) + b
        a = a * _shift_rows(a, s, 1.0)
        s *= 2
    h = a * h_ref[0:1, :] + b
    h_ref[0:1, :] = h[tt - 1:tt, :]
    hl_ref[0] = h[tt - 1:tt, :]
    y_ref[...] = (h * _gelu(gr_ref[...])).astype(BF)
    buf_ref[0:HALO, :] = buf_ref[tt:tt + HALO, :]


def _block_diag(w):
    n, c, d = w.shape
    eye = jnp.eye(n, dtype=w.dtype)
    return (w[:, :, None, :] * eye[:, None, :, None]).reshape(n * c, n * d)


def _rglru_prompt(xr, gr, pa, batch, t):
    tt = min(SCAN_TILE, t)
    nt = t // tt
    vec = lambda a: a.reshape(1, D_RNN)
    y, hl = pl.pallas_call(
        _rglru_prompt_kernel,
        out_shape=[jax.ShapeDtypeStruct((batch * t, D_RNN), BF), jax.ShapeDtypeStruct((batch, 1, D_RNN), F32)],
        grid=(batch, nt),
        in_specs=[pl.BlockSpec((tt, D_RNN), lambda b, i: (b * nt + i, 0)),
                  pl.BlockSpec((tt, D_RNN), lambda b, i: (b * nt + i, 0)),
                  _full((RNN_CONV, D_RNN)), _full((1, D_RNN)), _full((D_RNN, D_RNN)), _full((1, D_RNN)),
                  _full((D_RNN, D_RNN)), _full((1, D_RNN)), _full((1, D_RNN))],
        out_specs=[pl.BlockSpec((tt, D_RNN), lambda b, i: (b * nt + i, 0)),
                   pl.BlockSpec((1, 1, D_RNN), lambda b, i: (b, 0, 0))],
        scratch_shapes=[pltpu.VMEM((HALO + tt, D_RNN), F32), pltpu.VMEM((8, D_RNN), F32)],
        compiler_params=_params(("arbitrary", "arbitrary")),
        name="rglru_prompt",
    )(xr, gr, pa['conv_w'], vec(pa['conv_b']), _block_diag(pa['wa']).astype(BF), vec(pa['ba']),
      _block_diag(pa['wx']).astype(BF), vec(pa['bx']), vec(pa['lam']))
    return y, hl.reshape(batch, D_RNN)


def _rglru_sample_kernel(xr_ref, gr_ref, prev_ref, h0_ref, cw_ref, cb_ref, wa_ref, ba_ref, wx_ref, bx_ref, lam_ref,
                         y_ref, h_ref):
    xc = cb_ref[...] + cw_ref[RNN_CONV - 1:RNN_CONV, :] * xr_ref[...]
    for k in range(RNN_CONV - 1):
        xc = xc + cw_ref[k:k + 1, :] * prev_ref[k]
    a, i, mult = _rglru_gates(xc, None, wa_ref, ba_ref, wx_ref, bx_ref, lam_ref)
    h = xc * i * mult + a * h0_ref[...]
    h_ref[...] = h
    y_ref[...] = (h * _gelu(gr_ref[...])).astype(BF)


def _rglru_sample(xr, gr, h0, conv_prev, pa):
    b = xr.shape[0]
    vec = lambda a: a.reshape(1, D_RNN)
    return pl.pallas_call(
        _rglru_sample_kernel,
        out_shape=[jax.ShapeDtypeStruct((b, D_RNN), BF), jax.ShapeDtypeStruct((b, D_RNN), F32)],
        name="rglru_sample",
    )(xr, gr, conv_prev.transpose(1, 0, 2), h0, pa['conv_w'], vec(pa['conv_b']),
      _block_diag(pa['wa']).astype(BF), vec(pa['ba']), _block_diag(pa['wx']).astype(BF), vec(pa['bx']),
      vec(pa['lam']))


def _mm2_ln_kernel(a_ref, b_ref, wa_ref, wb_ref, x_ref, g_ref, beta_ref, o_ref):
    y = _dot(a_ref[...], wa_ref[...]) + _dot(b_ref[...], wb_ref[...])
    o_ref[...] = _ln(DN_ALPHA * x_ref[...] + y, g_ref[...], beta_ref[...])


def _mm2_ln(a, b, wa, wb, x, g, beta):
    m = x.shape[0]
    tm = _row_tile(m)
    ka, kb = a.shape[1], b.shape[1]
    return pl.pallas_call(
        _mm2_ln_kernel,
        out_shape=jax.ShapeDtypeStruct((m, D_MODEL), F32),
        grid=(m // tm,),
        in_specs=[pl.BlockSpec((tm, ka), lambda i: (i, 0)), pl.BlockSpec((tm, kb), lambda i: (i, 0)),
                  _full((ka, D_MODEL)), _full((kb, D_MODEL)), pl.BlockSpec((tm, D_MODEL), lambda i: (i, 0)),
                  _full((1, D_MODEL)), _full((1, D_MODEL))],
        out_specs=pl.BlockSpec((tm, D_MODEL), lambda i: (i, 0)),
        compiler_params=_params(("parallel",)),
        name="mm2_ln",
    )(a, b, wa, wb, x, g.reshape(1, -1), beta.reshape(1, -1))


def _mm_kernel(x_ref, w_ref, o_ref, ob_ref):
    y = _dot(x_ref[...].astype(BF), w_ref[0])
    o_ref[0] = y
    ob_ref[0] = y.astype(BF)


def _memory_kv(mem, wkv):
    r = mem.shape[0]
    nl, _, n = wkv.shape
    return pl.pallas_call(
        _mm_kernel,
        out_shape=[jax.ShapeDtypeStruct((nl, r, n), F32), jax.ShapeDtypeStruct((nl, r, n), BF)],
        grid=(nl,),
        in_specs=[_full((r, D_MODEL)), pl.BlockSpec((1, D_MODEL, n), lambda l: (l, 0, 0))],
        out_specs=[pl.BlockSpec((1, r, n), lambda l: (l, 0, 0)), pl.BlockSpec((1, r, n), lambda l: (l, 0, 0))],
        compiler_params=_params(("parallel",)),
        name="memory_kv",
    )(mem, wkv.astype(BF))


def _xattn_prompt_kernel(x_ref, k_ref, v_ref, wq_ref, wo_ref, g_ref, beta_ref, o_ref):
    x = x_ref[...]
    q = _dot(x.astype(BF), wq_ref[...]).astype(BF)
    outs = []
    for h in range(X_HEADS):
        hs = slice(h * X_HEAD_DIM, (h + 1) * X_HEAD_DIM)
        s = _dot_nt(q[:, hs], k_ref[0][:, hs]) * (X_HEAD_DIM ** -0.5)
        m = jnp.max(s, axis=-1, keepdims=True)
        e = jnp.exp(s - m)
        p = e / jnp.sum(e, axis=-1, keepdims=True)
        outs.append(_dot(p.astype(BF), v_ref[0][:, hs]).astype(BF))
    y = _dot(jnp.concatenate(outs, axis=1), wo_ref[...])
    o_ref[...] = _ln(DN_ALPHA * x + y, g_ref[...], beta_ref[...])


def _xattn_prompt(x, kb, vb, wq, wo, g, beta, batch, t):
    tm = _row_tile(t)
    nt = t // tm
    return pl.pallas_call(
        _xattn_prompt_kernel,
        out_shape=jax.ShapeDtypeStruct((batch * t, D_MODEL), F32),
        grid=(batch, nt),
        in_specs=[pl.BlockSpec((tm, D_MODEL), lambda b, i: (b * nt + i, 0)),
                  pl.BlockSpec((1, MEM_LEN, X_W), lambda b, i: (b, 0, 0)),
                  pl.BlockSpec((1, MEM_LEN, X_W), lambda b, i: (b, 0, 0)),
                  _full((D_MODEL, X_W)), _full((X_W, D_MODEL)), _full((1, D_MODEL)), _full((1, D_MODEL))],
        out_specs=pl.BlockSpec((tm, D_MODEL), lambda b, i: (b * nt + i, 0)),
        compiler_params=_params(("parallel", "parallel")),
        name="xattn_prompt",
    )(x, kb, vb, wq, wo, g.reshape(1, -1), beta.reshape(1, -1))


XS_SEQS = 8


def _xattn_sample_kernel(x_ref, kv_ref, wq_ref, wo_ref, g_ref, beta_ref, o_ref):
    x = x_ref[...]
    q = _dot(x.astype(BF), wq_ref[...]).astype(BF).astype(F32)
    rows = []
    for s_i in range(XS_SEQS):
        k = kv_ref[s_i, :, 0:X_W].astype(BF).astype(F32)
        v = kv_ref[s_i, :, X_W:2 * X_W].astype(BF).astype(F32)
        prod = k * q[s_i:s_i + 1, :]
        outs = []
        for h in range(X_HEADS):
            hs = slice(h * X_HEAD_DIM, (h + 1) * X_HEAD_DIM)
            s = jnp.sum(prod[:, hs], axis=-1, keepdims=True) * (X_HEAD_DIM ** -0.5)
            m = jnp.max(s, axis=0, keepdims=True)
            e = jnp.exp(s - m)
            p = (e / jnp.sum(e, axis=0, keepdims=True)).astype(BF).astype(F32)
            outs.append(jnp.sum(p * v[:, hs], axis=0, keepdims=True))
        rows.append(jnp.concatenate(outs, axis=1))
    o = jnp.concatenate(rows, axis=0)
    y = _dot(o.astype(BF), wo_ref[...])
    o_ref[...] = _ln(DN_ALPHA * x + y, g_ref[...], beta_ref[...])


def _xattn_sample(x, kv, wq, wo, g, beta):
    b = x.shape[0]
    return pl.pallas_call(
        _xattn_sample_kernel,
        out_shape=jax.ShapeDtypeStruct((b, D_MODEL), F32),
        grid=(b // XS_SEQS,),
        in_specs=[pl.BlockSpec((XS_SEQS, D_MODEL), lambda i: (i, 0)),
                  pl.BlockSpec((XS_SEQS, MEM_LEN, 2 * X_W), lambda i: (i, 0, 0)),
                  _full((D_MODEL, X_W)), _full((X_W, D_MODEL)), _full((1, D_MODEL)), _full((1, D_MODEL))],
        out_specs=pl.BlockSpec((XS_SEQS, D_MODEL), lambda i: (i, 0)),
        compiler_params=_params(("parallel",)),
        name="xattn_sample",
    )(x, kv, wq, wo, g.reshape(1, -1), beta.reshape(1, -1))


def _swiglu_rows(xb, wg_ref, wu_ref, wd_ref, lead):
    acc = None
    for c in range(D_FF // FF_CHUNK):
        cs = slice(c * FF_CHUNK, (c + 1) * FF_CHUNK)
        hg = _dot(xb, wg_ref[lead + (slice(None), cs)])
        hu = _dot(xb, wu_ref[lead + (slice(None), cs)])
        part = _dot((_silu(hg) * hu).astype(BF), wd_ref[lead + (cs, slice(None))])
        acc = part if acc is None else acc + part
    return acc


def _swiglu_ln_kernel(x_ref, wg_ref, wu_ref, wd_ref, g_ref, beta_ref, o_ref):
    x = x_ref[...]
    y = _swiglu_rows(x.astype(BF), wg_ref, wu_ref, wd_ref, ())
    o_ref[...] = _ln(DN_ALPHA * x + y, g_ref[...], beta_ref[...])


def _swiglu_ln(x, wg, wu, wd, g, beta):
    m = x.shape[0]
    tm = _row_tile(m)
    return pl.pallas_call(
        _swiglu_ln_kernel,
        out_shape=jax.ShapeDtypeStruct((m, D_MODEL), F32),
        grid=(m // tm,),
        in_specs=[pl.BlockSpec((tm, D_MODEL), lambda i: (i, 0)), _full((D_MODEL, D_FF)), _full((D_MODEL, D_FF)),
                  _full((D_FF, D_MODEL)), _full((1, D_MODEL)), _full((1, D_MODEL))],
        out_specs=pl.BlockSpec((tm, D_MODEL), lambda i: (i, 0)),
        compiler_params=_params(("parallel",)),
        name="swiglu_ln",
    )(x, wg, wu, wd, g.reshape(1, -1), beta.reshape(1, -1))


def _router_kernel(x_ref, r_ref, idx_ref, w_ref):
    logits = jnp.dot(x_ref[...], r_ref[...], preferred_element_type=F32, precision=lax.Precision.HIGHEST)
    lane = lax.broadcasted_iota(jnp.int32, logits.shape, 1)
    lane_f = lane.astype(F32)
    logits = jnp.where(lane < N_EXPERTS, logits, -jnp.inf)
    m1 = jnp.max(logits, axis=-1, keepdims=True)
    i1 = jnp.min(jnp.where(logits == m1, lane_f, 1.0e9), axis=-1, keepdims=True)
    rest = jnp.where(lane_f == i1, -jnp.inf, logits)
    m2 = jnp.max(rest, axis=-1, keepdims=True)
    i2 = jnp.min(jnp.where(rest == m2, lane_f, 1.0e9), axis=-1, keepdims=True)
    e2 = jnp.exp(m2 - m1)
    den = 1.0 + e2
    idx_ref[...] = jnp.where(lane == 0, i1, i2).astype(jnp.int32)
    w_ref[...] = jnp.where(lane == 0, 1.0 / den, e2 / den)


def _router(x, router):
    m = x.shape[0]
    tm = _row_tile(m, 384)
    rp = jnp.pad(router, ((0, 0), (0, LANES - N_EXPERTS)))
    idx, w = pl.pallas_call(
        _router_kernel,
        out_shape=[jax.ShapeDtypeStruct((m, LANES), jnp.int32), jax.ShapeDtypeStruct((m, LANES), F32)],
        grid=(m // tm,),
        in_specs=[pl.BlockSpec((tm, D_MODEL), lambda i: (i, 0)), _full((D_MODEL, LANES))],
        out_specs=[pl.BlockSpec((tm, LANES), lambda i: (i, 0)), pl.BlockSpec((tm, LANES), lambda i: (i, 0))],
        compiler_params=_params(("parallel",)),
        name="router",
    )(x, rp)
    return idx[:, :TOP_K], w[:, :TOP_K]


def _moe_kernel(te_ref, nu_ref, x_ref, s_ref, wg_ref, wu_ref, wd_ref, o_ref):
    i = pl.program_id(0)

    @pl.when(i < nu_ref[0])
    def _():
        o_ref[...] = s_ref[...] * _swiglu_rows(x_ref[...], wg_ref, wu_ref, wd_ref, (0,))

    @pl.when(i >= nu_ref[0])
    def _():
        o_ref[...] = jnp.zeros_like(o_ref)


def _moe_ffn(tile_expert, n_used, xs, scale, wg, wu, wd):
    p = xs.shape[0]
    return pl.pallas_call(
        _moe_kernel,
        out_shape=jax.ShapeDtypeStruct((p, D_MODEL), F32),
        grid_spec=pltpu.PrefetchScalarGridSpec(
            num_scalar_prefetch=2,
            grid=(p // MOE_TILE,),
            in_specs=[pl.BlockSpec((MOE_TILE, D_MODEL), lambda i, te, nu: (i, 0)),
                      pl.BlockSpec((MOE_TILE, 1), lambda i, te, nu: (i, 0)),
                      pl.BlockSpec((1, D_MODEL, D_FF), lambda i, te, nu: (te[i], 0, 0)),
                      pl.BlockSpec((1, D_MODEL, D_FF), lambda i, te, nu: (te[i], 0, 0)),
                      pl.BlockSpec((1, D_FF, D_MODEL), lambda i, te, nu: (te[i], 0, 0))],
            out_specs=pl.BlockSpec((MOE_TILE, D_MODEL), lambda i, te, nu: (i, 0))),
        compiler_params=_params(("arbitrary",)),
        name="moe_ffn",
    )(tile_expert, n_used, xs, scale, wg, wu, wd)


def _add2_ln_kernel(x_ref, y_ref, g_ref, beta_ref, o_ref):
    y = y_ref[:, 0:D_MODEL] + y_ref[:, D_MODEL:2 * D_MODEL]
    o_ref[...] = _ln(DN_ALPHA * x_ref[...] + y, g_ref[...], beta_ref[...])


def _add2_ln(x, y2, g, beta):
    m = x.shape[0]
    tm = _row_tile(m, 384)
    return pl.pallas_call(
        _add2_ln_kernel,
        out_shape=jax.ShapeDtypeStruct((m, D_MODEL), F32),
        grid=(m // tm,),
        in_specs=[pl.BlockSpec((tm, D_MODEL), lambda i: (i, 0)), pl.BlockSpec((tm, 2 * D_MODEL), lambda i: (i, 0)),
                  _full((1, D_MODEL)), _full((1, D_MODEL))],
        out_specs=pl.BlockSpec((tm, D_MODEL), lambda i: (i, 0)),
        compiler_params=_params(("parallel",)),
        name="add2_ln",
    )(x, y2, g.reshape(1, -1), beta.reshape(1, -1))


def _moe_layer(x, router, wg, wu, wd, g, beta):
    m = x.shape[0]
    top_i, top_w = _router(x, router)
    e_flat = top_i.reshape(-1)
    onehot = (e_flat[:, None] == jnp.arange(N_EXPERTS, dtype=jnp.int32)[None, :]).astype(jnp.int32)
    cum = jnp.cumsum(onehot, axis=0)
    rank = jnp.sum(onehot * (cum - 1), axis=1)
    counts = cum[-1]
    padded = ((counts + MOE_TILE - 1) // MOE_TILE) * MOE_TILE
    ends = jnp.cumsum(padded)
    starts = ends - padded
    pos = starts[e_flat] + rank
    n_tiles = (TOP_K * m + N_EXPERTS * (MOE_TILE - 1)) // MOE_TILE + 1
    p_rows = n_tiles * MOE_TILE
    tile_start = jnp.arange(n_tiles, dtype=jnp.int32) * MOE_TILE
    tile_expert = jnp.minimum(jnp.sum((tile_start[:, None] >= ends[None, :]).astype(jnp.int32), axis=1),
                              N_EXPERTS - 1).astype(jnp.int32)
    n_used = (ends[-1] // MOE_TILE).astype(jnp.int32).reshape(1)
    src = jnp.zeros((p_rows,), jnp.int32).at[pos].set(jnp.arange(TOP_K * m, dtype=jnp.int32) // TOP_K)
    scale = jnp.zeros((p_rows,), F32).at[pos].set(top_w.reshape(-1))
    xs = x.astype(BF)[src]
    ys = _moe_ffn(tile_expert, n_used, xs, scale.reshape(p_rows, 1), wg, wu, wd)
    y2 = ys[pos].reshape(m, TOP_K * D_MODEL)
    return _add2_ln(x, y2, g, beta)


def _glu_kernel(x_ref, w_ref, b_ref, o_ref):
    xb = x_ref[...].astype(BF)
    a = _dot(xb, w_ref[:, 0:D_MODEL]) + b_ref[:, 0:D_MODEL]
    gate = _dot(xb, w_ref[:, D_MODEL:2 * D_MODEL]) + b_ref[:, D_MODEL:2 * D_MODEL]
    o_ref[...] = a * jax.nn.sigmoid(gate)


def _glu(x, w, b):
    m = x.shape[0]
    tm = _row_tile(m)
    return pl.pallas_call(
        _glu_kernel,
        out_shape=jax.ShapeDtypeStruct((m, D_MODEL), F32),
        grid=(m // tm,),
        in_specs=[pl.BlockSpec((tm, D_MODEL), lambda i: (i, 0)), _full((D_MODEL, 2 * D_MODEL)),
                  _full((1, 2 * D_MODEL))],
        out_specs=pl.BlockSpec((tm, D_MODEL), lambda i: (i, 0)),
        compiler_params=_params(("parallel",)),
        name="glu",
    )(x, w, b.reshape(1, -1))


CONV_ROWS = 64


def _conv_tail(c, x, cg_ref, cb_ref, wp_ref, bp_ref, g_ref, beta_ref):
    c = _ln(c, cg_ref[...], cb_ref[...])
    y = _dot(_silu(c).astype(BF), wp_ref[...]) + bp_ref[...]
    return _ln(DN_ALPHA * x + y, g_ref[...], beta_ref[...])


def _conv_prompt_kernel(gl_ref, x_ref, dw_ref, db_ref, cg_ref, cb_ref, wp_ref, bp_ref, g_ref, beta_ref,
                        o_ref, buf_ref):
    t = pl.program_id(1)
    tt = gl_ref.shape[0]

    @pl.when(t == 0)
    def _():
        buf_ref[0:HALO, :] = jnp.zeros((HALO, D_MODEL), F32)

    buf_ref[HALO:HALO + tt, :] = gl_ref[...]
    off = HALO - (CONF_K - 1)
    for r0 in range(0, tt, CONV_ROWS):
        c = db_ref[...] + dw_ref[0:1, :] * buf_ref[r0 + off:r0 + off + CONV_ROWS, :]
        for k in range(1, CONF_K):
            c = c + dw_ref[k:k + 1, :] * buf_ref[r0 + off + k:r0 + off + k + CONV_ROWS, :]
        o_ref[r0:r0 + CONV_ROWS, :] = _conv_tail(c, x_ref[r0:r0 + CONV_ROWS, :], cg_ref, cb_ref, wp_ref, bp_ref,
                                                 g_ref, beta_ref)
    buf_ref[0:HALO, :] = buf_ref[tt:tt + HALO, :]


def _conv_prompt(gl, x, cp, g, beta, batch, t):
    tt = min(SCAN_TILE, t)
    nt = t // tt
    vec = lambda a: a.reshape(1, -1)
    return pl.pallas_call(
        _conv_prompt_kernel,
        out_shape=jax.ShapeDtypeStruct((batch * t, D_MODEL), F32),
        grid=(batch, nt),
        in_specs=[pl.BlockSpec((tt, D_MODEL), lambda b, i: (b * nt + i, 0)),
                  pl.BlockSpec((tt, D_MODEL), lambda b, i: (b * nt + i, 0)),
                  _full((CONF_K, D_MODEL)), _full((1, D_MODEL)), _full((1, D_MODEL)), _full((1, D_MODEL)),
                  _full((D_MODEL, D_MODEL)), _full((1, D_MODEL)), _full((1, D_MODEL)), _full((1, D_MODEL))],
        out_specs=pl.BlockSpec((tt, D_MODEL), lambda b, i: (b * nt + i, 0)),
        scratch_shapes=[pltpu.VMEM((HALO + tt, D_MODEL), F32)],
        compiler_params=_params(("arbitrary", "arbitrary")),
        name="conv_prompt",
    )(gl, x, cp['dw_w'], vec(cp['dw_b']), vec(cp['ln_g']), vec(cp['ln_b']), cp['w_pw'], vec(cp['b_pw']),
      vec(g), vec(beta))


CS_SEQS = 32


def _conv_sample_kernel(gl_ref, st_ref, x_ref, dw_ref, db_ref, cg_ref, cb_ref, wp_ref, bp_ref, g_ref, beta_ref,
                        o_ref):
    c = db_ref[...] + dw_ref[CONF_K - 1:CONF_K, :] * gl_ref[...]
    c = c + jnp.sum(st_ref[...] * dw_ref[0:CONF_K - 1, :][None, :, :], axis=1)
    o_ref[...] = _conv_tail(c, x_ref[...], cg_ref, cb_ref, wp_ref, bp_ref, g_ref, beta_ref)


def _conv_sample(gl, state, x, cp, g, beta):
    b = x.shape[0]
    vec = lambda a: a.reshape(1, -1)
    return pl.pallas_call(
        _conv_sample_kernel,
        out_shape=jax.ShapeDtypeStruct((b, D_MODEL), F32),
        grid=(b // CS_SEQS,),
        in_specs=[pl.BlockSpec((CS_SEQS, D_MODEL), lambda i: (i, 0)),
                  pl.BlockSpec((CS_SEQS, CONF_K - 1, D_MODEL), lambda i: (i, 0, 0)),
                  pl.BlockSpec((CS_SEQS, D_MODEL), lambda i: (i, 0)),
                  _full((CONF_K, D_MODEL)), _full((1, D_MODEL)), _full((1, D_MODEL)), _full((1, D_MODEL)),
                  _full((D_MODEL, D_MODEL)), _full((1, D_MODEL)), _full((1, D_MODEL)), _full((1, D_MODEL))],
        out_specs=pl.BlockSpec((CS_SEQS, D_MODEL), lambda i: (i, 0)),
        compiler_params=_params(("parallel",)),
        name="conv_sample",
    )(gl, state, x, cp['dw_w'], vec(cp['dw_b']), vec(cp['ln_g']), vec(cp['ln_b']), cp['w_pw'], vec(cp['b_pw']),
      vec(g), vec(beta))


def _flatten_blocks(kv2, batch, nb):
    x = kv2.reshape(batch, nb, BLOCK, 2, NSA_KV_HEADS, HEAD_DIM)
    return x.transpose(3, 0, 1, 4, 2, 5).reshape(2, batch * nb * NSA_KV_HEADS, BLOCK * HEAD_DIM)


def _mixer_a_prompt(x, w_in_p, wo_nsa, wo_rnn, pa, g, beta, batch, t):
    nb = t // BLOCK
    q, kv4, kwin, kvb, xr, gr, gates = _proj_a(x, w_in_p)
    cmp = _compress(_flatten_blocks(kv4[:, 0:256], batch, nb), pa['cmp_pos'], pa['cmp_w1'], pa['cmp_w2'])
    cmp = cmp.reshape(2, batch, nb, LANES).astype(BF)
    o_nsa = _nsa_prompt(q, gates, cmp[0], cmp[1], kvb, batch, t)
    o_rnn, h_last = _rglru_prompt(xr, gr, pa, batch, t)
    x_new = _mm2_ln(o_nsa, o_rnn, wo_nsa, wo_rnn, x, g, beta)
    n_keep = min(WINDOW, t)
    kv_out = kv4.reshape(batch, t, 4, NSA_KV_HEADS, HEAD_DIM)
    win_out = kwin.reshape(batch, t, 2, NSA_KV_HEADS, HEAD_DIM)[:, t - n_keep:]
    conv_out = xr.reshape(batch, t, D_RNN)[:, t - (RNN_CONV - 1):]
    return x_new, kv_out, win_out, h_last, conv_out


def _mixer_a_sample(x, kv_pool, page_table, win_buf, h0, conv_prev, w_in_p, wo_nsa, wo_rnn, pa, g, beta):
    b = x.shape[0]
    n_pages, page = page_table.shape[1], kv_pool.shape[1]
    past = n_pages * page
    nb_past = past // BLOCK
    q, kv4, kwin, _, xr, gr, gates = _proj_a(x, w_in_p)
    past_kv = kv_pool.reshape(kv_pool.shape[0], page, 4 * LANES)[page_table].reshape(b, past, 4 * LANES)
    cmp = _compress(_flatten_blocks(past_kv[:, :, 0:256].reshape(b * past, 256), b, nb_past),
                    pa['cmp_pos'], pa['cmp_w1'], pa['cmp_w2'])
    cmp = cmp.reshape(2, b, nb_past, LANES)
    cmp = jnp.pad(cmp, ((0, 0), (0, 0), (0, LANES - nb_past), (0, 0))).astype(BF)
    q16 = jnp.pad(q.astype(F32).reshape(b, NSA_HEADS, LANES), ((0, 0), (0, SAMPLE_ROWS - NSA_HEADS), (0, 0)))
    g3 = jnp.pad(gates[:, :3 * NSA_HEADS].reshape(b, NSA_HEADS, 3), ((0, 0), (0, SAMPLE_ROWS - NSA_HEADS), (0, 0)))
    n_buf = win_buf.shape[1]
    o16 = _nsa_sample(q16, g3, cmp[0], cmp[1], past_kv, kv4.reshape(b, 1, 4 * LANES),
                      win_buf.reshape(b, n_buf, 2 * LANES), kwin.reshape(b, 1, 2 * LANES))
    o_nsa = o16[:, :NSA_HEADS].reshape(b, Q_PAD_W).astype(BF)
    o_rnn, h_new = _rglru_sample(xr, gr, h0, conv_prev, pa)
    x_new = _mm2_ln(o_nsa, o_rnn, wo_nsa, wo_rnn, x, g, beta)
    kv_out = kv4.reshape(b, 1, 4, NSA_KV_HEADS, HEAD_DIM)
    win_out = jnp.concatenate([win_buf[:, 1:], kwin.reshape(b, 1, 2, NSA_KV_HEADS, HEAD_DIM)], axis=1)
    conv_out = jnp.concatenate([conv_prev[:, 1:], xr[:, None, :]], axis=1)
    return x_new, kv_out, win_out, h_new, conv_out


def kernel(x_prompt, x_sample, cache_nsa_kv, cache_nsa_win, state_rglru_h, state_rglru_conv, state_conv,
           cache_mem_kv, page_table, mem_prompt, ln_g, ln_b, a_w_in, a_cmp_pos, a_cmp_w1, a_cmp_w2,
           a_conv_w, a_conv_b, a_gate_a_w, a_gate_a_b, a_gate_x_w, a_gate_x_b, a_lambda, a_w_out,
           c_w_glu, c_b_glu, c_dw_w, c_dw_b, c_ln_g, c_ln_b, c_w_pw, c_b_pw, x_wq, x_wkv, x_wo,
           f_w_gu, f_w_down, m_router, m_w_gu, m_w_down):
    batch, t, _ = x_prompt.shape
    bs = x_sample.shape[0]
    xp = x_prompt.reshape(batch * t, D_MODEL)
    xs = x_sample.reshape(bs, D_MODEL)
    mkv_f, mkv_b = _memory_kv(mem_prompt.reshape(batch * MEM_LEN, D_MODEL), x_wkv.reshape(DEPTH, D_MODEL, 2 * X_W))
    pk, pw, ph, pcv, pc = [], [], [], [], []
    sk, sw, sh, scv, sc = [], [], [], [], []
    for l in range(DEPTH):
        if l % 2 == 0:
            i = l // 2
            pa = {'cmp_pos': a_cmp_pos[i], 'cmp_w1': a_cmp_w1[i], 'cmp_w2': a_cmp_w2[i],
                  'conv_w': a_conv_w[i], 'conv_b': a_conv_b[i], 'wa': a_gate_a_w[i], 'ba': a_gate_a_b[i],
                  'wx': a_gate_x_w[i], 'bx': a_gate_x_b[i], 'lam': a_lambda[i]}
            w_in_p = _prep_w_in(a_w_in[i])
            wo_nsa, wo_rnn = _prep_w_out(a_w_out[i])
            xp, kv_p, win_p, h_p, cb_p = _mixer_a_prompt(xp, w_in_p, wo_nsa, wo_rnn, pa, ln_g[l, 0], ln_b[l, 0],
                                                         batch, t)
            xs, kv_s, win_s, h_s, cb_s = _mixer_a_sample(xs, cache_nsa_kv[i], page_table, cache_nsa_win[i],
                                                         state_rglru_h[i], state_rglru_conv[i], w_in_p, wo_nsa,
                                                         wo_rnn, pa, ln_g[l, 0], ln_b[l, 0])
            pk.append(kv_p); pw.append(win_p); ph.append(h_p); pcv.append(cb_p)
            sk.append(kv_s); sw.append(win_s); sh.append(h_s); scv.append(cb_s)
        else:
            j = l // 2
            cp = {'dw_w': c_dw_w[j], 'dw_b': c_dw_b[j], 'ln_g': c_ln_g[j], 'ln_b': c_ln_b[j],
                  'w_pw': c_w_pw[j].astype(BF), 'b_pw': c_b_pw[j]}
            w_glu = c_w_glu[j].astype(BF)
            gl_p = _glu(xp, w_glu, c_b_glu[j])
            gl_s = _glu(xs, w_glu, c_b_glu[j])
            xp = _conv_prompt(gl_p, xp, cp, ln_g[l, 0], ln_b[l, 0], batch, t)
            xs_new = _conv_sample(gl_s, state_conv[j], xs, cp, ln_g[l, 0], ln_b[l, 0])
            pc.append(gl_p.reshape(batch, t, D_MODEL)[:, t - (CONF_K - 1):])
            sc.append(jnp.concatenate([state_conv[j][:, 1:], gl_s[:, None, :]], axis=1))
            xs = xs_new
        wq = x_wq[l].astype(BF)
        wo = x_wo[l].astype(BF)
        kb = mkv_b[l].reshape(batch, MEM_LEN, 2 * X_W)
        xp = _xattn_prompt(xp, kb[:, :, :X_W], kb[:, :, X_W:], wq, wo, ln_g[l, 1], ln_b[l, 1], batch, t)
        xs = _xattn_sample(xs, cache_mem_kv[l].reshape(bs, MEM_LEN, 2 * X_W), wq, wo, ln_g[l, 1], ln_b[l, 1])
        if l % 2 == 0:
            i = l // 2
            wg = f_w_gu[i][:, 0].astype(BF)
            wu = f_w_gu[i][:, 1].astype(BF)
            wd = f_w_down[i].astype(BF)
            xp = _swiglu_ln(xp, wg, wu, wd, ln_g[l, 2], ln_b[l, 2])
            xs = _swiglu_ln(xs, wg, wu, wd, ln_g[l, 2], ln_b[l, 2])
        else:
            j = l // 2
            wg = m_w_gu[j][:, :, 0].astype(BF)
            wu = m_w_gu[j][:, :, 1].astype(BF)
            wd = m_w_down[j].astype(BF)
            xall = _moe_layer(jnp.concatenate([xp, xs], axis=0), m_router[j], wg, wu, wd, ln_g[l, 2], ln_b[l, 2])
            xp, xs = xall[:batch * t], xall[batch * t:]
    p_mem = mkv_f.reshape(DEPTH, batch, MEM_LEN, 2, X_HEADS, X_HEAD_DIM)
    return (xp.reshape(batch, t, D_MODEL), xs.reshape(bs, 1, D_MODEL), jnp.stack(pk), jnp.stack(pw),
            jnp.stack(ph), jnp.stack(pcv), jnp.stack(pc), p_mem,
            jnp.stack(sk), jnp.stack(sw), jnp.stack(sh), jnp.stack(scv), jnp.stack(sc))
```

```python
import functools

import jax
import jax.numpy as jnp
from jax import lax
from jax.experimental import pallas as pl
from jax.experimental.pallas import tpu as pltpu

F32 = jnp.float32
BF = jnp.bfloat16

D_MODEL = 1024
NSA_HEADS = 8
NSA_KV_HEADS = 2
NSA_GROUP = NSA_HEADS // NSA_KV_HEADS
HEAD_DIM = 64
BLOCK = 64
N_SEL = 8
N_LOCAL_BLOCKS = 2
WINDOW = 512
Q_BLOCK = 128
FORCE_SCORE = 1.0e4
D_RNN = 512
RNN_CONV = 4
RG_C = 8.0
CONF_K = 31
MEM_LEN = 256
X_HEADS = 4
X_HEAD_DIM = 128
X_W = X_HEADS * X_HEAD_DIM
D_FF = 2816
N_EXPERTS = 8
TOP_K = 2
LN_EPS = 1e-5
DEPTH = 4
DN_ALPHA = (2.0 * DEPTH) ** 0.25

LANES = 128
SEL_KEY_TILE = 512
WIN_KEYS = WINDOW + Q_BLOCK
FF_CHUNK = 256
MOE_TILE = 256
SCAN_TILE = 256
HALO = 32
VMEM_LIMIT = 56 * 1024 * 1024
NEG = -1.0e30
MASK_BIAS = -(2.0 ** 30)
ALIBI_SLOPES = tuple(2.0 ** (-8.0 * (i + 1) / NSA_HEADS) for i in range(NSA_HEADS))


def _dot(a, b):
    return jnp.dot(a, b, preferred_element_type=F32)


def _dot_nt(a, b):
    return lax.dot_general(a, b, (((1,), (1,)), ((), ())), preferred_element_type=F32)


def _ln(z, g, b):
    mu = jnp.mean(z, axis=-1, keepdims=True)
    zc = z - mu
    var = jnp.mean(zc * zc, axis=-1, keepdims=True)
    return zc * lax.rsqrt(var + LN_EPS) * g + b


def _gelu(x):
    return 0.5 * x * (1.0 + jnp.tanh(0.7978845608028654 * (x + 0.044715 * (x * x * x))))


def _silu(x):
    return x * jax.nn.sigmoid(x)


def _params(sem):
    return pltpu.CompilerParams(dimension_semantics=sem, vmem_limit_bytes=VMEM_LIMIT)


def _row_tile(m, pref=512):
    return pref if m % pref == 0 else m


def _full(shape):
    n = len(shape)
    return pl.BlockSpec(shape, lambda *_: (0,) * n)


Q_PAD_W = NSA_HEADS * LANES
KV_W = 6 * NSA_KV_HEADS * HEAD_DIM
ROW_W = Q_PAD_W + 2 * D_RNN + LANES
KA_W = 4 * LANES
POS_HI = HEAD_DIM
POS_LO = HEAD_DIM + 1


def _proj_a_kernel(x_ref, wrow_ref, wt_ref, qf_ref, *refs, row_kv):
    if row_kv:
        wkv_ref, q_ref, xr_ref, gr_ref, gate_ref, kvt_ref, kat_ref, vt_ref, kvrow_ref = refs
    else:
        q_ref, xr_ref, gr_ref, gate_ref, kvt_ref, kat_ref, vt_ref = refs
    i = pl.program_id(1)
    tm = x_ref.shape[0]
    xb = x_ref[...].astype(BF)
    q = _dot(xb, wrow_ref[:, 0:Q_PAD_W])
    q_ref[...] = (q * (HEAD_DIM ** -0.5) + qf_ref[...]).astype(BF)
    o = Q_PAD_W
    xr_ref[...] = _dot(xb, wrow_ref[:, o:o + D_RNN])
    o += D_RNN
    gr_ref[...] = _dot(xb, wrow_ref[:, o:o + D_RNN])
    o += D_RNN
    gate_ref[...] = _dot(xb, wrow_ref[:, o:o + LANES])
    kvt = _dot_nt(wt_ref[...], xb)
    kvt_ref[0] = kvt
    vt_ref[0, 0:LANES, :] = kvt[3 * LANES:4 * LANES].astype(BF)
    vt_ref[0, LANES:2 * LANES, :] = kvt[5 * LANES:6 * LANES].astype(BF)
    pos = i * tm + lax.broadcasted_iota(jnp.int32, (HEAD_DIM, tm), 1)
    frow = lax.broadcasted_iota(jnp.int32, (HEAD_DIM, tm), 0)
    feat = jnp.where(frow == 0, (pos >> 6).astype(F32), jnp.where(frow == 1, (pos & (BLOCK - 1)).astype(F32), 0.0))
    for n, src in enumerate((2 * LANES, 2 * LANES + HEAD_DIM, 4 * LANES, 4 * LANES + HEAD_DIM)):
        kat_ref[0, n * LANES:n * LANES + HEAD_DIM, :] = kvt[src:src + HEAD_DIM].astype(BF)
        kat_ref[0, n * LANES + HEAD_DIM:(n + 1) * LANES, :] = feat.astype(BF)
    if row_kv:
        kvrow_ref[...] = _dot(xb, wkv_ref[...])


def _proj_a(x, w, batch, t, row_kv=False):
    m = batch * t
    tm = _row_tile(t)
    nt = t // tm
    row = lambda w_: pl.BlockSpec((tm, w_), lambda b, i: (b * nt + i, 0))
    fm = lambda f: pl.BlockSpec((1, f, tm), lambda b, i: (b, 0, i))
    outs = [((m, Q_PAD_W), BF, row(Q_PAD_W)), ((m, D_RNN), F32, row(D_RNN)), ((m, D_RNN), F32, row(D_RNN)),
            ((m, LANES), F32, row(LANES)), ((batch, KV_W, t), F32, fm(KV_W)), ((batch, KA_W, t), BF, fm(KA_W)),
            ((batch, 2 * LANES, t), BF, fm(2 * LANES))]
    ins = [x, w['row'], w['t'], w['qf']]
    in_specs = [row(D_MODEL), _full((D_MODEL, ROW_W)), _full((KV_W, D_MODEL)), _full((1, Q_PAD_W))]
    if row_kv:
        ins.append(w['kv'])
        in_specs.append(_full((D_MODEL, KV_W)))
        outs.append(((m, KV_W), F32, row(KV_W)))
    return pl.pallas_call(
        functools.partial(_proj_a_kernel, row_kv=row_kv),
        out_shape=[jax.ShapeDtypeStruct(s, dt) for s, dt, _ in outs],
        grid=(batch, nt),
        in_specs=in_specs,
        out_specs=[sp for _, _, sp in outs],
        compiler_params=_params(("parallel", "parallel")),
        name="proj_a",
    )(*ins)


def _prep_w_in(w_in):
    wq = w_in[:, :512].reshape(D_MODEL, NSA_HEADS, HEAD_DIM)
    wq_p = jnp.concatenate([wq, jnp.zeros_like(wq)], axis=2).reshape(D_MODEL, Q_PAD_W)
    wkv = w_in[:, 512:1280]
    wg = jnp.pad(w_in[:, 1280:1304], ((0, 0), (0, LANES - 3 * NSA_HEADS)))
    wxr = w_in[:, 1304:1816]
    wgr = w_in[:, 1816:2328]
    lane = jnp.arange(Q_PAD_W) % LANES
    slope = jnp.repeat(jnp.asarray(ALIBI_SLOPES, F32), LANES)
    qf = jnp.where(lane == POS_HI, BLOCK * slope, jnp.where(lane == POS_LO, slope, 0.0)).reshape(1, Q_PAD_W)
    return {'row': jnp.concatenate([wq_p, wxr, wgr, wg], axis=1).astype(BF), 't': wkv.T.astype(BF),
            'kv': wkv.astype(BF), 'qf': qf}


def _prep_w_out(w_out):
    wn = w_out[:512].reshape(NSA_HEADS, HEAD_DIM, D_MODEL)
    z = jnp.zeros_like(wn)
    lo = jnp.concatenate([wn, z], axis=1)
    hi = jnp.concatenate([z, wn], axis=1)
    wn_p = jnp.concatenate([lo[:NSA_GROUP], hi[NSA_GROUP:]], axis=0).reshape(Q_PAD_W, D_MODEL)
    return wn_p.astype(BF), w_out[:512].astype(BF), w_out[512:].astype(BF)


def _compress_kernel(x_ref, pos_ref, w1_ref, w2_ref, o_ref):
    xb = (x_ref[0] + pos_ref[0]).astype(BF)
    h = _gelu(_dot(xb, w1_ref[0]))
    o_ref[0] = _dot(h.astype(BF), w2_ref[0])


def _compress(xblk, pos, w1, w2):
    r = xblk.shape[1]
    tr = _row_tile(r, 256)
    kdim = BLOCK * HEAD_DIM
    pos_t = pos.transpose(0, 2, 1).reshape(2, 1, kdim)
    w1_t = w1.reshape(2, BLOCK, HEAD_DIM, w1.shape[-1]).transpose(0, 2, 1, 3).reshape(2, kdim, w1.shape[-1])
    return pl.pallas_call(
        _compress_kernel,
        out_shape=jax.ShapeDtypeStruct((2, r, HEAD_DIM), F32),
        grid=(2, r // tr),
        in_specs=[pl.BlockSpec((1, tr, kdim), lambda c, i: (c, i, 0)),
                  pl.BlockSpec((1, 1, kdim), lambda c, i: (c, 0, 0)),
                  pl.BlockSpec((1, kdim, 128), lambda c, i: (c, 0, 0)),
                  pl.BlockSpec((1, 128, HEAD_DIM), lambda c, i: (c, 0, 0))],
        out_specs=pl.BlockSpec((1, tr, HEAD_DIM), lambda c, i: (c, i, 0)),
        compiler_params=_params(("parallel", "parallel")),
        name="compress",
    )(xblk, pos_t, w1_t.astype(BF), w2.astype(BF))


def _cmp_keys(ck, nb_pad):
    batch, nb = ck.shape[0], ck.shape[1]
    ck = jnp.pad(ck.transpose(0, 2, 1, 3), ((0, 0), (0, 0), (0, nb_pad - nb), (0, 0)))
    lane = jnp.arange(HEAD_DIM)[None, :]
    n = jnp.arange(nb_pad, dtype=F32)[:, None]
    ext = jnp.where(lane == 0, n, jnp.where(lane == 1, float(BLOCK - 1), 0.0))
    ext = jnp.broadcast_to(ext[None, None], (batch, NSA_KV_HEADS, nb_pad, HEAD_DIM))
    return jnp.concatenate([ck, ext], axis=-1).astype(BF)


def _head_slopes(rows, rows_per_head, heads):
    r = lax.broadcasted_iota(jnp.int32, (rows, 1), 0)
    s = jnp.full((rows, 1), 0.0, F32)
    for g, h in enumerate(heads):
        s = jnp.where((r >= g * rows_per_head) & (r < (g + 1) * rows_per_head), ALIBI_SLOPES[h], s)
    return s


def _masked_softmax(s, valid):
    s = jnp.where(valid, s, NEG)
    m = jnp.max(s, axis=-1, keepdims=True)
    e = jnp.where(valid, jnp.exp(s - m), 0.0)
    return e / jnp.maximum(jnp.sum(e, axis=-1, keepdims=True), 1e-30)


def _select_blocks(score, blk_f, n_pick):
    sel = jnp.zeros(score.shape, F32)
    for _ in range(n_pick):
        m = jnp.max(score, axis=-1, keepdims=True)
        idx = jnp.min(jnp.where(score == m, blk_f, 1.0e9), axis=-1, keepdims=True)
        hit = blk_f == idx
        sel = jnp.where(hit & (m > -jnp.inf), 1.0, sel)
        score = jnp.where(hit, -jnp.inf, score)
    return sel


def _nsa_prompt_kernel(q_ref, gate_ref, ck_ref, cv_ref, kat_ref, vt_ref, e_ref, o_ref, *, nb):
    c = pl.program_id(1)
    start = c * Q_BLOCK
    rows = NSA_GROUP * Q_BLOCK
    sig = jax.nn.sigmoid(gate_ref[...])
    row = lax.broadcasted_iota(jnp.int32, (rows, 1), 0)
    qpos = start + (row & (Q_BLOCK - 1))
    blk = lax.broadcasted_iota(jnp.int32, (1, nb), 1)
    blk_f = blk.astype(F32)
    blk_end = blk * BLOCK + (BLOCK - 1)
    cv = cv_ref[0]
    qpos1 = qpos[0:Q_BLOCK]
    cur = qpos1 >> 6
    forced = (blk == 0) | ((blk <= cur) & (blk > cur - N_LOCAL_BLOCKS))
    n_tiles = (start + Q_BLOCK + SEL_KEY_TILE - 1) // SEL_KEY_TILE
    kcol = lax.broadcasted_iota(jnp.int32, (1, SEL_KEY_TILE), 1)
    w0 = pl.multiple_of(jnp.maximum(start - WINDOW, 0), Q_BLOCK)
    dist = qpos1 - (w0 + lax.broadcasted_iota(jnp.int32, (1, WIN_KEYS), 1))
    wbias = jnp.where((dist >= 0) & (dist < WINDOW), 0.0, MASK_BIAS)
    wbias = jnp.concatenate([wbias] * NSA_GROUP, axis=0)

    for kh in range(NSA_KV_HEADS):
        heads = [kh * NSA_GROUP + g for g in range(NSA_GROUP)]
        qs = jnp.concatenate([q_ref[:, h * LANES:(h + 1) * LANES] for h in heads], axis=0)

        s_c = _dot_nt(qs, ck_ref[0, kh])
        p_c = _masked_softmax(s_c, blk_end <= qpos)
        o_c = _dot(p_c.astype(BF), cv)
        imp = p_c[0:Q_BLOCK]
        for g in range(1, NSA_GROUP):
            imp = imp + p_c[g * Q_BLOCK:(g + 1) * Q_BLOCK]
        score = jnp.where(forced, FORCE_SCORE, imp)
        score = jnp.where(blk <= cur, score, -jnp.inf)
        sel = _select_blocks(score, blk_f, min(N_SEL, nb))
        qm = ((sel - 1.0) * (-MASK_BIAS)).astype(BF)
        qsm = jnp.concatenate([qs, jnp.concatenate([qm] * NSA_GROUP, axis=0)], axis=1)

        def scores(k0, qsm=qsm, kh=kh):
            kt = kat_ref[0, kh * LANES:(kh + 1) * LANES, pl.ds(k0, SEL_KEY_TILE)]
            return _dot(qsm, jnp.concatenate([kt, e_ref[:, pl.ds(k0, SEL_KEY_TILE)]], axis=0))

        def update(s, carry, k0):
            m_i, l_i, acc = carry
            m_new = jnp.maximum(m_i, jnp.max(s, axis=-1, keepdims=True))
            p = jnp.exp(s - m_new)
            alpha = jnp.exp(m_i - m_new)
            l_new = alpha * l_i + jnp.sum(p, axis=-1, keepdims=True)
            acc = alpha * acc + _dot_nt(p.astype(BF), vt_ref[0, 0:LANES, pl.ds(k0, SEL_KEY_TILE)])
            return m_new, l_new, acc

        def sel_step(j, carry):
            k0 = pl.multiple_of(j * SEL_KEY_TILE, SEL_KEY_TILE)
            return update(scores(k0), carry, k0)

        init = (jnp.full((rows, 1), NEG, F32), jnp.zeros((rows, 1), F32), jnp.zeros((rows, LANES), F32))
        carry = lax.fori_loop(0, n_tiles - 1, sel_step, init)
        k0 = pl.multiple_of((n_tiles - 1) * SEL_KEY_TILE, SEL_KEY_TILE)
        s_last = jnp.where(k0 + kcol <= qpos, scores(k0), MASK_BIAS)
        _, l_s, acc_s = update(s_last, carry, k0)
        o_s = acc_s / jnp.maximum(l_s, 1e-30)

        kwt = kat_ref[0, (2 + kh) * LANES:(3 + kh) * LANES, pl.ds(w0, WIN_KEYS)]
        s_w = _dot(qs, kwt) + wbias
        e_w = jnp.exp(s_w - jnp.max(s_w, axis=-1, keepdims=True))
        o_w = (_dot_nt(e_w.astype(BF), vt_ref[0, LANES:2 * LANES, pl.ds(w0, WIN_KEYS)])
               / jnp.maximum(jnp.sum(e_w, axis=-1, keepdims=True), 1e-30))

        for g, h in enumerate(heads):
            rs = slice(g * Q_BLOCK, (g + 1) * Q_BLOCK)
            o = (sig[:, 3 * h:3 * h + 1] * o_c[rs] + sig[:, 3 * h + 1:3 * h + 2] * o_s[rs]
                 + sig[:, 3 * h + 2:3 * h + 3] * o_w[rs])
            o_ref[:, h * LANES:(h + 1) * LANES] = o.astype(BF)


def _nsa_prompt(q, gates, ckp, cv, kat, vt, batch, t):
    nb = t // BLOCK
    nc = t // Q_BLOCK
    m = batch * t
    eye = (jnp.arange(nb, dtype=jnp.int32)[:, None] == (jnp.arange(t, dtype=jnp.int32)[None, :] >> 6)).astype(BF)
    return pl.pallas_call(
        functools.partial(_nsa_prompt_kernel, nb=nb),
        out_shape=jax.ShapeDtypeStruct((m, Q_PAD_W), BF),
        grid=(batch, nc),
        in_specs=[pl.BlockSpec((Q_BLOCK, Q_PAD_W), lambda b, c: (b * nc + c, 0)),
                  pl.BlockSpec((Q_BLOCK, LANES), lambda b, c: (b * nc + c, 0)),
                  pl.BlockSpec((1, NSA_KV_HEADS, nb, LANES), lambda b, c: (b, 0, 0, 0)),
                  pl.BlockSpec((1, nb, LANES), lambda b, c: (b, 0, 0)),
                  pl.BlockSpec((1, KA_W, t), lambda b, c: (b, 0, 0)),
                  pl.BlockSpec((1, 2 * LANES, t), lambda b, c: (b, 0, 0)),
                  _full((nb, t))],
        out_specs=pl.BlockSpec((Q_BLOCK, Q_PAD_W), lambda b, c: (b * nc + c, 0)),
        compiler_params=_params(("parallel", "arbitrary")),
        name="nsa_prompt",
    )(q, gates, ckp, cv, kat, vt, eye)


SAMPLE_ROWS = 16


def _nsa_sample_kernel(q_ref, gate_ref, ck_ref, cv_ref, pg_ref, new_ref, win_ref, o_ref, *, past):
    rows = SAMPLE_ROWS
    nbp = cv_ref.shape[1]
    n_pages, page = pg_ref.shape[1], pg_ref.shape[3]
    n_buf = win_ref.shape[3]
    pos = past
    q = q_ref[0].astype(BF)
    qd = q[:, 0:HEAD_DIM]
    qf = qd.astype(F32)
    row = lax.broadcasted_iota(jnp.int32, (rows, 1), 0)
    grp0 = row < NSA_GROUP
    slope = _head_slopes(rows, 1, list(range(NSA_HEADS)))
    sig = jax.nn.sigmoid(gate_ref[0])
    blk = lax.broadcasted_iota(jnp.int32, (1, nbp), 1)
    blk_f = blk.astype(F32)
    blk_end = blk * BLOCK + (BLOCK - 1)

    def by_head(f):
        return jnp.where(grp0, f(0), f(1))

    s_c = by_head(lambda kh: _dot_nt(q, ck_ref[0, kh]))
    p_c = _masked_softmax(s_c, blk_end <= pos)
    o_c2 = _dot(p_c.astype(BF), cv_ref[0])
    o_c = jnp.where(grp0, o_c2[:, 0:HEAD_DIM], o_c2[:, HEAD_DIM:2 * HEAD_DIM])
    imp0 = jnp.sum(jnp.where(grp0, p_c, 0.0), axis=0, keepdims=True)
    imp1 = jnp.sum(jnp.where((row >= NSA_GROUP) & (row < NSA_HEADS), p_c, 0.0), axis=0, keepdims=True)
    imp = jnp.where(grp0, imp0, imp1)
    cur = pos // BLOCK
    forced = (blk == 0) | ((blk <= cur) & (blk > cur - N_LOCAL_BLOCKS))
    score = jnp.where(forced, FORCE_SCORE, imp)
    score = jnp.where(blk <= cur, score, -jnp.inf)
    sel = _select_blocks(score, blk_f, N_SEL)

    def new_row(c, kh):
        o = c * LANES + kh * HEAD_DIM
        return new_ref[0, :, o:o + HEAD_DIM].astype(BF).astype(F32)

    def attend(s, valid, vt_of, s_self, valid_self, v_self):
        s = jnp.where(valid, s, NEG)
        s_self = jnp.where(valid_self, s_self, NEG)
        m = jnp.maximum(jnp.max(s, axis=-1, keepdims=True), s_self)
        e = jnp.where(valid, jnp.exp(s - m), 0.0)
        e_self = jnp.where(valid_self, jnp.exp(s_self - m), 0.0)
        den = jnp.maximum(jnp.sum(e, axis=-1, keepdims=True) + e_self, 1e-30)
        eb = e.astype(BF)
        num = by_head(lambda kh: _dot_nt(eb, vt_of(kh))) + e_self.astype(BF).astype(F32) * v_self
        return num / den

    def page_rows(c, kh):
        o = c * LANES + kh * HEAD_DIM
        return jnp.concatenate([pg_ref[0, p, o:o + HEAD_DIM, :] for p in range(n_pages)], axis=1).astype(BF)

    e_row = lax.broadcasted_iota(jnp.int32, (nbp, past), 0)
    e_col = lax.broadcasted_iota(jnp.int32, (nbp, past), 1) >> 6
    selx = _dot(sel.astype(BF), jnp.where(e_row == e_col, 1.0, 0.0).astype(BF))
    kpos = lax.broadcasted_iota(jnp.int32, (1, past), 1)
    s_s = by_head(lambda kh: _dot(qd, page_rows(2, kh))) + slope * (kpos.astype(F32) - float(pos))
    s_self = by_head(lambda kh: jnp.sum(qf * new_row(2, kh), axis=-1, keepdims=True))
    sel_self = jnp.sum(jnp.where(blk == cur, sel, 0.0), axis=-1, keepdims=True) > 0.5
    o_s = attend(s_s, selx > 0.5, lambda kh: page_rows(3, kh), s_self, sel_self,
                 by_head(lambda kh: jnp.broadcast_to(new_row(3, kh), (rows, HEAD_DIM))))

    dist = n_buf - lax.broadcasted_iota(jnp.int32, (1, n_buf), 1)
    win = lambda c, kh: win_ref[0, 0, c * LANES + kh * HEAD_DIM:c * LANES + (kh + 1) * HEAD_DIM, :].astype(BF)
    s_w = by_head(lambda kh: _dot(qd, win(0, kh))) - slope * dist.astype(F32)
    s_wn = by_head(lambda kh: jnp.sum(qf * new_row(4, kh), axis=-1, keepdims=True))
    o_w = attend(s_w, (dist >= 0) & (dist < WINDOW), lambda kh: win(1, kh), s_wn, row >= 0,
                 by_head(lambda kh: jnp.broadcast_to(new_row(5, kh), (rows, HEAD_DIM))))

    o_ref[0] = sig[:, 0:1] * o_c + sig[:, 1:2] * o_s + sig[:, 2:3] * o_w


def _nsa_sample(q, gates3, ckp, cv, pages, kv_new, win_all, layer):
    b, n_pages, _, page = pages.shape
    past = n_pages * page
    nbp = cv.shape[1]
    n_buf = win_all.shape[3]
    per3 = lambda shape: pl.BlockSpec((1,) + shape, lambda i: (i, 0, 0))
    per4 = lambda shape: pl.BlockSpec((1,) + shape, lambda i: (i, 0, 0, 0))
    return pl.pallas_call(
        functools.partial(_nsa_sample_kernel, past=past),
        out_shape=jax.ShapeDtypeStruct((b, SAMPLE_ROWS, HEAD_DIM), F32),
        grid=(b,),
        in_specs=[per3((SAMPLE_ROWS, LANES)), per3((SAMPLE_ROWS, 3)), per4((NSA_KV_HEADS, nbp, LANES)),
                  per3((nbp, LANES)), per4((n_pages, 4 * LANES, page)), per3((1, KV_W)),
                  pl.BlockSpec((1, 1, 2 * LANES, n_buf), lambda i: (layer, i, 0, 0))],
        out_specs=per3((SAMPLE_ROWS, HEAD_DIM)),
        compiler_params=_params(("parallel",)),
        name="nsa_sample",
    )(q, gates3, ckp, cv, pages, kv_new, win_all)


def _rglru_gates(xc, wa_ref, ba_ref, wx_ref, bx_ref, lam_ref):
    xb = xc.astype(BF)
    r = jax.nn.sigmoid(_dot(xb, wa_ref[...]) + ba_ref[...])
    i = jax.nn.sigmoid(_dot(xb, wx_ref[...]) + bx_ref[...])
    z = -lam_ref[...]
    softplus = jnp.maximum(z, 0.0) + jnp.log1p(jnp.exp(-jnp.abs(z)))
    log_a = -RG_C * r * softplus
    th = jnp.tanh(log_a)
    mult = jnp.sqrt(-2.0 * th / (1.0 - th))
    return jnp.exp(log_a), i, mult


def _shift_rows(x, s, fill):
    n = x.shape[0]
    if s % 8 == 0:
        return jnp.concatenate([jnp.full((s, x.shape[1]), fill, x.dtype), x[:n - s]], axis=0)
    r = lax.broadcasted_iota(jnp.int32, (n, 1), 0)
    return jnp.where(r < s, fill, pltpu.roll(x, s, 0))


def _rglru_prompt_kernel(xr_ref, gr_ref, cw_ref, cb_ref, wa_ref, ba_ref, wx_ref, bx_ref, lam_ref,
                         y_ref, hl_ref, buf_ref, h_ref):
    t = pl.program_id(1)
    tt = xr_ref.shape[0]

    @pl.when(t == 0)
    def _():
        buf_ref[0:HALO, :] = jnp.zeros((HALO, D_RNN), F32)
        h_ref[...] = jnp.zeros_like(h_ref)

    buf_ref[HALO:HALO + tt, :] = xr_ref[...]
    xc = cb_ref[...] + cw_ref[RNN_CONV - 1:RNN_CONV, :] * xr_ref[...]
    for j in range(1, RNN_CONV):
        xc = xc + cw_ref[RNN_CONV - 1 - j:RNN_CONV - j, :] * buf_ref[HALO - j:HALO - j + tt, :]
    a, i, mult = _rglru_gates(xc, wa_ref, ba_ref, wx_ref, bx_ref, lam_ref)
    row = lax.broadcasted_iota(jnp.int32, (tt, 1), 0)
    mult = jnp.where(row + t * tt == 0, 1.0, mult)
    b = xc * i * mult
    s = 1
    while s < tt:
        b = a * _shift_rows(b, s, 0.0) + b
        a = a * _shift_rows(a, s, 1.0)
        s *= 2
    h = a * h_ref[0:1, :] + b
    h_ref[0:1, :] = h[tt - 1:tt, :]
    hl_ref[0] = h[tt - 1:tt, :]
    y_ref[...] = (h * _gelu(gr_ref[...])).astype(BF)
    buf_ref[0:HALO, :] = buf_ref[tt:tt + HALO, :]


def _block_diag(w):
    n, c, d = w.shape
    eye = jnp.eye(n, dtype=w.dtype)
    return (w[:, :, None, :] * eye[:, None, :, None]).reshape(n * c, n * d)


def _rglru_prompt(xr, gr, pa, batch, t):
    tt = min(SCAN_TILE, t)
    nt = t // tt
    vec = lambda a: a.reshape(1, D_RNN)
    y, hl = pl.pallas_call(
        _rglru_prompt_kernel,
        out_shape=[jax.ShapeDtypeStruct((batch * t, D_RNN), BF), jax.ShapeDtypeStruct((batch, 1, D_RNN), F32)],
        grid=(batch, nt),
        in_specs=[pl.BlockSpec((tt, D_RNN), lambda b, i: (b * nt + i, 0)),
                  pl.BlockSpec((tt, D_RNN), lambda b, i: (b * nt + i, 0)),
                  _full((RNN_CONV, D_RNN)), _full((1, D_RNN)), _full((D_RNN, D_RNN)), _full((1, D_RNN)),
                  _full((D_RNN, D_RNN)), _full((1, D_RNN)), _full((1, D_RNN))],
        out_specs=[pl.BlockSpec((tt, D_RNN), lambda b, i: (b * nt + i, 0)),
                   pl.BlockSpec((1, 1, D_RNN), lambda b, i: (b, 0, 0))],
        scratch_shapes=[pltpu.VMEM((HALO + tt, D_RNN), F32), pltpu.VMEM((8, D_RNN), F32)],
        compiler_params=_params(("arbitrary", "arbitrary")),
        name="rglru_prompt",
    )(xr, gr, pa['conv_w'], vec(pa['conv_b']), _block_diag(pa['wa']).astype(BF), vec(pa['ba']),
      _block_diag(pa['wx']).astype(BF), vec(pa['bx']), vec(pa['lam']))
    return y, hl.reshape(batch, D_RNN)


def _rglru_sample_kernel(xr_ref, gr_ref, prev_ref, h0_ref, cw_ref, cb_ref, wa_ref, ba_ref, wx_ref, bx_ref, lam_ref,
                         y_ref, h_ref):
    xc = cb_ref[...] + cw_ref[RNN_CONV - 1:RNN_CONV, :] * xr_ref[...]
    for k in range(RNN_CONV - 1):
        xc = xc + cw_ref[k:k + 1, :] * prev_ref[k]
    a, i, mult = _rglru_gates(xc, wa_ref, ba_ref, wx_ref, bx_ref, lam_ref)
    h = xc * i * mult + a * h0_ref[...]
    h_ref[...] = h
    y_ref[...] = (h * _gelu(gr_ref[...])).astype(BF)


def _rglru_sample(xr, gr, h0, prev_t, pa):
    b = xr.shape[0]
    vec = lambda a: a.reshape(1, D_RNN)
    return pl.pallas_call(
        _rglru_sample_kernel,
        out_shape=[jax.ShapeDtypeStruct((b, D_RNN), BF), jax.ShapeDtypeStruct((b, D_RNN), F32)],
        name="rglru_sample",
    )(xr, gr, prev_t, h0, pa['conv_w'], vec(pa['conv_b']),
      _block_diag(pa['wa']).astype(BF), vec(pa['ba']), _block_diag(pa['wx']).astype(BF), vec(pa['bx']),
      vec(pa['lam']))


def _mm2_ln_kernel(a_ref, b_ref, wa_ref, wb_ref, x_ref, g_ref, beta_ref, o_ref):
    y = _dot(a_ref[...], wa_ref[...]) + _dot(b_ref[...], wb_ref[...])
    o_ref[...] = _ln(DN_ALPHA * x_ref[...] + y, g_ref[...], beta_ref[...])


def _mm2_ln(a, b, wa, wb, x, g, beta):
    m = x.shape[0]
    tm = _row_tile(m)
    ka, kb = a.shape[1], b.shape[1]
    return pl.pallas_call(
        _mm2_ln_kernel,
        out_shape=jax.ShapeDtypeStruct((m, D_MODEL), F32),
        grid=(m // tm,),
        in_specs=[pl.BlockSpec((tm, ka), lambda i: (i, 0)), pl.BlockSpec((tm, kb), lambda i: (i, 0)),
                  _full((ka, D_MODEL)), _full((kb, D_MODEL)), pl.BlockSpec((tm, D_MODEL), lambda i: (i, 0)),
                  _full((1, D_MODEL)), _full((1, D_MODEL))],
        out_specs=pl.BlockSpec((tm, D_MODEL), lambda i: (i, 0)),
        compiler_params=_params(("parallel",)),
        name="mm2_ln",
    )(a, b, wa, wb, x, g.reshape(1, -1), beta.reshape(1, -1))


def _mm_kernel(x_ref, w_ref, o_ref, ob_ref):
    y = _dot(x_ref[...].astype(BF), w_ref[0])
    o_ref[0] = y
    ob_ref[0] = y.astype(BF)


def _memory_kv(mem, wkv):
    r = mem.shape[0]
    nl, _, n = wkv.shape
    return pl.pallas_call(
        _mm_kernel,
        out_shape=[jax.ShapeDtypeStruct((nl, r, n), F32), jax.ShapeDtypeStruct((nl, r, n), BF)],
        grid=(nl,),
        in_specs=[_full((r, D_MODEL)), pl.BlockSpec((1, D_MODEL, n), lambda l: (l, 0, 0))],
        out_specs=[pl.BlockSpec((1, r, n), lambda l: (l, 0, 0)), pl.BlockSpec((1, r, n), lambda l: (l, 0, 0))],
        compiler_params=_params(("parallel",)),
        name="memory_kv",
    )(mem, wkv.astype(BF))


def _xattn_prompt_kernel(x_ref, k_ref, v_ref, wq_ref, wo_ref, g_ref, beta_ref, o_ref):
    x = x_ref[...]
    q = _dot(x.astype(BF), wq_ref[...]).astype(BF)
    outs = []
    for h in range(X_HEADS):
        hs = slice(h * X_HEAD_DIM, (h + 1) * X_HEAD_DIM)
        s = _dot_nt(q[:, hs], k_ref[0, :, hs]) * (X_HEAD_DIM ** -0.5)
        m = jnp.max(s, axis=-1, keepdims=True)
        e = jnp.exp(s - m)
        p = e / jnp.sum(e, axis=-1, keepdims=True)
        outs.append(_dot(p.astype(BF), v_ref[0, :, hs]).astype(BF))
    y = _dot(jnp.concatenate(outs, axis=1), wo_ref[...])
    o_ref[...] = _ln(DN_ALPHA * x + y, g_ref[...], beta_ref[...])


def _xattn_prompt(x, kb, vb, wq, wo, g, beta, batch, t):
    tm = _row_tile(t)
    nt = t // tm
    return pl.pallas_call(
        _xattn_prompt_kernel,
        out_shape=jax.ShapeDtypeStruct((batch * t, D_MODEL), F32),
        grid=(batch, nt),
        in_specs=[pl.BlockSpec((tm, D_MODEL), lambda b, i: (b * nt + i, 0)),
                  pl.BlockSpec((1, MEM_LEN, X_W), lambda b, i: (b, 0, 0)),
                  pl.BlockSpec((1, MEM_LEN, X_W), lambda b, i: (b, 0, 0)),
                  _full((D_MODEL, X_W)), _full((X_W, D_MODEL)), _full((1, D_MODEL)), _full((1, D_MODEL))],
        out_specs=pl.BlockSpec((tm, D_MODEL), lambda b, i: (b * nt + i, 0)),
        compiler_params=_params(("parallel", "parallel")),
        name="xattn_prompt",
    )(x, kb, vb, wq, wo, g.reshape(1, -1), beta.reshape(1, -1))


XS_SEQS = 8


def _xattn_sample_kernel(x_ref, kv_ref, wq_ref, wo_ref, g_ref, beta_ref, o_ref):
    x = x_ref[...]
    xb = x.astype(BF)
    qh = [_dot(xb, wq_ref[:, h * X_HEAD_DIM:(h + 1) * X_HEAD_DIM]).astype(BF).astype(F32) for h in range(X_HEADS)]
    rows = []
    for s_i in range(XS_SEQS):
        q4 = jnp.concatenate([qh[h][s_i:s_i + 1, :] for h in range(X_HEADS)], axis=0)
        k = kv_ref[0, s_i, :, 0].astype(BF).astype(F32)
        v = kv_ref[0, s_i, :, 1].astype(BF).astype(F32)
        s = jnp.sum(k * q4[None], axis=-1, keepdims=True) * (X_HEAD_DIM ** -0.5)
        e = jnp.exp(s - jnp.max(s, axis=0, keepdims=True))
        p = (e / jnp.sum(e, axis=0, keepdims=True)).astype(BF).astype(F32)
        o4 = jnp.sum(p * v, axis=0)
        rows.append(jnp.concatenate([o4[h:h + 1, :] for h in range(X_HEADS)], axis=1))
    o = jnp.concatenate(rows, axis=0)
    y = _dot(o.astype(BF), wo_ref[...])
    o_ref[...] = _ln(DN_ALPHA * x + y, g_ref[...], beta_ref[...])


def _xattn_sample(x, cache, layer, wq, wo, g, beta):
    b = x.shape[0]
    return pl.pallas_call(
        _xattn_sample_kernel,
        out_shape=jax.ShapeDtypeStruct((b, D_MODEL), F32),
        grid=(b // XS_SEQS,),
        in_specs=[pl.BlockSpec((XS_SEQS, D_MODEL), lambda i: (i, 0)),
                  pl.BlockSpec((1, XS_SEQS, MEM_LEN, 2, X_HEADS, X_HEAD_DIM), lambda i: (layer, i, 0, 0, 0, 0)),
                  _full((D_MODEL, X_W)), _full((X_W, D_MODEL)), _full((1, D_MODEL)), _full((1, D_MODEL))],
        out_specs=pl.BlockSpec((XS_SEQS, D_MODEL), lambda i: (i, 0)),
        compiler_params=_params(("parallel",)),
        name="xattn_sample",
    )(x, cache, wq, wo, g.reshape(1, -1), beta.reshape(1, -1))


def _swiglu_rows(xb, wg_ref, wu_ref, wd_ref, lead):
    acc = None
    for c in range(D_FF // FF_CHUNK):
        cs = slice(c * FF_CHUNK, (c + 1) * FF_CHUNK)
        hg = _dot(xb, wg_ref[lead + (slice(None), cs)])
        hu = _dot(xb, wu_ref[lead + (slice(None), cs)])
        part = _dot((_silu(hg) * hu).astype(BF), wd_ref[lead + (cs, slice(None))])
        acc = part if acc is None else acc + part
    return acc


def _swiglu_ln_kernel(x_ref, wg_ref, wu_ref, wd_ref, g_ref, beta_ref, o_ref):
    x = x_ref[...]
    y = _swiglu_rows(x.astype(BF), wg_ref, wu_ref, wd_ref, ())
    o_ref[...] = _ln(DN_ALPHA * x + y, g_ref[...], beta_ref[...])


def _swiglu_ln(x, wg, wu, wd, g, beta):
    m = x.shape[0]
    tm = _row_tile(m)
    return pl.pallas_call(
        _swiglu_ln_kernel,
        out_shape=jax.ShapeDtypeStruct((m, D_MODEL), F32),
        grid=(m // tm,),
        in_specs=[pl.BlockSpec((tm, D_MODEL), lambda i: (i, 0)), _full((D_MODEL, D_FF)), _full((D_MODEL, D_FF)),
                  _full((D_FF, D_MODEL)), _full((1, D_MODEL)), _full((1, D_MODEL))],
        out_specs=pl.BlockSpec((tm, D_MODEL), lambda i: (i, 0)),
        compiler_params=_params(("parallel",)),
        name="swiglu_ln",
    )(x, wg, wu, wd, g.reshape(1, -1), beta.reshape(1, -1))


def _router_kernel(x_ref, r_ref, idx_ref, w_ref):
    logits = jnp.dot(x_ref[...], r_ref[...], preferred_element_type=F32, precision=lax.Precision.HIGHEST)
    lane = lax.broadcasted_iota(jnp.int32, logits.shape, 1)
    lane_f = lane.astype(F32)
    logits = jnp.where(lane < N_EXPERTS, logits, -jnp.inf)
    m1 = jnp.max(logits, axis=-1, keepdims=True)
    i1 = jnp.min(jnp.where(logits == m1, lane_f, 1.0e9), axis=-1, keepdims=True)
    rest = jnp.where(lane_f == i1, -jnp.inf, logits)
    m2 = jnp.max(rest, axis=-1, keepdims=True)
    i2 = jnp.min(jnp.where(rest == m2, lane_f, 1.0e9), axis=-1, keepdims=True)
    e2 = jnp.exp(m2 - m1)
    den = 1.0 + e2
    idx_ref[...] = jnp.where(lane == 0, i1, i2).astype(jnp.int32)
    w_ref[...] = jnp.where(lane == 0, 1.0 / den, e2 / den)


def _router(x, router):
    m = x.shape[0]
    tm = _row_tile(m)
    rp = jnp.pad(router, ((0, 0), (0, LANES - N_EXPERTS)))
    idx, w = pl.pallas_call(
        _router_kernel,
        out_shape=[jax.ShapeDtypeStruct((m, LANES), jnp.int32), jax.ShapeDtypeStruct((m, LANES), F32)],
        grid=(m // tm,),
        in_specs=[pl.BlockSpec((tm, D_MODEL), lambda i: (i, 0)), _full((D_MODEL, LANES))],
        out_specs=[pl.BlockSpec((tm, LANES), lambda i: (i, 0)), pl.BlockSpec((tm, LANES), lambda i: (i, 0))],
        compiler_params=_params(("parallel",)),
        name="router",
    )(x, rp)
    return idx[:, :TOP_K], w[:, :TOP_K]


def _moe_kernel(te_ref, nu_ref, x_ref, s_ref, wg_ref, wu_ref, wd_ref, o_ref):
    i = pl.program_id(0)

    @pl.when(i < nu_ref[0])
    def _():
        o_ref[...] = s_ref[...] * _swiglu_rows(x_ref[...], wg_ref, wu_ref, wd_ref, (0,))

    @pl.when(i >= nu_ref[0])
    def _():
        o_ref[...] = jnp.zeros_like(o_ref)


def _moe_ffn(tile_expert, n_used, xs, scale, wg, wu, wd):
    p = xs.shape[0]
    return pl.pallas_call(
        _moe_kernel,
        out_shape=jax.ShapeDtypeStruct((p, D_MODEL), F32),
        grid_spec=pltpu.PrefetchScalarGridSpec(
            num_scalar_prefetch=2,
            grid=(p // MOE_TILE,),
            in_specs=[pl.BlockSpec((MOE_TILE, D_MODEL), lambda i, te, nu: (i, 0)),
                      pl.BlockSpec((MOE_TILE, 1), lambda i, te, nu: (i, 0)),
                      pl.BlockSpec((1, D_MODEL, D_FF), lambda i, te, nu: (te[i], 0, 0)),
                      pl.BlockSpec((1, D_MODEL, D_FF), lambda i, te, nu: (te[i], 0, 0)),
                      pl.BlockSpec((1, D_FF, D_MODEL), lambda i, te, nu: (te[i], 0, 0))],
            out_specs=pl.BlockSpec((MOE_TILE, D_MODEL), lambda i, te, nu: (i, 0))),
        compiler_params=_params(("arbitrary",)),
        name="moe_ffn",
    )(tile_expert, n_used, xs, scale, wg, wu, wd)


def _add2_ln_kernel(x_ref, y0_ref, y1_ref, g_ref, beta_ref, o_ref):
    o_ref[...] = _ln(DN_ALPHA * x_ref[...] + (y0_ref[...] + y1_ref[...]), g_ref[...], beta_ref[...])


def _add2_ln(x, y2, row0, g, beta):
    m = x.shape[0]
    tm = _row_tile(m)
    nt = m // tm
    b0 = row0 // tm
    return pl.pallas_call(
        _add2_ln_kernel,
        out_shape=jax.ShapeDtypeStruct((m, D_MODEL), F32),
        grid=(nt,),
        in_specs=[pl.BlockSpec((tm, D_MODEL), lambda i: (i, 0)), pl.BlockSpec((tm, D_MODEL), lambda i: (i + b0, 0)),
                  pl.BlockSpec((tm, D_MODEL), lambda i: (i + b0 + nt, 0)), _full((1, D_MODEL)),
                  _full((1, D_MODEL))],
        out_specs=pl.BlockSpec((tm, D_MODEL), lambda i: (i, 0)),
        compiler_params=_params(("parallel",)),
        name="add2_ln",
    )(x, y2, y2, g.reshape(1, -1), beta.reshape(1, -1))


def _moe_layer(xp, xs, router, wg, wu, wd, g, beta):
    mp, ms = xp.shape[0], xs.shape[0]
    m = mp + ms
    ip, wp = _router(xp, router)
    is_, ws = _router(xs, router)
    e_flat = jnp.concatenate([ip.T.reshape(-1), is_.T.reshape(-1)])
    w_flat = jnp.concatenate([wp.T.reshape(-1), ws.T.reshape(-1)])
    tok = jnp.concatenate([jnp.tile(jnp.arange(mp, dtype=jnp.int32), TOP_K),
                           mp + jnp.tile(jnp.arange(ms, dtype=jnp.int32), TOP_K)])
    onehot = (e_flat[:, None] == jnp.arange(N_EXPERTS, dtype=jnp.int32)[None, :]).astype(jnp.int32)
    cum = jnp.cumsum(onehot, axis=0)
    rank = jnp.sum(onehot * (cum - 1), axis=1)
    counts = cum[-1]
    padded = ((counts + MOE_TILE - 1) // MOE_TILE) * MOE_TILE
    ends = jnp.cumsum(padded)
    starts = ends - padded
    pos = starts[e_flat] + rank
    n_tiles = (TOP_K * m + N_EXPERTS * (MOE_TILE - 1)) // MOE_TILE + 1
    p_rows = n_tiles * MOE_TILE
    tile_start = jnp.arange(n_tiles, dtype=jnp.int32) * MOE_TILE
    tile_expert = jnp.minimum(jnp.sum((tile_start[:, None] >= ends[None, :]).astype(jnp.int32), axis=1),
                              N_EXPERTS - 1).astype(jnp.int32)
    n_used = (ends[-1] // MOE_TILE).astype(jnp.int32).reshape(1)
    src = jnp.zeros((p_rows,), jnp.int32).at[pos].set(tok)
    scale = jnp.zeros((p_rows,), F32).at[pos].set(w_flat)
    x_sorted = jnp.concatenate([xp.astype(BF), xs.astype(BF)], axis=0)[src]
    y2 = _moe_ffn(tile_expert, n_used, x_sorted, scale.reshape(p_rows, 1), wg, wu, wd)[pos]
    return _add2_ln(xp, y2, 0, g, beta), _add2_ln(xs, y2, TOP_K * mp, g, beta)


def _glu_kernel(x_ref, w_ref, b_ref, o_ref):
    xb = x_ref[...].astype(BF)
    a = _dot(xb, w_ref[:, 0:D_MODEL]) + b_ref[:, 0:D_MODEL]
    gate = _dot(xb, w_ref[:, D_MODEL:2 * D_MODEL]) + b_ref[:, D_MODEL:2 * D_MODEL]
    o_ref[...] = a * jax.nn.sigmoid(gate)


def _glu(x, w, b):
    m = x.shape[0]
    tm = _row_tile(m)
    return pl.pallas_call(
        _glu_kernel,
        out_shape=jax.ShapeDtypeStruct((m, D_MODEL), F32),
        grid=(m // tm,),
        in_specs=[pl.BlockSpec((tm, D_MODEL), lambda i: (i, 0)), _full((D_MODEL, 2 * D_MODEL)),
                  _full((1, 2 * D_MODEL))],
        out_specs=pl.BlockSpec((tm, D_MODEL), lambda i: (i, 0)),
        compiler_params=_params(("parallel",)),
        name="glu",
    )(x, w, b.reshape(1, -1))


CONV_ROWS = 64


def _conv_tail(c, x, cg_ref, cb_ref, wp_ref, bp_ref, g_ref, beta_ref):
    c = _ln(c, cg_ref[...], cb_ref[...])
    y = _dot(_silu(c).astype(BF), wp_ref[...]) + bp_ref[...]
    return _ln(DN_ALPHA * x + y, g_ref[...], beta_ref[...])


def _conv_prompt_kernel(gl_ref, x_ref, dw_ref, db_ref, cg_ref, cb_ref, wp_ref, bp_ref, g_ref, beta_ref,
                        o_ref, buf_ref):
    t = pl.program_id(1)
    tt = gl_ref.shape[0]

    @pl.when(t == 0)
    def _():
        buf_ref[0:HALO, :] = jnp.zeros((HALO, D_MODEL), F32)

    buf_ref[HALO:HALO + tt, :] = gl_ref[...]
    off = HALO - (CONF_K - 1)
    for r0 in range(0, tt, CONV_ROWS):
        c = db_ref[...] + dw_ref[0:1, :] * buf_ref[r0 + off:r0 + off + CONV_ROWS, :]
        for k in range(1, CONF_K):
            c = c + dw_ref[k:k + 1, :] * buf_ref[r0 + off + k:r0 + off + k + CONV_ROWS, :]
        o_ref[r0:r0 + CONV_ROWS, :] = _conv_tail(c, x_ref[r0:r0 + CONV_ROWS, :], cg_ref, cb_ref, wp_ref, bp_ref,
                                                 g_ref, beta_ref)
    buf_ref[0:HALO, :] = buf_ref[tt:tt + HALO, :]


def _conv_prompt(gl, x, cp, g, beta, batch, t):
    tt = min(SCAN_TILE, t)
    nt = t // tt
    vec = lambda a: a.reshape(1, -1)
    return pl.pallas_call(
        _conv_prompt_kernel,
        out_shape=jax.ShapeDtypeStruct((batch * t, D_MODEL), F32),
        grid=(batch, nt),
        in_specs=[pl.BlockSpec((tt, D_MODEL), lambda b, i: (b * nt + i, 0)),
                  pl.BlockSpec((tt, D_MODEL), lambda b, i: (b * nt + i, 0)),
                  _full((CONF_K, D_MODEL)), _full((1, D_MODEL)), _full((1, D_MODEL)), _full((1, D_MODEL)),
                  _full((D_MODEL, D_MODEL)), _full((1, D_MODEL)), _full((1, D_MODEL)), _full((1, D_MODEL))],
        out_specs=pl.BlockSpec((tt, D_MODEL), lambda b, i: (b * nt + i, 0)),
        scratch_shapes=[pltpu.VMEM((HALO + tt, D_MODEL), F32)],
        compiler_params=_params(("arbitrary", "arbitrary")),
        name="conv_prompt",
    )(gl, x, cp['dw_w'], vec(cp['dw_b']), vec(cp['ln_g']), vec(cp['ln_b']), cp['w_pw'], vec(cp['b_pw']),
      vec(g), vec(beta))


CS_SEQS = 32


def _conv_sample_kernel(gl_ref, st_ref, x_ref, dw_ref, db_ref, cg_ref, cb_ref, wp_ref, bp_ref, g_ref, beta_ref,
                        o_ref):
    c = db_ref[...] + dw_ref[CONF_K - 1:CONF_K, :] * gl_ref[...]
    for k in range(CONF_K - 1):
        c = c + dw_ref[k:k + 1, :] * st_ref[0, k]
    o_ref[...] = _conv_tail(c, x_ref[...], cg_ref, cb_ref, wp_ref, bp_ref, g_ref, beta_ref)


def _conv_sample(gl, state_t, layer, x, cp, g, beta):
    b = x.shape[0]
    vec = lambda a: a.reshape(1, -1)
    return pl.pallas_call(
        _conv_sample_kernel,
        out_shape=jax.ShapeDtypeStruct((b, D_MODEL), F32),
        grid=(b // CS_SEQS,),
        in_specs=[pl.BlockSpec((CS_SEQS, D_MODEL), lambda i: (i, 0)),
                  pl.BlockSpec((1, CONF_K - 1, CS_SEQS, D_MODEL), lambda i: (layer, 0, i, 0)),
                  pl.BlockSpec((CS_SEQS, D_MODEL), lambda i: (i, 0)),
                  _full((CONF_K, D_MODEL)), _full((1, D_MODEL)), _full((1, D_MODEL)), _full((1, D_MODEL)),
                  _full((D_MODEL, D_MODEL)), _full((1, D_MODEL)), _full((1, D_MODEL)), _full((1, D_MODEL))],
        out_specs=pl.BlockSpec((CS_SEQS, D_MODEL), lambda i: (i, 0)),
        compiler_params=_params(("parallel",)),
        name="conv_sample",
    )(gl, state_t, x, cp['dw_w'], vec(cp['dw_b']), vec(cp['ln_g']), vec(cp['ln_b']), cp['w_pw'], vec(cp['b_pw']),
      vec(g), vec(beta))


def _flatten_blocks(kt2, lead):
    n = kt2.shape[-1] // BLOCK
    nl = len(lead)
    x = kt2.reshape(*lead, 2, NSA_KV_HEADS, HEAD_DIM, n, BLOCK)
    perm = (nl,) + tuple(range(nl)) + (nl + 3, nl + 1, nl + 2, nl + 4)
    return x.transpose(perm).reshape(2, -1, HEAD_DIM * BLOCK)


def _mixer_a_prompt(x, w_in, wo_nsa, wo_rnn, pa, g, beta, batch, t):
    nb = t // BLOCK
    q, xr, gr, gates, kvt, kat, vt = _proj_a(x, w_in, batch, t)
    cmp = _compress(_flatten_blocks(kvt[:, 0:256], (batch,)), pa['cmp_pos'], pa['cmp_w1'], pa['cmp_w2'])
    ckp = _cmp_keys(cmp[0].reshape(batch, nb, NSA_KV_HEADS, HEAD_DIM), nb)
    cv = cmp[1].reshape(batch, nb, LANES).astype(BF)
    o_nsa = _nsa_prompt(q, gates, ckp, cv, kat, vt, batch, t)
    o_rnn, h_last = _rglru_prompt(xr, gr, pa, batch, t)
    x_new = _mm2_ln(o_nsa, o_rnn, wo_nsa, wo_rnn, x, g, beta)
    n_keep = min(WINDOW, t)
    kv_out = kvt[:, 0:512].reshape(batch, 4, NSA_KV_HEADS, HEAD_DIM, t).transpose(0, 4, 1, 2, 3)
    win_out = kvt[:, 512:768, t - n_keep:].reshape(batch, 2, NSA_KV_HEADS, HEAD_DIM, n_keep).transpose(0, 4, 1, 2, 3)
    conv_out = xr.reshape(batch, t, D_RNN)[:, t - (RNN_CONV - 1):]
    return x_new, kv_out, win_out, h_last, conv_out


def _mixer_a_sample(x, pool_all, page_ids, win_all, layer, h0, prev_t, w_in, wo_nsa, wo_rnn, pa, g, beta):
    b = x.shape[0]
    n_pages, page = page_ids.shape[1], pool_all.shape[2]
    nb_past = n_pages * page // BLOCK
    q, xr, gr, gates, kvt, _, _, kv_row = _proj_a(x, w_in, 1, b, row_kv=True)
    pages = pool_all[page_ids]
    cmp = _compress(_flatten_blocks(pages[:, :, 0:256], (b, n_pages)), pa['cmp_pos'], pa['cmp_w1'], pa['cmp_w2'])
    cmp = cmp.reshape(2, b, nb_past, NSA_KV_HEADS, HEAD_DIM)
    ckp = _cmp_keys(cmp[0], LANES)
    cv = jnp.pad(cmp[1].reshape(b, nb_past, LANES), ((0, 0), (0, LANES - nb_past), (0, 0))).astype(BF)
    q16 = jnp.pad(q.astype(F32).reshape(b, NSA_HEADS, LANES), ((0, 0), (0, SAMPLE_ROWS - NSA_HEADS), (0, 0)))
    g3 = jnp.pad(gates[:, :3 * NSA_HEADS].reshape(b, NSA_HEADS, 3), ((0, 0), (0, SAMPLE_ROWS - NSA_HEADS), (0, 0)))
    o16 = _nsa_sample(q16, g3, ckp, cv, pages, kv_row.reshape(b, 1, KV_W), win_all, layer)
    o_nsa = o16[:, :NSA_HEADS].reshape(b, NSA_HEADS * HEAD_DIM).astype(BF)
    o_rnn, h_new = _rglru_sample(xr, gr, h0, prev_t, pa)
    x_new = _mm2_ln(o_nsa, o_rnn, wo_nsa, wo_rnn, x, g, beta)
    kv_out = kvt[0, 0:512].reshape(4, NSA_KV_HEADS, HEAD_DIM, b).transpose(3, 0, 1, 2)[:, None]
    win_col = kvt[0, 512:768].T
    conv_out = jnp.concatenate([prev_t[1:], xr[None]], axis=0).transpose(1, 0, 2)
    return x_new, kv_out, win_col, h_new, conv_out


def kernel(x_prompt, x_sample, cache_nsa_kv, cache_nsa_win, state_rglru_h, state_rglru_conv, state_conv,
           cache_mem_kv, page_table, mem_prompt, ln_g, ln_b, a_w_in, a_cmp_pos, a_cmp_w1, a_cmp_w2,
           a_conv_w, a_conv_b, a_gate_a_w, a_gate_a_b, a_gate_x_w, a_gate_x_b, a_lambda, a_w_out,
           c_w_glu, c_b_glu, c_dw_w, c_dw_b, c_ln_g, c_ln_b, c_w_pw, c_b_pw, x_wq, x_wkv, x_wo,
           f_w_gu, f_w_down, m_router, m_w_gu, m_w_down):
    batch, t, _ = x_prompt.shape
    bs = x_sample.shape[0]
    xp = x_prompt.reshape(batch * t, D_MODEL)
    xs = x_sample.reshape(bs, D_MODEL)
    mkv_f, mkv_b = _memory_kv(mem_prompt.reshape(batch * MEM_LEN, D_MODEL), x_wkv.reshape(DEPTH, D_MODEL, 2 * X_W))
    n_layers_a, n_pool, page = cache_nsa_kv.shape[:3]
    pool_all = cache_nsa_kv.transpose(0, 1, 3, 4, 5, 2).reshape(n_layers_a * n_pool, 4 * LANES, page)
    n_buf = cache_nsa_win.shape[2]
    win_t = cache_nsa_win.transpose(0, 1, 3, 4, 5, 2).reshape(n_layers_a, bs, 2 * LANES, n_buf)
    rconv_t = state_rglru_conv.transpose(0, 2, 1, 3)
    sconv_t = state_conv.transpose(0, 2, 1, 3)
    pk, pw, ph, pcv, pc = [], [], [], [], []
    sk, sw, sh, scv, sc = [], [], [], [], []
    for l in range(DEPTH):
        if l % 2 == 0:
            i = l // 2
            pa = {'cmp_pos': a_cmp_pos[i], 'cmp_w1': a_cmp_w1[i], 'cmp_w2': a_cmp_w2[i],
                  'conv_w': a_conv_w[i], 'conv_b': a_conv_b[i], 'wa': a_gate_a_w[i], 'ba': a_gate_a_b[i],
                  'wx': a_gate_x_w[i], 'bx': a_gate_x_b[i], 'lam': a_lambda[i]}
            w_in = _prep_w_in(a_w_in[i])
            wo_pad, wo_cmp, wo_rnn = _prep_w_out(a_w_out[i])
            xp, kv_p, win_p, h_p, cb_p = _mixer_a_prompt(xp, w_in, wo_pad, wo_rnn, pa, ln_g[l, 0], ln_b[l, 0],
                                                         batch, t)
            xs, kv_s, win_s, h_s, cb_s = _mixer_a_sample(xs, pool_all, page_table + i * n_pool, win_t, i,
                                                         state_rglru_h[i], rconv_t[i], w_in, wo_cmp, wo_rnn, pa,
                                                         ln_g[l, 0], ln_b[l, 0])
            pk.append(kv_p); pw.append(win_p); ph.append(h_p); pcv.append(cb_p)
            sk.append(kv_s); sw.append(win_s); sh.append(h_s); scv.append(cb_s)
        else:
            j = l // 2
            cp = {'dw_w': c_dw_w[j], 'dw_b': c_dw_b[j], 'ln_g': c_ln_g[j], 'ln_b': c_ln_b[j],
                  'w_pw': c_w_pw[j].astype(BF), 'b_pw': c_b_pw[j]}
            w_glu = c_w_glu[j].astype(BF)
            gl_p = _glu(xp, w_glu, c_b_glu[j])
            gl_s = _glu(xs, w_glu, c_b_glu[j])
            xp = _conv_prompt(gl_p, xp, cp, ln_g[l, 0], ln_b[l, 0], batch, t)
            xs = _conv_sample(gl_s, sconv_t, j, xs, cp, ln_g[l, 0], ln_b[l, 0])
            pc.append(gl_p.reshape(batch, t, D_MODEL)[:, t - (CONF_K - 1):])
            sc.append(jnp.concatenate([sconv_t[j, 1:], gl_s[None]], axis=0).transpose(1, 0, 2))
        wq = x_wq[l].astype(BF)
        wo = x_wo[l].astype(BF)
        kb = mkv_b[l].reshape(batch, MEM_LEN, 2 * X_W)
        xp = _xattn_prompt(xp, kb[:, :, :X_W], kb[:, :, X_W:], wq, wo, ln_g[l, 1], ln_b[l, 1], batch, t)
        xs = _xattn_sample(xs, cache_mem_kv, l, wq, wo, ln_g[l, 1], ln_b[l, 1])
        if l % 2 == 0:
            i = l // 2
            wg = f_w_gu[i][:, 0].astype(BF)
            wu = f_w_gu[i][:, 1].astype(BF)
            wd = f_w_down[i].astype(BF)
            xp = _swiglu_ln(xp, wg, wu, wd, ln_g[l, 2], ln_b[l, 2])
            xs = _swiglu_ln(xs, wg, wu, wd, ln_g[l, 2], ln_b[l, 2])
        else:
            j = l // 2
            wg = m_w_gu[j][:, :, 0].astype(BF)
            wu = m_w_gu[j][:, :, 1].astype(BF)
            wd = m_w_down[j].astype(BF)
            xp, xs = _moe_layer(xp, xs, m_router[j], wg, wu, wd, ln_g[l, 2], ln_b[l, 2])
    p_mem = mkv_f.reshape(DEPTH, batch, MEM_LEN, 2, X_HEADS, X_HEAD_DIM)
    s_win = jnp.concatenate([win_t[:, :, :, 1:], jnp.stack(sw)[:, :, :, None]], axis=3)
    s_win = s_win.reshape(n_layers_a, bs, 2, NSA_KV_HEADS, HEAD_DIM, n_buf).transpose(0, 1, 5, 2, 3, 4)
    return (xp.reshape(batch, t, D_MODEL), xs.reshape(bs, 1, D_MODEL), jnp.stack(pk), jnp.stack(pw),
            jnp.stack(ph), jnp.stack(pcv), jnp.stack(pc), p_mem,
            jnp.stack(sk), s_win, jnp.stack(sh), jnp.stack(scv), jnp.stack(sc))
```

```python
import functools

import jax
import jax.numpy as jnp
from jax import lax
from jax.experimental import pallas as pl
from jax.experimental.pallas import tpu as pltpu

F32 = jnp.float32
BF = jnp.bfloat16

D_MODEL = 1024
NSA_HEADS = 8
NSA_KV_HEADS = 2
NSA_GROUP = NSA_HEADS // NSA_KV_HEADS
HEAD_DIM = 64
BLOCK = 64
N_SEL = 8
N_LOCAL_BLOCKS = 2
WINDOW = 512
Q_BLOCK = 128
FORCE_SCORE = 1.0e4
D_RNN = 512
RNN_CONV = 4
RG_C = 8.0
CONF_K = 31
MEM_LEN = 256
X_HEADS = 4
X_HEAD_DIM = 128
X_W = X_HEADS * X_HEAD_DIM
D_FF = 2816
N_EXPERTS = 8
TOP_K = 2
LN_EPS = 1e-5
DEPTH = 4
DN_ALPHA = (2.0 * DEPTH) ** 0.25

LANES = 128
SUBLANES = 8
SEL_KEY_TILE = 512
WIN_KEYS = WINDOW + Q_BLOCK
FF_CHUNK = 256
MOE_TILE = 256
SCAN_TILE = 256
HALO = 32
VMEM_LIMIT = 56 * 1024 * 1024
NEG = -1.0e30
MASK_BIAS = -(2.0 ** 30)
ALIBI_SLOPES = tuple(2.0 ** (-8.0 * (i + 1) / NSA_HEADS) for i in range(NSA_HEADS))


def _dot(a, b):
    return jnp.dot(a, b, preferred_element_type=F32)


def _dot_nt(a, b):
    return lax.dot_general(a, b, (((1,), (1,)), ((), ())), preferred_element_type=F32)


def _ln(z, g, b):
    mu = jnp.mean(z, axis=-1, keepdims=True)
    zc = z - mu
    var = jnp.mean(zc * zc, axis=-1, keepdims=True)
    return zc * lax.rsqrt(var + LN_EPS) * g + b


def _gelu(x):
    return 0.5 * x * (1.0 + jnp.tanh(0.7978845608028654 * (x + 0.044715 * (x * x * x))))


def _silu(x):
    return x * jax.nn.sigmoid(x)


def _params(sem):
    return pltpu.CompilerParams(dimension_semantics=sem, vmem_limit_bytes=VMEM_LIMIT)


def _row_tile(m, pref=512):
    return pref if m % pref == 0 else m


def _full(shape):
    n = len(shape)
    return pl.BlockSpec(shape, lambda *_: (0,) * n)


Q_PAD_W = NSA_HEADS * LANES
KV_W = 6 * NSA_KV_HEADS * HEAD_DIM
KA_W = 4 * LANES
ROW_W = Q_PAD_W + 2 * D_RNN + LANES + KA_W
POS_HI = HEAD_DIM
POS_LO = HEAD_DIM + 1


def _proj_a_kernel(x_ref, wrow_ref, wt_ref, qf_ref, *refs, row_kv):
    if row_kv:
        wkv_ref, q_ref, xr_ref, gr_ref, gate_ref, ka_ref, kvt_ref, vt_ref, kvrow_ref = refs
    else:
        q_ref, xr_ref, gr_ref, gate_ref, ka_ref, kvt_ref, vt_ref = refs
    i = pl.program_id(1)
    tm = x_ref.shape[0]
    xb = x_ref[...].astype(BF)
    q = _dot(xb, wrow_ref[:, 0:Q_PAD_W])
    q_ref[...] = (q * (HEAD_DIM ** -0.5) + qf_ref[...]).astype(BF)
    o = Q_PAD_W
    xr_ref[...] = _dot(xb, wrow_ref[:, o:o + D_RNN])
    o += D_RNN
    gr_ref[...] = _dot(xb, wrow_ref[:, o:o + D_RNN])
    o += D_RNN
    gate_ref[...] = _dot(xb, wrow_ref[:, o:o + LANES])
    o += LANES
    pos = i * tm + lax.broadcasted_iota(jnp.int32, (tm, KA_W), 0)
    lane = lax.broadcasted_iota(jnp.int32, (tm, KA_W), 1) & (LANES - 1)
    feat = jnp.where(lane == POS_HI, (pos >> 6).astype(F32),
                     jnp.where(lane == POS_LO, (pos & (BLOCK - 1)).astype(F32), 0.0))
    ka_ref[...] = (_dot(xb, wrow_ref[:, o:o + KA_W]) + feat).astype(BF)
    kvt = _dot_nt(wt_ref[...], xb)
    kvt_ref[0] = kvt
    vt_ref[0, 0:LANES, :] = kvt[3 * LANES:4 * LANES].astype(BF)
    vt_ref[0, LANES:2 * LANES, :] = kvt[5 * LANES:6 * LANES].astype(BF)
    if row_kv:
        kvrow_ref[...] = _dot(xb, wkv_ref[...])


def _proj_a(x, w, batch, t, row_kv=False):
    m = batch * t
    tm = _row_tile(t)
    nt = t // tm
    row = lambda w_: pl.BlockSpec((tm, w_), lambda b, i: (b * nt + i, 0))
    fm = lambda f: pl.BlockSpec((1, f, tm), lambda b, i: (b, 0, i))
    outs = [((m, Q_PAD_W), BF, row(Q_PAD_W)), ((m, D_RNN), F32, row(D_RNN)), ((m, D_RNN), F32, row(D_RNN)),
            ((m, LANES), F32, row(LANES)), ((m, KA_W), BF, row(KA_W)), ((batch, KV_W, t), F32, fm(KV_W)),
            ((batch, 2 * LANES, t), BF, fm(2 * LANES))]
    ins = [x, w['row'], w['t'], w['qf']]
    in_specs = [row(D_MODEL), _full((D_MODEL, ROW_W)), _full((KV_W, D_MODEL)), _full((1, Q_PAD_W))]
    if row_kv:
        ins.append(w['kv'])
        in_specs.append(_full((D_MODEL, KV_W)))
        outs.append(((m, KV_W), F32, row(KV_W)))
    return pl.pallas_call(
        functools.partial(_proj_a_kernel, row_kv=row_kv),
        out_shape=[jax.ShapeDtypeStruct(s, dt) for s, dt, _ in outs],
        grid=(batch, nt),
        in_specs=in_specs,
        out_specs=[sp for _, _, sp in outs],
        compiler_params=_params(("parallel", "parallel")),
        name="proj_a",
    )(*ins)


def _prep_w_in(w_in):
    wq = w_in[:, :512].reshape(D_MODEL, NSA_HEADS, HEAD_DIM)
    wq_p = jnp.concatenate([wq, jnp.zeros_like(wq)], axis=2).reshape(D_MODEL, Q_PAD_W)
    wkv = w_in[:, 512:1280]
    wg = jnp.pad(w_in[:, 1280:1304], ((0, 0), (0, LANES - 3 * NSA_HEADS)))
    wxr = w_in[:, 1304:1816]
    wgr = w_in[:, 1816:2328]
    lane = jnp.arange(Q_PAD_W) % LANES
    slope = jnp.repeat(jnp.asarray(ALIBI_SLOPES, F32), LANES)
    qf = jnp.where(lane == POS_HI, BLOCK * slope, jnp.where(lane == POS_LO, slope, 0.0)).reshape(1, Q_PAD_W)
    wkv6 = wkv.reshape(D_MODEL, 6, NSA_KV_HEADS, HEAD_DIM)
    wk = jnp.stack([wkv6[:, 2], wkv6[:, 4]], axis=1)
    wka = jnp.concatenate([wk, jnp.zeros_like(wk)], axis=3).reshape(D_MODEL, KA_W)
    return {'row': jnp.concatenate([wq_p, wxr, wgr, wg, wka], axis=1).astype(BF), 't': wkv.T.astype(BF),
            'kv': wkv.astype(BF), 'qf': qf}


def _prep_w_out(w_out):
    wn = w_out[:512].reshape(NSA_HEADS, HEAD_DIM, D_MODEL)
    z = jnp.zeros_like(wn)
    lo = jnp.concatenate([wn, z], axis=1)
    hi = jnp.concatenate([z, wn], axis=1)
    wn_p = jnp.concatenate([lo[:NSA_GROUP], hi[NSA_GROUP:]], axis=0).reshape(Q_PAD_W, D_MODEL)
    return wn_p.astype(BF), w_out[:512].astype(BF), w_out[512:].astype(BF)


def _compress_kernel(x_ref, pos_ref, w1_ref, w2_ref, o_ref):
    xb = (x_ref[0] + pos_ref[0]).astype(BF)
    h = _gelu(_dot(xb, w1_ref[0]))
    o_ref[0] = _dot(h.astype(BF), w2_ref[0])


def _compress(xblk, pos, w1, w2):
    r = xblk.shape[1]
    tr = _row_tile(r, 256)
    kdim = BLOCK * HEAD_DIM
    pos_t = pos.transpose(0, 2, 1).reshape(2, 1, kdim)
    w1_t = w1.reshape(2, BLOCK, HEAD_DIM, w1.shape[-1]).transpose(0, 2, 1, 3).reshape(2, kdim, w1.shape[-1])
    return pl.pallas_call(
        _compress_kernel,
        out_shape=jax.ShapeDtypeStruct((2, r, HEAD_DIM), F32),
        grid=(2, r // tr),
        in_specs=[pl.BlockSpec((1, tr, kdim), lambda c, i: (c, i, 0)),
                  pl.BlockSpec((1, 1, kdim), lambda c, i: (c, 0, 0)),
                  pl.BlockSpec((1, kdim, 128), lambda c, i: (c, 0, 0)),
                  pl.BlockSpec((1, 128, HEAD_DIM), lambda c, i: (c, 0, 0))],
        out_specs=pl.BlockSpec((1, tr, HEAD_DIM), lambda c, i: (c, i, 0)),
        compiler_params=_params(("parallel", "parallel")),
        name="compress",
    )(xblk, pos_t, w1_t.astype(BF), w2.astype(BF))


CP_SEQS = 8


def _compress_pages_kernel(pg_ref, pos_ref, w1_ref, w2_ref, o_ref, *, rows, stride):
    c = pl.program_id(1)
    outs = []
    for h in range(NSA_KV_HEADS):
        def body(d, acc, h=h):
            f = c * LANES + h * HEAD_DIM + d
            lhs = pg_ref[pl.ds(f, rows, stride=stride), :] + pos_ref[0, pl.ds(d, 1), :]
            return acc + _dot(lhs.astype(BF), w1_ref[0, d])

        acc = lax.fori_loop(0, HEAD_DIM, body, jnp.zeros((rows, 2 * 128), F32), unroll=8)
        hid = _gelu(acc).astype(BF)
        outs.append([_dot(hid[:, k * 128:(k + 1) * 128], w2_ref[0]) for k in range(2)])
    o_ref[0] = jnp.concatenate([outs[0][0], outs[1][0], outs[0][1], outs[1][1]], axis=1)


def _compress_pages(pages, pos, w1, w2):
    n, stride, page = pages.shape
    assert page == 2 * BLOCK and stride == 2 * LANES
    rows = min(n, CP_SEQS * 16)
    hidden = w1.shape[-1]
    pos_t = jnp.tile(pos.transpose(0, 2, 1), (1, 1, page // BLOCK))
    w1_t = w1.reshape(2, BLOCK, HEAD_DIM, hidden).transpose(0, 2, 1, 3)
    z = jnp.zeros_like(w1_t)
    w1_bd = jnp.concatenate([jnp.concatenate([w1_t, z], axis=3), jnp.concatenate([z, w1_t], axis=3)], axis=2)
    return pl.pallas_call(
        functools.partial(_compress_pages_kernel, rows=rows, stride=stride),
        out_shape=jax.ShapeDtypeStruct((2, n, 4 * HEAD_DIM), F32),
        grid=(n // rows, 2),
        in_specs=[pl.BlockSpec((rows * stride, page), lambda i, c: (i, 0)),
                  pl.BlockSpec((1, HEAD_DIM, page), lambda i, c: (c, 0, 0)),
                  pl.BlockSpec((1, HEAD_DIM, page, 2 * hidden), lambda i, c: (c, 0, 0, 0)),
                  pl.BlockSpec((1, hidden, HEAD_DIM), lambda i, c: (c, 0, 0))],
        out_specs=pl.BlockSpec((1, rows, 4 * HEAD_DIM), lambda i, c: (c, i, 0)),
        compiler_params=_params(("parallel", "arbitrary")),
        name="compress_pages",
    )(pages.reshape(n * stride, page), pos_t, w1_bd.astype(BF), w2.astype(BF))


def _cmp_keys(ck, nb_pad):
    batch, nb = ck.shape[0], ck.shape[1]
    ck = jnp.pad(ck.transpose(0, 2, 1, 3), ((0, 0), (0, 0), (0, nb_pad - nb), (0, 0)))
    lane = jnp.arange(HEAD_DIM)[None, :]
    n = jnp.arange(nb_pad, dtype=F32)[:, None]
    ext = jnp.where(lane == 0, n, jnp.where(lane == 1, float(BLOCK - 1), 0.0))
    ext = jnp.broadcast_to(ext[None, None], (batch, NSA_KV_HEADS, nb_pad, HEAD_DIM))
    return jnp.concatenate([ck, ext], axis=-1).astype(BF)


def _head_slopes(rows, rows_per_head, heads):
    r = lax.broadcasted_iota(jnp.int32, (rows, 1), 0)
    s = jnp.full((rows, 1), 0.0, F32)
    for g, h in enumerate(heads):
        s = jnp.where((r >= g * rows_per_head) & (r < (g + 1) * rows_per_head), ALIBI_SLOPES[h], s)
    return s


def _masked_softmax(s, valid):
    s = jnp.where(valid, s, NEG)
    m = jnp.max(s, axis=-1, keepdims=True)
    e = jnp.where(valid, jnp.exp(s - m), 0.0)
    return e / jnp.maximum(jnp.sum(e, axis=-1, keepdims=True), 1e-30)


def _select_blocks(score, blk_f, n_pick):
    sel = jnp.zeros(score.shape, F32)
    for _ in range(n_pick):
        m = jnp.max(score, axis=-1, keepdims=True)
        idx = jnp.min(jnp.where(score == m, blk_f, 1.0e9), axis=-1, keepdims=True)
        hit = blk_f == idx
        sel = jnp.where(hit & (m > -jnp.inf), 1.0, sel)
        score = jnp.where(hit, -jnp.inf, score)
    return sel


def _nsa_prompt_kernel(q_ref, gate_ref, ck_ref, cv_ref, ka_ref, vt_ref, e_ref, o_ref, *, nb):
    c = pl.program_id(1)
    start = c * Q_BLOCK
    rows = NSA_GROUP * Q_BLOCK
    sig = jax.nn.sigmoid(gate_ref[...])
    sig_t = sig.T
    row = lax.broadcasted_iota(jnp.int32, (rows, 1), 0)
    qpos = start + (row & (Q_BLOCK - 1))
    qpos_l = start + (lax.broadcasted_iota(jnp.int32, (1, rows), 1) & (Q_BLOCK - 1))
    blk = lax.broadcasted_iota(jnp.int32, (1, nb), 1)
    blk_f = blk.astype(F32)
    blk_end = blk * BLOCK + (BLOCK - 1)
    cv = cv_ref[0]
    qpos1 = qpos[0:Q_BLOCK]
    cur = qpos1 >> 6
    forced = (blk == 0) | ((blk <= cur) & (blk > cur - N_LOCAL_BLOCKS))
    n_tiles = (start + Q_BLOCK + SEL_KEY_TILE - 1) // SEL_KEY_TILE
    krow = lax.broadcasted_iota(jnp.int32, (SEL_KEY_TILE, 1), 0)
    w0 = pl.multiple_of(jnp.maximum(start - WINDOW, 0), Q_BLOCK)
    dist = qpos_l[:, 0:Q_BLOCK] - (w0 + lax.broadcasted_iota(jnp.int32, (WIN_KEYS, 1), 0))
    wbias = jnp.where((dist >= 0) & (dist < WINDOW), 0.0, MASK_BIAS)
    wbias = jnp.concatenate([wbias] * NSA_GROUP, axis=1)

    kv_heads = range(NSA_KV_HEADS)
    qs, o_c, qsm = [], [], []
    for kh in kv_heads:
        q4 = jnp.concatenate([q_ref[:, h * LANES:(h + 1) * LANES]
                              for h in range(kh * NSA_GROUP, (kh + 1) * NSA_GROUP)], axis=0)
        s_c = _dot_nt(q4, ck_ref[0, kh])
        p_c = _masked_softmax(s_c, blk_end <= qpos)
        o_c.append(_dot(p_c.astype(BF), cv))
        imp = p_c[0:Q_BLOCK]
        for g in range(1, NSA_GROUP):
            imp = imp + p_c[g * Q_BLOCK:(g + 1) * Q_BLOCK]
        score = jnp.where(forced, FORCE_SCORE, imp)
        score = jnp.where(blk <= cur, score, -jnp.inf)
        sel = _select_blocks(score, blk_f, min(N_SEL, nb))
        qm = ((sel - 1.0) * (-MASK_BIAS)).astype(BF)
        qs.append(q4)
        qsm.append(jnp.concatenate([q4, jnp.concatenate([qm] * NSA_GROUP, axis=0)], axis=1))

    def scores(kh, k0):
        keys = jnp.concatenate([ka_ref[pl.ds(k0, SEL_KEY_TILE), kh * LANES:(kh + 1) * LANES],
                                e_ref[pl.ds(k0, SEL_KEY_TILE), :]], axis=1)
        return _dot_nt(keys, qsm[kh])

    def update(s, carry, k0):
        m_i, l_i, acc = carry
        m_new = jnp.maximum(m_i, jnp.max(s, axis=0, keepdims=True))
        p = jnp.exp(s - m_new)
        alpha = jnp.exp(m_i - m_new)
        l_new = alpha * l_i + jnp.sum(p, axis=0, keepdims=True)
        acc = alpha * acc + _dot(vt_ref[0, 0:LANES, pl.ds(k0, SEL_KEY_TILE)], p.astype(BF))
        return m_new, l_new, acc

    def sel_step(j, carry):
        k0 = pl.multiple_of(j * SEL_KEY_TILE, SEL_KEY_TILE)
        s = [scores(kh, k0) for kh in kv_heads]
        return tuple(update(s[kh], carry[kh], k0) for kh in kv_heads)

    init = (jnp.full((1, rows), NEG, F32), jnp.zeros((1, rows), F32), jnp.zeros((LANES, rows), F32))
    carry = lax.fori_loop(0, n_tiles - 1, sel_step, (init,) * NSA_KV_HEADS)
    k0 = pl.multiple_of((n_tiles - 1) * SEL_KEY_TILE, SEL_KEY_TILE)
    causal = k0 + krow <= qpos_l
    s_d = [jnp.where(causal, scores(kh, k0), MASK_BIAS) for kh in kv_heads]
    s_w = [_dot_nt(ka_ref[pl.ds(w0, WIN_KEYS), (2 + kh) * LANES:(3 + kh) * LANES], qs[kh]) + wbias
           for kh in kv_heads]
    for kh in kv_heads:
        _, l_s, acc_s = update(s_d[kh], carry[kh], k0)
        os_t = acc_s / jnp.maximum(l_s, 1e-30)
        e_w = jnp.exp(s_w[kh] - jnp.max(s_w[kh], axis=0, keepdims=True))
        ow_t = (_dot(vt_ref[0, LANES:2 * LANES, pl.ds(w0, WIN_KEYS)], e_w.astype(BF))
                / jnp.maximum(jnp.sum(e_w, axis=0, keepdims=True), 1e-30))

        for g in range(NSA_GROUP):
            h = kh * NSA_GROUP + g
            qs_ = slice(g * Q_BLOCK, (g + 1) * Q_BLOCK)
            sw_t = sig_t[3 * h + 1:3 * h + 2, :] * os_t[:, qs_] + sig_t[3 * h + 2:3 * h + 3, :] * ow_t[:, qs_]
            o = sig[:, 3 * h:3 * h + 1] * o_c[kh][qs_] + sw_t.T
            o_ref[:, h * LANES:(h + 1) * LANES] = o.astype(BF)


def _nsa_prompt(q, gates, ckp, cv, ka, vt, batch, t):
    nb = t // BLOCK
    nc = t // Q_BLOCK
    m = batch * t
    eye = ((jnp.arange(t, dtype=jnp.int32)[:, None] >> 6) == jnp.arange(nb, dtype=jnp.int32)[None, :]).astype(BF)
    return pl.pallas_call(
        functools.partial(_nsa_prompt_kernel, nb=nb),
        out_shape=jax.ShapeDtypeStruct((m, Q_PAD_W), BF),
        grid=(batch, nc),
        in_specs=[pl.BlockSpec((Q_BLOCK, Q_PAD_W), lambda b, c: (b * nc + c, 0)),
                  pl.BlockSpec((Q_BLOCK, LANES), lambda b, c: (b * nc + c, 0)),
                  pl.BlockSpec((1, NSA_KV_HEADS, nb, LANES), lambda b, c: (b, 0, 0, 0)),
                  pl.BlockSpec((1, nb, LANES), lambda b, c: (b, 0, 0)),
                  pl.BlockSpec((t, KA_W), lambda b, c: (b, 0)),
                  pl.BlockSpec((1, 2 * LANES, t), lambda b, c: (b, 0, 0)),
                  _full((t, nb))],
        out_specs=pl.BlockSpec((Q_BLOCK, Q_PAD_W), lambda b, c: (b * nc + c, 0)),
        compiler_params=_params(("parallel", "arbitrary")),
        name="nsa_prompt",
    )(q, gates, ckp, cv, ka, vt, eye)


SAMPLE_ROWS = 16


def _nsa_sample_kernel(q_ref, gate_ref, ck_ref, cv_ref, pg_ref, new_ref, win_ref, o_ref, *, past):
    rows = SAMPLE_ROWS
    nbp = cv_ref.shape[1]
    n_pages, page = pg_ref.shape[1], pg_ref.shape[3]
    n_buf = win_ref.shape[3]
    pos = past
    q = q_ref[0].astype(BF)
    qd = q[:, 0:HEAD_DIM]
    qf = qd.astype(F32)
    row = lax.broadcasted_iota(jnp.int32, (rows, 1), 0)
    grp0 = row < NSA_GROUP
    slope = _head_slopes(rows, 1, list(range(NSA_HEADS)))
    sig = jax.nn.sigmoid(gate_ref[0])
    blk = lax.broadcasted_iota(jnp.int32, (1, nbp), 1)
    blk_f = blk.astype(F32)
    blk_end = blk * BLOCK + (BLOCK - 1)

    def by_head(f):
        return jnp.where(grp0, f(0), f(1))

    s_c = by_head(lambda kh: _dot_nt(q, ck_ref[0, kh]))
    p_c = _masked_softmax(s_c, blk_end <= pos)
    o_c2 = _dot(p_c.astype(BF), cv_ref[0])
    o_c = jnp.where(grp0, o_c2[:, 0:HEAD_DIM], o_c2[:, HEAD_DIM:2 * HEAD_DIM])
    imp0 = jnp.sum(jnp.where(grp0, p_c, 0.0), axis=0, keepdims=True)
    imp1 = jnp.sum(jnp.where((row >= NSA_GROUP) & (row < NSA_HEADS), p_c, 0.0), axis=0, keepdims=True)
    imp = jnp.where(grp0, imp0, imp1)
    cur = pos // BLOCK
    forced = (blk == 0) | ((blk <= cur) & (blk > cur - N_LOCAL_BLOCKS))
    score = jnp.where(forced, FORCE_SCORE, imp)
    score = jnp.where(blk <= cur, score, -jnp.inf)
    sel = _select_blocks(score, blk_f, N_SEL)

    def new_row(c, kh):
        o = c * LANES + kh * HEAD_DIM
        return new_ref[0, :, o:o + HEAD_DIM].astype(BF).astype(F32)

    def attend(s, valid, vt_of, s_self, valid_self, v_self):
        s = jnp.where(valid, s, NEG)
        s_self = jnp.where(valid_self, s_self, NEG)
        m = jnp.maximum(jnp.max(s, axis=-1, keepdims=True), s_self)
        e = jnp.where(valid, jnp.exp(s - m), 0.0)
        e_self = jnp.where(valid_self, jnp.exp(s_self - m), 0.0)
        den = jnp.maximum(jnp.sum(e, axis=-1, keepdims=True) + e_self, 1e-30)
        eb = e.astype(BF)
        num = by_head(lambda kh: _dot_nt(eb, vt_of(kh))) + e_self.astype(BF).astype(F32) * v_self
        return num / den

    def page_rows(c, kh):
        o = c * LANES + kh * HEAD_DIM
        return jnp.concatenate([pg_ref[0, p, o:o + HEAD_DIM, :] for p in range(n_pages)], axis=1).astype(BF)

    e_row = lax.broadcasted_iota(jnp.int32, (nbp, past), 0)
    e_col = lax.broadcasted_iota(jnp.int32, (nbp, past), 1) >> 6
    selx = _dot(sel.astype(BF), jnp.where(e_row == e_col, 1.0, 0.0).astype(BF))
    kpos = lax.broadcasted_iota(jnp.int32, (1, past), 1)
    s_s = by_head(lambda kh: _dot(qd, page_rows(0, kh))) + slope * (kpos.astype(F32) - float(pos))
    s_self = by_head(lambda kh: jnp.sum(qf * new_row(2, kh), axis=-1, keepdims=True))
    sel_self = jnp.sum(jnp.where(blk == cur, sel, 0.0), axis=-1, keepdims=True) > 0.5
    o_s = attend(s_s, selx > 0.5, lambda kh: page_rows(1, kh), s_self, sel_self,
                 by_head(lambda kh: jnp.broadcast_to(new_row(3, kh), (rows, HEAD_DIM))))

    dist = n_buf - lax.broadcasted_iota(jnp.int32, (1, n_buf), 1)
    win = lambda c, kh: win_ref[0, 0, c * LANES + kh * HEAD_DIM:c * LANES + (kh + 1) * HEAD_DIM, :].astype(BF)
    s_w = by_head(lambda kh: _dot(qd, win(0, kh))) - slope * dist.astype(F32)
    s_wn = by_head(lambda kh: jnp.sum(qf * new_row(4, kh), axis=-1, keepdims=True))
    o_w = attend(s_w, (dist >= 0) & (dist < WINDOW), lambda kh: win(1, kh), s_wn, row >= 0,
                 by_head(lambda kh: jnp.broadcast_to(new_row(5, kh), (rows, HEAD_DIM))))

    o_ref[0] = sig[:, 0:1] * o_c + sig[:, 1:2] * o_s + sig[:, 2:3] * o_w


def _nsa_sample(q, gates3, ckp, cv, pages, kv_new, win_all, layer):
    b, n_pages, _, page = pages.shape
    past = n_pages * page
    nbp = cv.shape[1]
    n_buf = win_all.shape[3]
    per3 = lambda shape: pl.BlockSpec((1,) + shape, lambda i: (i, 0, 0))
    per4 = lambda shape: pl.BlockSpec((1,) + shape, lambda i: (i, 0, 0, 0))
    return pl.pallas_call(
        functools.partial(_nsa_sample_kernel, past=past),
        out_shape=jax.ShapeDtypeStruct((b, SAMPLE_ROWS, HEAD_DIM), F32),
        grid=(b,),
        in_specs=[per3((SAMPLE_ROWS, LANES)), per3((SAMPLE_ROWS, 3)), per4((NSA_KV_HEADS, nbp, LANES)),
                  per3((nbp, LANES)), per4((n_pages, 2 * LANES, page)), per3((1, KV_W)),
                  pl.BlockSpec((1, 1, 2 * LANES, n_buf), lambda i: (layer, i, 0, 0))],
        out_specs=per3((SAMPLE_ROWS, HEAD_DIM)),
        compiler_params=_params(("parallel",)),
        name="nsa_sample",
    )(q, gates3, ckp, cv, pages, kv_new, win_all)


def _rglru_gates(xc, wa_ref, ba_ref, wx_ref, bx_ref, lam_ref):
    xb = xc.astype(BF)
    r = jax.nn.sigmoid(_dot(xb, wa_ref[...]) + ba_ref[...])
    i = jax.nn.sigmoid(_dot(xb, wx_ref[...]) + bx_ref[...])
    z = -lam_ref[...]
    softplus = jnp.maximum(z, 0.0) + jnp.log1p(jnp.exp(-jnp.abs(z)))
    log_a = -RG_C * r * softplus
    th = jnp.tanh(log_a)
    mult = jnp.sqrt(-2.0 * th / (1.0 - th))
    return jnp.exp(log_a), i, mult


def _shift_rows(x, s, fill):
    n = x.shape[0]
    if s % 8 == 0:
        return jnp.concatenate([jnp.full((s, x.shape[1]), fill, x.dtype), x[:n - s]], axis=0)
    r = lax.broadcasted_iota(jnp.int32, (n, 1), 0)
    return jnp.where(r < s, fill, pltpu.roll(x, s, 0))


def _rglru_prompt_kernel(xr_ref, gr_ref, cw_ref, cb_ref, wa_ref, ba_ref, wx_ref, bx_ref, lam_ref,
                         y_ref, hl_ref, buf_ref, h_ref):
    t = pl.program_id(1)
    tt = xr_ref.shape[0]

    @pl.when(t == 0)
    def _():
        buf_ref[0:HALO, :] = jnp.zeros((HALO, D_RNN), F32)
        h_ref[...] = jnp.zeros_like(h_ref)

    buf_ref[HALO:HALO + tt, :] = xr_ref[...]
    xc = cb_ref[...] + cw_ref[RNN_CONV - 1:RNN_CONV, :] * xr_ref[...]
    for j in range(1, RNN_CONV):
        xc = xc + cw_ref[RNN_CONV - 1 - j:RNN_CONV - j, :] * buf_ref[HALO - j:HALO - j + tt, :]
    a, i, mult = _rglru_gates(xc, wa_ref, ba_ref, wx_ref, bx_ref, lam_ref)
    row = lax.broadcasted_iota(jnp.int32, (tt, 1), 0)
    mult = jnp.where(row + t * tt == 0, 1.0, mult)
    b = xc * i * mult
    s = 1
    while s < tt:
        b = a * _shift_rows(b, s, 0.0) + b
        a = a * _shift_rows(a, s, 1.0)
        s *= 2
    h = a * h_ref[0:1, :] + b
    h_ref[0:1, :] = h[tt - 1:tt, :]
    hl_ref[0] = h[tt - 1:tt, :]
    y_ref[...] = (h * _gelu(gr_ref[...])).astype(BF)
    buf_ref[0:HALO, :] = buf_ref[tt:tt + HALO, :]


def _block_diag(w):
    n, c, d = w.shape
    eye = jnp.eye(n, dtype=w.dtype)
    return (w[:, :, None, :] * eye[:, None, :, None]).reshape(n * c, n * d)


def _rglru_prompt(xr, gr, pa, batch, t):
    tt = min(SCAN_TILE, t)
    nt = t // tt
    vec = lambda a: a.reshape(1, D_RNN)
    y, hl = pl.pallas_call(
        _rglru_prompt_kernel,
        out_shape=[jax.ShapeDtypeStruct((batch * t, D_RNN), BF), jax.ShapeDtypeStruct((batch, 1, D_RNN), F32)],
        grid=(batch, nt),
        in_specs=[pl.BlockSpec((tt, D_RNN), lambda b, i: (b * nt + i, 0)),
                  pl.BlockSpec((tt, D_RNN), lambda b, i: (b * nt + i, 0)),
                  _full((RNN_CONV, D_RNN)), _full((1, D_RNN)), _full((D_RNN, D_RNN)), _full((1, D_RNN)),
                  _full((D_RNN, D_RNN)), _full((1, D_RNN)), _full((1, D_RNN))],
        out_specs=[pl.BlockSpec((tt, D_RNN), lambda b, i: (b * nt + i, 0)),
                   pl.BlockSpec((1, 1, D_RNN), lambda b, i: (b, 0, 0))],
        scratch_shapes=[pltpu.VMEM((HALO + tt, D_RNN), F32), pltpu.VMEM((8, D_RNN), F32)],
        compiler_params=_params(("arbitrary", "arbitrary")),
        name="rglru_prompt",
    )(xr, gr, pa['conv_w'], vec(pa['conv_b']), _block_diag(pa['wa']).astype(BF), vec(pa['ba']),
      _block_diag(pa['wx']).astype(BF), vec(pa['bx']), vec(pa['lam']))
    return y, hl.reshape(batch, D_RNN)


def _rglru_sample_kernel(xr_ref, gr_ref, prev_ref, h0_ref, cw_ref, cb_ref, wa_ref, ba_ref, wx_ref, bx_ref, lam_ref,
                         y_ref, h_ref):
    xc = cb_ref[...] + cw_ref[RNN_CONV - 1:RNN_CONV, :] * xr_ref[...]
    for k in range(RNN_CONV - 1):
        xc = xc + cw_ref[k:k + 1, :] * prev_ref[k]
    a, i, mult = _rglru_gates(xc, wa_ref, ba_ref, wx_ref, bx_ref, lam_ref)
    h = xc * i * mult + a * h0_ref[...]
    h_ref[...] = h
    y_ref[...] = (h * _gelu(gr_ref[...])).astype(BF)


def _rglru_sample(xr, gr, h0, prev_t, pa):
    b = xr.shape[0]
    vec = lambda a: a.reshape(1, D_RNN)
    return pl.pallas_call(
        _rglru_sample_kernel,
        out_shape=[jax.ShapeDtypeStruct((b, D_RNN), BF), jax.ShapeDtypeStruct((b, D_RNN), F32)],
        name="rglru_sample",
    )(xr, gr, prev_t, h0, pa['conv_w'], vec(pa['conv_b']),
      _block_diag(pa['wa']).astype(BF), vec(pa['ba']), _block_diag(pa['wx']).astype(BF), vec(pa['bx']),
      vec(pa['lam']))


def _mm2_ln_kernel(a_ref, b_ref, wa_ref, wb_ref, x_ref, g_ref, beta_ref, o_ref):
    y = _dot(a_ref[...], wa_ref[...]) + _dot(b_ref[...], wb_ref[...])
    o_ref[...] = _ln(DN_ALPHA * x_ref[...] + y, g_ref[...], beta_ref[...])


def _mm2_ln(a, b, wa, wb, x, g, beta):
    m = x.shape[0]
    tm = _row_tile(m)
    ka, kb = a.shape[1], b.shape[1]
    return pl.pallas_call(
        _mm2_ln_kernel,
        out_shape=jax.ShapeDtypeStruct((m, D_MODEL), F32),
        grid=(m // tm,),
        in_specs=[pl.BlockSpec((tm, ka), lambda i: (i, 0)), pl.BlockSpec((tm, kb), lambda i: (i, 0)),
                  _full((ka, D_MODEL)), _full((kb, D_MODEL)), pl.BlockSpec((tm, D_MODEL), lambda i: (i, 0)),
                  _full((1, D_MODEL)), _full((1, D_MODEL))],
        out_specs=pl.BlockSpec((tm, D_MODEL), lambda i: (i, 0)),
        compiler_params=_params(("parallel",)),
        name="mm2_ln",
    )(a, b, wa, wb, x, g.reshape(1, -1), beta.reshape(1, -1))


def _mm_kernel(x_ref, w_ref, o_ref, ob_ref):
    y = _dot(x_ref[...].astype(BF), w_ref[0])
    o_ref[0] = y
    ob_ref[0] = y.astype(BF)


def _memory_kv(mem, wkv):
    r = mem.shape[0]
    nl, _, n = wkv.shape
    return pl.pallas_call(
        _mm_kernel,
        out_shape=[jax.ShapeDtypeStruct((nl, r, n), F32), jax.ShapeDtypeStruct((nl, r, n), BF)],
        grid=(nl,),
        in_specs=[_full((r, D_MODEL)), pl.BlockSpec((1, D_MODEL, n), lambda l: (l, 0, 0))],
        out_specs=[pl.BlockSpec((1, r, n), lambda l: (l, 0, 0)), pl.BlockSpec((1, r, n), lambda l: (l, 0, 0))],
        compiler_params=_params(("parallel",)),
        name="memory_kv",
    )(mem, wkv.astype(BF))


def _xattn_prompt_kernel(x_ref, k_ref, v_ref, wq_ref, wo_ref, g_ref, beta_ref, o_ref):
    x = x_ref[...]
    q = _dot(x.astype(BF), wq_ref[...]).astype(BF)
    outs = []
    for h in range(X_HEADS):
        hs = slice(h * X_HEAD_DIM, (h + 1) * X_HEAD_DIM)
        s = _dot_nt(q[:, hs], k_ref[0, :, hs]) * (X_HEAD_DIM ** -0.5)
        m = jnp.max(s, axis=-1, keepdims=True)
        e = jnp.exp(s - m)
        p = e / jnp.sum(e, axis=-1, keepdims=True)
        outs.append(_dot(p.astype(BF), v_ref[0, :, hs]).astype(BF))
    y = _dot(jnp.concatenate(outs, axis=1), wo_ref[...])
    o_ref[...] = _ln(DN_ALPHA * x + y, g_ref[...], beta_ref[...])


def _xattn_prompt(x, kb, vb, wq, wo, g, beta, batch, t):
    tm = _row_tile(t)
    nt = t // tm
    return pl.pallas_call(
        _xattn_prompt_kernel,
        out_shape=jax.ShapeDtypeStruct((batch * t, D_MODEL), F32),
        grid=(batch, nt),
        in_specs=[pl.BlockSpec((tm, D_MODEL), lambda b, i: (b * nt + i, 0)),
                  pl.BlockSpec((1, MEM_LEN, X_W), lambda b, i: (b, 0, 0)),
                  pl.BlockSpec((1, MEM_LEN, X_W), lambda b, i: (b, 0, 0)),
                  _full((D_MODEL, X_W)), _full((X_W, D_MODEL)), _full((1, D_MODEL)), _full((1, D_MODEL))],
        out_specs=pl.BlockSpec((tm, D_MODEL), lambda b, i: (b * nt + i, 0)),
        compiler_params=_params(("parallel", "parallel")),
        name="xattn_prompt",
    )(x, kb, vb, wq, wo, g.reshape(1, -1), beta.reshape(1, -1))


XS_SEQS = 8


def _xattn_sample_kernel(x_ref, kv_ref, wq_ref, wo_ref, g_ref, beta_ref, o_ref):
    x = x_ref[...]
    xb = x.astype(BF)
    qh = [_dot(xb, wq_ref[:, h * X_HEAD_DIM:(h + 1) * X_HEAD_DIM]).astype(BF).astype(F32) for h in range(X_HEADS)]
    rows = []
    for s_i in range(XS_SEQS):
        q4 = jnp.concatenate([qh[h][s_i:s_i + 1, :] for h in range(X_HEADS)], axis=0)
        k = kv_ref[0, s_i, :, 0].astype(BF).astype(F32)
        v = kv_ref[0, s_i, :, 1].astype(BF).astype(F32)
        s = jnp.sum(k * q4[None], axis=-1, keepdims=True) * (X_HEAD_DIM ** -0.5)
        e = jnp.exp(s - jnp.max(s, axis=0, keepdims=True))
        p = (e / jnp.sum(e, axis=0, keepdims=True)).astype(BF).astype(F32)
        o4 = jnp.sum(p * v, axis=0)
        rows.append(jnp.concatenate([o4[h:h + 1, :] for h in range(X_HEADS)], axis=1))
    o = jnp.concatenate(rows, axis=0)
    y = _dot(o.astype(BF), wo_ref[...])
    o_ref[...] = _ln(DN_ALPHA * x + y, g_ref[...], beta_ref[...])


def _xattn_sample(x, cache, layer, wq, wo, g, beta):
    b = x.shape[0]
    return pl.pallas_call(
        _xattn_sample_kernel,
        out_shape=jax.ShapeDtypeStruct((b, D_MODEL), F32),
        grid=(b // XS_SEQS,),
        in_specs=[pl.BlockSpec((XS_SEQS, D_MODEL), lambda i: (i, 0)),
                  pl.BlockSpec((1, XS_SEQS, MEM_LEN, 2, X_HEADS, X_HEAD_DIM), lambda i: (layer, i, 0, 0, 0, 0)),
                  _full((D_MODEL, X_W)), _full((X_W, D_MODEL)), _full((1, D_MODEL)), _full((1, D_MODEL))],
        out_specs=pl.BlockSpec((XS_SEQS, D_MODEL), lambda i: (i, 0)),
        compiler_params=_params(("parallel",)),
        name="xattn_sample",
    )(x, cache, wq, wo, g.reshape(1, -1), beta.reshape(1, -1))


def _swiglu_rows(xb, wg_ref, wu_ref, wd_ref, lead):
    acc = None
    for c in range(D_FF // FF_CHUNK):
        cs = slice(c * FF_CHUNK, (c + 1) * FF_CHUNK)
        hg = _dot(xb, wg_ref[lead + (slice(None), cs)])
        hu = _dot(xb, wu_ref[lead + (slice(None), cs)])
        part = _dot((_silu(hg) * hu).astype(BF), wd_ref[lead + (cs, slice(None))])
        acc = part if acc is None else acc + part
    return acc


def _swiglu_ln_kernel(x_ref, wg_ref, wu_ref, wd_ref, g_ref, beta_ref, o_ref):
    x = x_ref[...]
    y = _swiglu_rows(x.astype(BF), wg_ref, wu_ref, wd_ref, ())
    o_ref[...] = _ln(DN_ALPHA * x + y, g_ref[...], beta_ref[...])


def _swiglu_ln(x, wg, wu, wd, g, beta):
    m = x.shape[0]
    tm = _row_tile(m)
    return pl.pallas_call(
        _swiglu_ln_kernel,
        out_shape=jax.ShapeDtypeStruct((m, D_MODEL), F32),
        grid=(m // tm,),
        in_specs=[pl.BlockSpec((tm, D_MODEL), lambda i: (i, 0)), _full((D_MODEL, D_FF)), _full((D_MODEL, D_FF)),
                  _full((D_FF, D_MODEL)), _full((1, D_MODEL)), _full((1, D_MODEL))],
        out_specs=pl.BlockSpec((tm, D_MODEL), lambda i: (i, 0)),
        compiler_params=_params(("parallel",)),
        name="swiglu_ln",
    )(x, wg, wu, wd, g.reshape(1, -1), beta.reshape(1, -1))


def _router_kernel(x_ref, r_ref, idx_ref, w_ref):
    logits = jnp.dot(x_ref[...], r_ref[...], preferred_element_type=F32, precision=lax.Precision.HIGHEST)
    lane = lax.broadcasted_iota(jnp.int32, logits.shape, 1)
    lane_f = lane.astype(F32)
    logits = jnp.where(lane < N_EXPERTS, logits, -jnp.inf)
    m1 = jnp.max(logits, axis=-1, keepdims=True)
    i1 = jnp.min(jnp.where(logits == m1, lane_f, 1.0e9), axis=-1, keepdims=True)
    rest = jnp.where(lane_f == i1, -jnp.inf, logits)
    m2 = jnp.max(rest, axis=-1, keepdims=True)
    i2 = jnp.min(jnp.where(rest == m2, lane_f, 1.0e9), axis=-1, keepdims=True)
    e2 = jnp.exp(m2 - m1)
    den = 1.0 + e2
    idx_ref[...] = jnp.where(lane == 0, i1, i2).astype(jnp.int32)
    w_ref[...] = jnp.where(lane == 0, 1.0 / den, e2 / den)


def _router(x, router):
    m = x.shape[0]
    tm = _row_tile(m)
    rp = jnp.pad(router, ((0, 0), (0, LANES - N_EXPERTS)))
    idx, w = pl.pallas_call(
        _router_kernel,
        out_shape=[jax.ShapeDtypeStruct((m, LANES), jnp.int32), jax.ShapeDtypeStruct((m, LANES), F32)],
        grid=(m // tm,),
        in_specs=[pl.BlockSpec((tm, D_MODEL), lambda i: (i, 0)), _full((D_MODEL, LANES))],
        out_specs=[pl.BlockSpec((tm, LANES), lambda i: (i, 0)), pl.BlockSpec((tm, LANES), lambda i: (i, 0))],
        compiler_params=_params(("parallel",)),
        name="router",
    )(x, rp)
    return idx[:, :TOP_K], w[:, :TOP_K]


def _moe_kernel(te_ref, nu_ref, x_ref, s_ref, wg_ref, wu_ref, wd_ref, o_ref):
    i = pl.program_id(0)

    @pl.when(i < nu_ref[0])
    def _():
        o_ref[...] = s_ref[...] * _swiglu_rows(x_ref[...], wg_ref, wu_ref, wd_ref, (0,))

    @pl.when(i >= nu_ref[0])
    def _():
        o_ref[...] = jnp.zeros_like(o_ref)


def _moe_ffn(tile_expert, n_used, xs, scale, wg, wu, wd):
    p = xs.shape[0]
    return pl.pallas_call(
        _moe_kernel,
        out_shape=jax.ShapeDtypeStruct((p, D_MODEL), F32),
        grid_spec=pltpu.PrefetchScalarGridSpec(
            num_scalar_prefetch=2,
            grid=(p // MOE_TILE,),
            in_specs=[pl.BlockSpec((MOE_TILE, D_MODEL), lambda i, te, nu: (i, 0)),
                      pl.BlockSpec((MOE_TILE, 1), lambda i, te, nu: (i, 0)),
                      pl.BlockSpec((1, D_MODEL, D_FF), lambda i, te, nu: (te[i], 0, 0)),
                      pl.BlockSpec((1, D_MODEL, D_FF), lambda i, te, nu: (te[i], 0, 0)),
                      pl.BlockSpec((1, D_FF, D_MODEL), lambda i, te, nu: (te[i], 0, 0))],
            out_specs=pl.BlockSpec((MOE_TILE, D_MODEL), lambda i, te, nu: (i, 0))),
        compiler_params=_params(("arbitrary",)),
        name="moe_ffn",
    )(tile_expert, n_used, xs, scale, wg, wu, wd)


def _add2_ln_kernel(x_ref, y0_ref, y1_ref, g_ref, beta_ref, o_ref):
    o_ref[...] = _ln(DN_ALPHA * x_ref[...] + (y0_ref[...] + y1_ref[...]), g_ref[...], beta_ref[...])


def _add2_ln(x, y2, row0, g, beta):
    m = x.shape[0]
    tm = _row_tile(m)
    nt = m // tm
    b0 = row0 // tm
    return pl.pallas_call(
        _add2_ln_kernel,
        out_shape=jax.ShapeDtypeStruct((m, D_MODEL), F32),
        grid=(nt,),
        in_specs=[pl.BlockSpec((tm, D_MODEL), lambda i: (i, 0)), pl.BlockSpec((tm, D_MODEL), lambda i: (i + b0, 0)),
                  pl.BlockSpec((tm, D_MODEL), lambda i: (i + b0 + nt, 0)), _full((1, D_MODEL)),
                  _full((1, D_MODEL))],
        out_specs=pl.BlockSpec((tm, D_MODEL), lambda i: (i, 0)),
        compiler_params=_params(("parallel",)),
        name="add2_ln",
    )(x, y2, y2, g.reshape(1, -1), beta.reshape(1, -1))


def _moe_layer(xp, xs, router, wg, wu, wd, g, beta):
    mp, ms = xp.shape[0], xs.shape[0]
    m = mp + ms
    ip, wp = _router(xp, router)
    is_, ws = _router(xs, router)
    e_flat = jnp.concatenate([ip.T.reshape(-1), is_.T.reshape(-1)])
    w_flat = jnp.concatenate([wp.T.reshape(-1), ws.T.reshape(-1)])
    tok = jnp.concatenate([jnp.tile(jnp.arange(mp, dtype=jnp.int32), TOP_K),
                           mp + jnp.tile(jnp.arange(ms, dtype=jnp.int32), TOP_K)])
    onehot = (e_flat[:, None] == jnp.arange(N_EXPERTS, dtype=jnp.int32)[None, :]).astype(jnp.int32)
    cum = jnp.cumsum(onehot, axis=0)
    rank = jnp.sum(onehot * (cum - 1), axis=1)
    counts = cum[-1]
    padded = ((counts + MOE_TILE - 1) // MOE_TILE) * MOE_TILE
    ends = jnp.cumsum(padded)
    starts = ends - padded
    pos = starts[e_flat] + rank
    n_tiles = (TOP_K * m + N_EXPERTS * (MOE_TILE - 1)) // MOE_TILE + 1
    p_rows = n_tiles * MOE_TILE
    tile_start = jnp.arange(n_tiles, dtype=jnp.int32) * MOE_TILE
    tile_expert = jnp.minimum(jnp.sum((tile_start[:, None] >= ends[None, :]).astype(jnp.int32), axis=1),
                              N_EXPERTS - 1).astype(jnp.int32)
    n_used = (ends[-1] // MOE_TILE).astype(jnp.int32).reshape(1)
    src = jnp.zeros((p_rows,), jnp.int32).at[pos].set(tok)
    scale = jnp.zeros((p_rows,), F32).at[pos].set(w_flat)
    x_sorted = jnp.concatenate([xp.astype(BF), xs.astype(BF)], axis=0)[src]
    y2 = _moe_ffn(tile_expert, n_used, x_sorted, scale.reshape(p_rows, 1), wg, wu, wd)[pos]
    return _add2_ln(xp, y2, 0, g, beta), _add2_ln(xs, y2, TOP_K * mp, g, beta)


def _glu_kernel(x_ref, w_ref, b_ref, o_ref):
    xb = x_ref[...].astype(BF)
    a = _dot(xb, w_ref[:, 0:D_MODEL]) + b_ref[:, 0:D_MODEL]
    gate = _dot(xb, w_ref[:, D_MODEL:2 * D_MODEL]) + b_ref[:, D_MODEL:2 * D_MODEL]
    o_ref[...] = a * jax.nn.sigmoid(gate)


def _glu(x, w, b):
    m = x.shape[0]
    tm = _row_tile(m)
    return pl.pallas_call(
        _glu_kernel,
        out_shape=jax.ShapeDtypeStruct((m, D_MODEL), F32),
        grid=(m // tm,),
        in_specs=[pl.BlockSpec((tm, D_MODEL), lambda i: (i, 0)), _full((D_MODEL, 2 * D_MODEL)),
                  _full((1, 2 * D_MODEL))],
        out_specs=pl.BlockSpec((tm, D_MODEL), lambda i: (i, 0)),
        compiler_params=_params(("parallel",)),
        name="glu",
    )(x, w, b.reshape(1, -1))


CONV_ROWS = 64


def _conv_tail(c, x, cg_ref, cb_ref, wp_ref, bp_ref, g_ref, beta_ref):
    c = _ln(c, cg_ref[...], cb_ref[...])
    y = _dot(_silu(c).astype(BF), wp_ref[...]) + bp_ref[...]
    return _ln(DN_ALPHA * x + y, g_ref[...], beta_ref[...])


def _conv_prompt_kernel(gl_ref, x_ref, dw_ref, db_ref, cg_ref, cb_ref, wp_ref, bp_ref, g_ref, beta_ref,
                        o_ref, buf_ref, sh_ref):
    t = pl.program_id(1)
    tt = gl_ref.shape[0]
    n = HALO + tt

    @pl.when(t == 0)
    def _():
        buf_ref[0:HALO, :] = jnp.zeros((HALO, D_MODEL), F32)

    buf_ref[HALO:HALO + tt, :] = gl_ref[...]
    full = buf_ref[...]
    for r in range(1, SUBLANES):
        sh_ref[r - 1] = pltpu.roll(full, n - r, 0)
    off = HALO - (CONF_K - 1)
    for r0 in range(0, tt, CONV_ROWS):
        c = db_ref[...]
        for k in range(CONF_K):
            o = r0 + off + k
            r = o % SUBLANES
            a = o - r
            win = buf_ref[a:a + CONV_ROWS, :] if r == 0 else sh_ref[r - 1, a:a + CONV_ROWS, :]
            c = c + dw_ref[k:k + 1, :] * win
        o_ref[r0:r0 + CONV_ROWS, :] = _conv_tail(c, x_ref[r0:r0 + CONV_ROWS, :], cg_ref, cb_ref, wp_ref, bp_ref,
                                                 g_ref, beta_ref)
    buf_ref[0:HALO, :] = buf_ref[tt:tt + HALO, :]


def _conv_prompt(gl, x, cp, g, beta, batch, t):
    tt = min(SCAN_TILE, t)
    nt = t // tt
    vec = lambda a: a.reshape(1, -1)
    return pl.pallas_call(
        _conv_prompt_kernel,
        out_shape=jax.ShapeDtypeStruct((batch * t, D_MODEL), F32),
        grid=(batch, nt),
        in_specs=[pl.BlockSpec((tt, D_MODEL), lambda b, i: (b * nt + i, 0)),
                  pl.BlockSpec((tt, D_MODEL), lambda b, i: (b * nt + i, 0)),
                  _full((CONF_K, D_MODEL)), _full((1, D_MODEL)), _full((1, D_MODEL)), _full((1, D_MODEL)),
                  _full((D_MODEL, D_MODEL)), _full((1, D_MODEL)), _full((1, D_MODEL)), _full((1, D_MODEL))],
        out_specs=pl.BlockSpec((tt, D_MODEL), lambda b, i: (b * nt + i, 0)),
        scratch_shapes=[pltpu.VMEM((HALO + tt, D_MODEL), F32), pltpu.VMEM((SUBLANES - 1, HALO + tt, D_MODEL), F32)],
        compiler_params=_params(("arbitrary", "arbitrary")),
        name="conv_prompt",
    )(gl, x, cp['dw_w'], vec(cp['dw_b']), vec(cp['ln_g']), vec(cp['ln_b']), cp['w_pw'], vec(cp['b_pw']),
      vec(g), vec(beta))


CS_SEQS = 32


def _conv_sample_kernel(gl_ref, st_ref, x_ref, dw_ref, db_ref, cg_ref, cb_ref, wp_ref, bp_ref, g_ref, beta_ref,
                        o_ref):
    c = db_ref[...] + dw_ref[CONF_K - 1:CONF_K, :] * gl_ref[...]
    for k in range(CONF_K - 1):
        c = c + dw_ref[k:k + 1, :] * st_ref[0, k]
    o_ref[...] = _conv_tail(c, x_ref[...], cg_ref, cb_ref, wp_ref, bp_ref, g_ref, beta_ref)


def _conv_sample(gl, state_t, layer, x, cp, g, beta):
    b = x.shape[0]
    vec = lambda a: a.reshape(1, -1)
    return pl.pallas_call(
        _conv_sample_kernel,
        out_shape=jax.ShapeDtypeStruct((b, D_MODEL), F32),
        grid=(b // CS_SEQS,),
        in_specs=[pl.BlockSpec((CS_SEQS, D_MODEL), lambda i: (i, 0)),
                  pl.BlockSpec((1, CONF_K - 1, CS_SEQS, D_MODEL), lambda i: (layer, 0, i, 0)),
                  pl.BlockSpec((CS_SEQS, D_MODEL), lambda i: (i, 0)),
                  _full((CONF_K, D_MODEL)), _full((1, D_MODEL)), _full((1, D_MODEL)), _full((1, D_MODEL)),
                  _full((D_MODEL, D_MODEL)), _full((1, D_MODEL)), _full((1, D_MODEL)), _full((1, D_MODEL))],
        out_specs=pl.BlockSpec((CS_SEQS, D_MODEL), lambda i: (i, 0)),
        compiler_params=_params(("parallel",)),
        name="conv_sample",
    )(gl, state_t, x, cp['dw_w'], vec(cp['dw_b']), vec(cp['ln_g']), vec(cp['ln_b']), cp['w_pw'], vec(cp['b_pw']),
      vec(g), vec(beta))


def _flatten_blocks(kt2, lead):
    n = kt2.shape[-1] // BLOCK
    nl = len(lead)
    x = kt2.reshape(*lead, 2, NSA_KV_HEADS, HEAD_DIM, n, BLOCK)
    perm = (nl,) + tuple(range(nl)) + (nl + 3, nl + 1, nl + 2, nl + 4)
    return x.transpose(perm).reshape(2, -1, HEAD_DIM * BLOCK)


def _mixer_a_prompt(x, w_in, wo_nsa, wo_rnn, pa, g, beta, batch, t):
    nb = t // BLOCK
    q, xr, gr, gates, ka, kvt, vt = _proj_a(x, w_in, batch, t)
    cmp = _compress(_flatten_blocks(kvt[:, 0:256], (batch,)), pa['cmp_pos'], pa['cmp_w1'], pa['cmp_w2'])
    ckp = _cmp_keys(cmp[0].reshape(batch, nb, NSA_KV_HEADS, HEAD_DIM), nb)
    cv = cmp[1].reshape(batch, nb, LANES).astype(BF)
    o_nsa = _nsa_prompt(q, gates, ckp, cv, ka, vt, batch, t)
    o_rnn, h_last = _rglru_prompt(xr, gr, pa, batch, t)
    x_new = _mm2_ln(o_nsa, o_rnn, wo_nsa, wo_rnn, x, g, beta)
    n_keep = min(WINDOW, t)
    kv_out = kvt[:, 0:512].reshape(batch, 4, NSA_KV_HEADS, HEAD_DIM, t).transpose(0, 4, 1, 2, 3)
    win_out = kvt[:, 512:768, t - n_keep:].reshape(batch, 2, NSA_KV_HEADS, HEAD_DIM, n_keep).transpose(0, 4, 1, 2, 3)
    conv_out = xr.reshape(batch, t, D_RNN)[:, t - (RNN_CONV - 1):]
    return x_new, kv_out, win_out, h_last, conv_out


def _mixer_a_sample(x, pool_all, page_ids, win_all, layer, h0, prev_t, w_in, wo_nsa, wo_rnn, pa, g, beta):
    b = x.shape[0]
    n_pages, page = page_ids.shape[1], pool_all.shape[2]
    nb_past = n_pages * page // BLOCK
    q, xr, gr, gates, _, kvt, _, kv_row = _proj_a(x, w_in, 1, b, row_kv=True)
    pool4 = pool_all.reshape(pool_all.shape[0], 2, 2 * LANES, page)
    pg_cmp = pool4[page_ids, 0]
    pages = pool4[page_ids, 1]
    cmp = _compress_pages(pg_cmp.reshape(b * n_pages, 2 * LANES, page), pa['cmp_pos'], pa['cmp_w1'], pa['cmp_w2'])
    cmp = cmp.reshape(2, b, nb_past, NSA_KV_HEADS, HEAD_DIM)
    ckp = _cmp_keys(cmp[0], LANES)
    cv = jnp.pad(cmp[1].reshape(b, nb_past, LANES), ((0, 0), (0, LANES - nb_past), (0, 0))).astype(BF)
    q16 = jnp.pad(q.astype(F32).reshape(b, NSA_HEADS, LANES), ((0, 0), (0, SAMPLE_ROWS - NSA_HEADS), (0, 0)))
    g3 = jnp.pad(gates[:, :3 * NSA_HEADS].reshape(b, NSA_HEADS, 3), ((0, 0), (0, SAMPLE_ROWS - NSA_HEADS), (0, 0)))
    o16 = _nsa_sample(q16, g3, ckp, cv, pages, kv_row.reshape(b, 1, KV_W), win_all, layer)
    o_nsa = o16[:, :NSA_HEADS].reshape(b, NSA_HEADS * HEAD_DIM).astype(BF)
    o_rnn, h_new = _rglru_sample(xr, gr, h0, prev_t, pa)
    x_new = _mm2_ln(o_nsa, o_rnn, wo_nsa, wo_rnn, x, g, beta)
    kv_out = kvt[0, 0:512].reshape(4, NSA_KV_HEADS, HEAD_DIM, b).transpose(3, 0, 1, 2)[:, None]
    win_col = kvt[0, 512:768].T
    conv_out = jnp.concatenate([prev_t[1:], xr[None]], axis=0).transpose(1, 0, 2)
    return x_new, kv_out, win_col, h_new, conv_out


def kernel(x_prompt, x_sample, cache_nsa_kv, cache_nsa_win, state_rglru_h, state_rglru_conv, state_conv,
           cache_mem_kv, page_table, mem_prompt, ln_g, ln_b, a_w_in, a_cmp_pos, a_cmp_w1, a_cmp_w2,
           a_conv_w, a_conv_b, a_gate_a_w, a_gate_a_b, a_gate_x_w, a_gate_x_b, a_lambda, a_w_out,
           c_w_glu, c_b_glu, c_dw_w, c_dw_b, c_ln_g, c_ln_b, c_w_pw, c_b_pw, x_wq, x_wkv, x_wo,
           f_w_gu, f_w_down, m_router, m_w_gu, m_w_down):
    batch, t, _ = x_prompt.shape
    bs = x_sample.shape[0]
    xp = x_prompt.reshape(batch * t, D_MODEL)
    xs = x_sample.reshape(bs, D_MODEL)
    mkv_f, mkv_b = _memory_kv(mem_prompt.reshape(batch * MEM_LEN, D_MODEL), x_wkv.reshape(DEPTH, D_MODEL, 2 * X_W))
    n_layers_a, n_pool, page = cache_nsa_kv.shape[:3]
    pool_all = cache_nsa_kv.transpose(0, 1, 3, 4, 5, 2).reshape(n_layers_a * n_pool, 4 * LANES, page)
    n_buf = cache_nsa_win.shape[2]
    win_t = cache_nsa_win.transpose(0, 1, 3, 4, 5, 2).reshape(n_layers_a, bs, 2 * LANES, n_buf)
    rconv_t = state_rglru_conv.transpose(0, 2, 1, 3)
    sconv_t = state_conv.transpose(0, 2, 1, 3)
    pk, pw, ph, pcv, pc = [], [], [], [], []
    sk, sw, sh, scv, sc = [], [], [], [], []
    for l in range(DEPTH):
        if l % 2 == 0:
            i = l // 2
            pa = {'cmp_pos': a_cmp_pos[i], 'cmp_w1': a_cmp_w1[i], 'cmp_w2': a_cmp_w2[i],
                  'conv_w': a_conv_w[i], 'conv_b': a_conv_b[i], 'wa': a_gate_a_w[i], 'ba': a_gate_a_b[i],
                  'wx': a_gate_x_w[i], 'bx': a_gate_x_b[i], 'lam': a_lambda[i]}
            w_in = _prep_w_in(a_w_in[i])
            wo_pad, wo_cmp, wo_rnn = _prep_w_out(a_w_out[i])
            xp, kv_p, win_p, h_p, cb_p = _mixer_a_prompt(xp, w_in, wo_pad, wo_rnn, pa, ln_g[l, 0], ln_b[l, 0],
                                                         batch, t)
            xs, kv_s, win_s, h_s, cb_s = _mixer_a_sample(xs, pool_all, page_table + i * n_pool, win_t, i,
                                                         state_rglru_h[i], rconv_t[i], w_in, wo_cmp, wo_rnn, pa,
                                                         ln_g[l, 0], ln_b[l, 0])
            pk.append(kv_p); pw.append(win_p); ph.append(h_p); pcv.append(cb_p)
            sk.append(kv_s); sw.append(win_s); sh.append(h_s); scv.append(cb_s)
        else:
            j = l // 2
            cp = {'dw_w': c_dw_w[j], 'dw_b': c_dw_b[j], 'ln_g': c_ln_g[j], 'ln_b': c_ln_b[j],
                  'w_pw': c_w_pw[j].astype(BF), 'b_pw': c_b_pw[j]}
            w_glu = c_w_glu[j].astype(BF)
            gl_p = _glu(xp, w_glu, c_b_glu[j])
            gl_s = _glu(xs, w_glu, c_b_glu[j])
            xp = _conv_prompt(gl_p, xp, cp, ln_g[l, 0], ln_b[l, 0], batch, t)
            xs = _conv_sample(gl_s, sconv_t, j, xs, cp, ln_g[l, 0], ln_b[l, 0])
            pc.append(gl_p.reshape(batch, t, D_MODEL)[:, t - (CONF_K - 1):])
            sc.append(jnp.concatenate([sconv_t[j, 1:], gl_s[None]], axis=0).transpose(1, 0, 2))
        wq = x_wq[l].astype(BF)
        wo = x_wo[l].astype(BF)
        kb = mkv_b[l].reshape(batch, MEM_LEN, 2 * X_W)
        xp = _xattn_prompt(xp, kb[:, :, :X_W], kb[:, :, X_W:], wq, wo, ln_g[l, 1], ln_b[l, 1], batch, t)
        xs = _xattn_sample(xs, cache_mem_kv, l, wq, wo, ln_g[l, 1], ln_b[l, 1])
        if l % 2 == 0:
            i = l // 2
            wg = f_w_gu[i][:, 0].astype(BF)
            wu = f_w_gu[i][:, 1].astype(BF)
            wd = f_w_down[i].astype(BF)
            xp = _swiglu_ln(xp, wg, wu, wd, ln_g[l, 2], ln_b[l, 2])
            xs = _swiglu_ln(xs, wg, wu, wd, ln_g[l, 2], ln_b[l, 2])
        else:
            j = l // 2
            wg = m_w_gu[j][:, :, 0].astype(BF)
            wu = m_w_gu[j][:, :, 1].astype(BF)
            wd = m_w_down[j].astype(BF)
            xp, xs = _moe_layer(xp, xs, m_router[j], wg, wu, wd, ln_g[l, 2], ln_b[l, 2])
    p_mem = mkv_f.reshape(DEPTH, batch, MEM_LEN, 2, X_HEADS, X_HEAD_DIM)
    s_win = jnp.concatenate([win_t[:, :, :, 1:], jnp.stack(sw)[:, :, :, None]], axis=3)
    s_win = s_win.reshape(n_layers_a, bs, 2, NSA_KV_HEADS, HEAD_DIM, n_buf).transpose(0, 1, 5, 2, 3, 4)
    return (xp.reshape(batch, t, D_MODEL), xs.reshape(bs, 1, D_MODEL), jnp.stack(pk), jnp.stack(pw),
            jnp.stack(ph), jnp.stack(pcv), jnp.stack(pc), p_mem,
            jnp.stack(sk), s_win, jnp.stack(sh), jnp.stack(scv), jnp.stack(sc))
```

```python
import functools

import jax
import jax.numpy as jnp
from jax import lax
from jax.experimental import pallas as pl
from jax.experimental.pallas import tpu as pltpu

F32 = jnp.float32
BF = jnp.bfloat16

D_MODEL = 1024
NSA_HEADS = 8
NSA_KV_HEADS = 2
NSA_GROUP = NSA_HEADS // NSA_KV_HEADS
HEAD_DIM = 64
BLOCK = 64
N_SEL = 8
N_LOCAL_BLOCKS = 2
WINDOW = 512
Q_BLOCK = 128
FORCE_SCORE = 1.0e4
D_RNN = 512
RNN_CONV = 4
RG_C = 8.0
CONF_K = 31
MEM_LEN = 256
X_HEADS = 4
X_HEAD_DIM = 128
X_W = X_HEADS * X_HEAD_DIM
D_FF = 2816
N_EXPERTS = 8
TOP_K = 2
LN_EPS = 1e-5
DEPTH = 4
DN_ALPHA = (2.0 * DEPTH) ** 0.25

LANES = 128
SUBLANES = 8
SEL_KEY_TILE = 512
WIN_KEYS = WINDOW + Q_BLOCK
FF_CHUNK = 256
MOE_TILE = 256
SCAN_TILE = 256
HALO = 32
VMEM_LIMIT = 56 * 1024 * 1024
NEG = -1.0e30
MASK_BIAS = -(2.0 ** 30)
ALIBI_SLOPES = tuple(2.0 ** (-8.0 * (i + 1) / NSA_HEADS) for i in range(NSA_HEADS))


def _dot(a, b):
    return jnp.dot(a, b, preferred_element_type=F32)


def _dot_nt(a, b):
    return lax.dot_general(a, b, (((1,), (1,)), ((), ())), preferred_element_type=F32)


def _ln(z, g, b):
    mu = jnp.mean(z, axis=-1, keepdims=True)
    zc = z - mu
    var = jnp.mean(zc * zc, axis=-1, keepdims=True)
    return zc * lax.rsqrt(var + LN_EPS) * g + b


def _gelu(x):
    return 0.5 * x * (1.0 + jnp.tanh(0.7978845608028654 * (x + 0.044715 * (x * x * x))))


def _silu(x):
    return x * jax.nn.sigmoid(x)


def _params(sem):
    return pltpu.CompilerParams(dimension_semantics=sem, vmem_limit_bytes=VMEM_LIMIT)


def _row_tile(m, pref=512):
    return pref if m % pref == 0 else m


def _full(shape):
    n = len(shape)
    return pl.BlockSpec(shape, lambda *_: (0,) * n)


Q_PAD_W = NSA_HEADS * LANES
KV_W = 6 * NSA_KV_HEADS * HEAD_DIM
KA_W = 4 * LANES
ROW_W = Q_PAD_W + 2 * D_RNN + LANES + KA_W
POS_HI = HEAD_DIM
POS_LO = HEAD_DIM + 1


def _proj_a_kernel(x_ref, wrow_ref, wt_ref, qf_ref, *refs, row_kv):
    if row_kv:
        wkv_ref, q_ref, xr_ref, gr_ref, gate_ref, ka_ref, kvt_ref, vt_ref, kvrow_ref = refs
    else:
        q_ref, xr_ref, gr_ref, gate_ref, ka_ref, kvt_ref, vt_ref = refs
    i = pl.program_id(1)
    tm = x_ref.shape[0]
    xb = x_ref[...].astype(BF)
    q = _dot(xb, wrow_ref[:, 0:Q_PAD_W])
    q_ref[...] = (q * (HEAD_DIM ** -0.5) + qf_ref[...]).astype(BF)
    o = Q_PAD_W
    xr_ref[...] = _dot(xb, wrow_ref[:, o:o + D_RNN])
    o += D_RNN
    gr_ref[...] = _dot(xb, wrow_ref[:, o:o + D_RNN])
    o += D_RNN
    gate_ref[...] = _dot(xb, wrow_ref[:, o:o + LANES])
    o += LANES
    pos = i * tm + lax.broadcasted_iota(jnp.int32, (tm, KA_W), 0)
    lane = lax.broadcasted_iota(jnp.int32, (tm, KA_W), 1) & (LANES - 1)
    feat = jnp.where(lane == POS_HI, (pos >> 6).astype(F32),
                     jnp.where(lane == POS_LO, (pos & (BLOCK - 1)).astype(F32), 0.0))
    ka_ref[...] = (_dot(xb, wrow_ref[:, o:o + KA_W]) + feat).astype(BF)
    kvt = _dot_nt(wt_ref[...], xb)
    kvt_ref[0] = kvt
    vt_ref[0, 0:LANES, :] = kvt[3 * LANES:4 * LANES].astype(BF)
    vt_ref[0, LANES:2 * LANES, :] = kvt[5 * LANES:6 * LANES].astype(BF)
    if row_kv:
        kvrow_ref[...] = _dot(xb, wkv_ref[...])


def _proj_a(x, w, batch, t, row_kv=False):
    m = batch * t
    tm = _row_tile(t)
    nt = t // tm
    row = lambda w_: pl.BlockSpec((tm, w_), lambda b, i: (b * nt + i, 0))
    fm = lambda f: pl.BlockSpec((1, f, tm), lambda b, i: (b, 0, i))
    outs = [((m, Q_PAD_W), BF, row(Q_PAD_W)), ((m, D_RNN), F32, row(D_RNN)), ((m, D_RNN), F32, row(D_RNN)),
            ((m, LANES), F32, row(LANES)), ((m, KA_W), BF, row(KA_W)), ((batch, KV_W, t), F32, fm(KV_W)),
            ((batch, 2 * LANES, t), BF, fm(2 * LANES))]
    ins = [x, w['row'], w['t'], w['qf']]
    in_specs = [row(D_MODEL), _full((D_MODEL, ROW_W)), _full((KV_W, D_MODEL)), _full((1, Q_PAD_W))]
    if row_kv:
        ins.append(w['kv'])
        in_specs.append(_full((D_MODEL, KV_W)))
        outs.append(((m, KV_W), F32, row(KV_W)))
    return pl.pallas_call(
        functools.partial(_proj_a_kernel, row_kv=row_kv),
        out_shape=[jax.ShapeDtypeStruct(s, dt) for s, dt, _ in outs],
        grid=(batch, nt),
        in_specs=in_specs,
        out_specs=[sp for _, _, sp in outs],
        compiler_params=_params(("parallel", "parallel")),
        name="proj_a",
    )(*ins)


def _prep_w_in(w_in):
    wq = w_in[:, :512].reshape(D_MODEL, NSA_HEADS, HEAD_DIM)
    wq_p = jnp.concatenate([wq, jnp.zeros_like(wq)], axis=2).reshape(D_MODEL, Q_PAD_W)
    wkv = w_in[:, 512:1280]
    wg = jnp.pad(w_in[:, 1280:1304], ((0, 0), (0, LANES - 3 * NSA_HEADS)))
    wxr = w_in[:, 1304:1816]
    wgr = w_in[:, 1816:2328]
    lane = jnp.arange(Q_PAD_W) % LANES
    slope = jnp.repeat(jnp.asarray(ALIBI_SLOPES, F32), LANES)
    qf = jnp.where(lane == POS_HI, BLOCK * slope, jnp.where(lane == POS_LO, slope, 0.0)).reshape(1, Q_PAD_W)
    wkv6 = wkv.reshape(D_MODEL, 6, NSA_KV_HEADS, HEAD_DIM)
    wk = jnp.stack([wkv6[:, 2], wkv6[:, 4]], axis=1)
    wka = jnp.concatenate([wk, jnp.zeros_like(wk)], axis=3).reshape(D_MODEL, KA_W)
    return {'row': jnp.concatenate([wq_p, wxr, wgr, wg, wka], axis=1).astype(BF), 't': wkv.T.astype(BF),
            'kv': wkv.astype(BF), 'qf': qf}


def _prep_w_out(w_out):
    wn = w_out[:512].reshape(NSA_HEADS, HEAD_DIM, D_MODEL)
    z = jnp.zeros_like(wn)
    lo = jnp.concatenate([wn, z], axis=1)
    hi = jnp.concatenate([z, wn], axis=1)
    wn_p = jnp.concatenate([lo[:NSA_GROUP], hi[NSA_GROUP:]], axis=0).reshape(Q_PAD_W, D_MODEL)
    return wn_p.astype(BF), w_out[:512].astype(BF), w_out[512:].astype(BF)


def _compress_kernel(x_ref, pos_ref, w1_ref, w2_ref, o_ref):
    xb = (x_ref[0] + pos_ref[0]).astype(BF)
    h = _gelu(_dot(xb, w1_ref[0]))
    o_ref[0] = _dot(h.astype(BF), w2_ref[0])


def _compress(xblk, pos, w1, w2):
    r = xblk.shape[1]
    tr = _row_tile(r, 256)
    kdim = BLOCK * HEAD_DIM
    pos_t = pos.transpose(0, 2, 1).reshape(2, 1, kdim)
    w1_t = w1.reshape(2, BLOCK, HEAD_DIM, w1.shape[-1]).transpose(0, 2, 1, 3).reshape(2, kdim, w1.shape[-1])
    return pl.pallas_call(
        _compress_kernel,
        out_shape=jax.ShapeDtypeStruct((2, r, HEAD_DIM), F32),
        grid=(2, r // tr),
        in_specs=[pl.BlockSpec((1, tr, kdim), lambda c, i: (c, i, 0)),
                  pl.BlockSpec((1, 1, kdim), lambda c, i: (c, 0, 0)),
                  pl.BlockSpec((1, kdim, 128), lambda c, i: (c, 0, 0)),
                  pl.BlockSpec((1, 128, HEAD_DIM), lambda c, i: (c, 0, 0))],
        out_specs=pl.BlockSpec((1, tr, HEAD_DIM), lambda c, i: (c, i, 0)),
        compiler_params=_params(("parallel", "parallel")),
        name="compress",
    )(xblk, pos_t, w1_t.astype(BF), w2.astype(BF))


CP_SEQS = 8


def _compress_pages_kernel(pg_ref, pos_ref, w1_ref, w2_ref, o_ref, *, rows, stride):
    c = pl.program_id(1)
    outs = []
    for h in range(NSA_KV_HEADS):
        def body(d, acc, h=h):
            f = c * LANES + h * HEAD_DIM + d
            lhs = pg_ref[pl.ds(f, rows, stride=stride), :] + pos_ref[0, pl.ds(d, 1), :]
            return acc + _dot(lhs.astype(BF), w1_ref[0, d])

        acc = lax.fori_loop(0, HEAD_DIM, body, jnp.zeros((rows, 2 * 128), F32), unroll=8)
        hid = _gelu(acc).astype(BF)
        outs.append([_dot(hid[:, k * 128:(k + 1) * 128], w2_ref[0]) for k in range(2)])
    o_ref[0] = jnp.concatenate([outs[0][0], outs[1][0], outs[0][1], outs[1][1]], axis=1)


def _compress_pages(pages, pos, w1, w2):
    n, stride, page = pages.shape
    assert page == 2 * BLOCK and stride == 2 * LANES
    rows = min(n, CP_SEQS * 16)
    hidden = w1.shape[-1]
    pos_t = jnp.tile(pos.transpose(0, 2, 1), (1, 1, page // BLOCK))
    w1_t = w1.reshape(2, BLOCK, HEAD_DIM, hidden).transpose(0, 2, 1, 3)
    z = jnp.zeros_like(w1_t)
    w1_bd = jnp.concatenate([jnp.concatenate([w1_t, z], axis=3), jnp.concatenate([z, w1_t], axis=3)], axis=2)
    return pl.pallas_call(
        functools.partial(_compress_pages_kernel, rows=rows, stride=stride),
        out_shape=jax.ShapeDtypeStruct((2, n, 4 * HEAD_DIM), F32),
        grid=(n // rows, 2),
        in_specs=[pl.BlockSpec((rows * stride, page), lambda i, c: (i, 0)),
                  pl.BlockSpec((1, HEAD_DIM, page), lambda i, c: (c, 0, 0)),
                  pl.BlockSpec((1, HEAD_DIM, page, 2 * hidden), lambda i, c: (c, 0, 0, 0)),
                  pl.BlockSpec((1, hidden, HEAD_DIM), lambda i, c: (c, 0, 0))],
        out_specs=pl.BlockSpec((1, rows, 4 * HEAD_DIM), lambda i, c: (c, i, 0)),
        compiler_params=_params(("parallel", "arbitrary")),
        name="compress_pages",
    )(pages.reshape(n * stride, page), pos_t, w1_bd.astype(BF), w2.astype(BF))


def _cmp_keys(ck, nb_pad):
    batch, nb = ck.shape[0], ck.shape[1]
    ck = jnp.pad(ck.transpose(0, 2, 1, 3), ((0, 0), (0, 0), (0, nb_pad - nb), (0, 0)))
    lane = jnp.arange(HEAD_DIM)[None, :]
    n = jnp.arange(nb_pad, dtype=F32)[:, None]
    ext = jnp.where(lane == 0, n, jnp.where(lane == 1, float(BLOCK - 1), 0.0))
    ext = jnp.broadcast_to(ext[None, None], (batch, NSA_KV_HEADS, nb_pad, HEAD_DIM))
    return jnp.concatenate([ck, ext], axis=-1).astype(BF)


def _head_slopes(rows, rows_per_head, heads):
    r = lax.broadcasted_iota(jnp.int32, (rows, 1), 0)
    s = jnp.full((rows, 1), 0.0, F32)
    for g, h in enumerate(heads):
        s = jnp.where((r >= g * rows_per_head) & (r < (g + 1) * rows_per_head), ALIBI_SLOPES[h], s)
    return s


def _masked_softmax(s, valid, axis=-1):
    s = jnp.where(valid, s, NEG)
    m = jnp.max(s, axis=axis, keepdims=True)
    e = jnp.where(valid, jnp.exp(s - m), 0.0)
    return e / jnp.maximum(jnp.sum(e, axis=axis, keepdims=True), 1e-30)


def _select_blocks(score, blk_f, n_pick, axis=-1):
    sel = jnp.zeros(score.shape, F32)
    for _ in range(n_pick):
        m = jnp.max(score, axis=axis, keepdims=True)
        idx = jnp.min(jnp.where(score == m, blk_f, 1.0e9), axis=axis, keepdims=True)
        hit = blk_f == idx
        sel = jnp.where(hit & (m > -jnp.inf), 1.0, sel)
        score = jnp.where(hit, -jnp.inf, score)
    return sel


def _nsa_prompt_kernel(q_ref, gate_ref, ck_ref, cvt_ref, ka_ref, vt_ref, e_ref, o_ref, *, nb):
    c = pl.program_id(1)
    start = c * Q_BLOCK
    rows = NSA_GROUP * Q_BLOCK
    sig_t = jax.nn.sigmoid(gate_ref[...]).T
    qpos_l = start + (lax.broadcasted_iota(jnp.int32, (1, rows), 1) & (Q_BLOCK - 1))
    blk = lax.broadcasted_iota(jnp.int32, (nb, 1), 0)
    blk_f = blk.astype(F32)
    cmp_valid = blk * BLOCK + (BLOCK - 1) <= qpos_l
    cur = qpos_l[:, 0:Q_BLOCK] >> 6
    forced = (blk == 0) | ((blk <= cur) & (blk > cur - N_LOCAL_BLOCKS))
    n_tiles = (start + Q_BLOCK + SEL_KEY_TILE - 1) // SEL_KEY_TILE
    krow = lax.broadcasted_iota(jnp.int32, (SEL_KEY_TILE, 1), 0)
    w0 = pl.multiple_of(jnp.maximum(start - WINDOW, 0), Q_BLOCK)
    dist = qpos_l[:, 0:Q_BLOCK] - (w0 + lax.broadcasted_iota(jnp.int32, (WIN_KEYS, 1), 0))
    wbias = jnp.where((dist >= 0) & (dist < WINDOW), 0.0, MASK_BIAS)
    wbias = jnp.concatenate([wbias] * NSA_GROUP, axis=1)

    kv_heads = range(NSA_KV_HEADS)
    qs, oc_t, qsm = [], [], []
    for kh in kv_heads:
        q4 = jnp.concatenate([q_ref[:, h * LANES:(h + 1) * LANES]
                              for h in range(kh * NSA_GROUP, (kh + 1) * NSA_GROUP)], axis=0)
        p_c = _masked_softmax(_dot_nt(ck_ref[0, kh], q4), cmp_valid, axis=0)
        oc_t.append(_dot(cvt_ref[0], p_c.astype(BF)))
        imp = p_c[:, 0:Q_BLOCK]
        for g in range(1, NSA_GROUP):
            imp = imp + p_c[:, g * Q_BLOCK:(g + 1) * Q_BLOCK]
        score = jnp.where(forced, FORCE_SCORE, imp)
        score = jnp.where(blk <= cur, score, -jnp.inf)
        sel = _select_blocks(score, blk_f, min(N_SEL, nb), axis=0).T
        qm = ((sel - 1.0) * (-MASK_BIAS)).astype(BF)
        qs.append(q4)
        qsm.append(jnp.concatenate([q4, jnp.concatenate([qm] * NSA_GROUP, axis=0)], axis=1))

    def scores(kh, k0):
        keys = jnp.concatenate([ka_ref[pl.ds(k0, SEL_KEY_TILE), kh * LANES:(kh + 1) * LANES],
                                e_ref[pl.ds(k0, SEL_KEY_TILE), :]], axis=1)
        return _dot_nt(keys, qsm[kh])

    def update(s, carry, k0):
        m_i, l_i, acc = carry
        m_new = jnp.maximum(m_i, jnp.max(s, axis=0, keepdims=True))
        p = jnp.exp(s - m_new)
        alpha = jnp.exp(m_i - m_new)
        l_new = alpha * l_i + jnp.sum(p, axis=0, keepdims=True)
        acc = alpha * acc + _dot(vt_ref[0, 0:LANES, pl.ds(k0, SEL_KEY_TILE)], p.astype(BF))
        return m_new, l_new, acc

    def sel_step(j, carry):
        k0 = pl.multiple_of(j * SEL_KEY_TILE, SEL_KEY_TILE)
        s = [scores(kh, k0) for kh in kv_heads]
        return tuple(update(s[kh], carry[kh], k0) for kh in kv_heads)

    init = (jnp.full((1, rows), NEG, F32), jnp.zeros((1, rows), F32), jnp.zeros((LANES, rows), F32))
    carry = lax.fori_loop(0, n_tiles - 1, sel_step, (init,) * NSA_KV_HEADS)
    k0 = pl.multiple_of((n_tiles - 1) * SEL_KEY_TILE, SEL_KEY_TILE)
    causal = k0 + krow <= qpos_l
    s_d = [jnp.where(causal, scores(kh, k0), MASK_BIAS) for kh in kv_heads]
    s_w = [_dot_nt(ka_ref[pl.ds(w0, WIN_KEYS), (2 + kh) * LANES:(3 + kh) * LANES], qs[kh]) + wbias
           for kh in kv_heads]
    for kh in kv_heads:
        _, l_s, acc_s = update(s_d[kh], carry[kh], k0)
        os_t = acc_s / jnp.maximum(l_s, 1e-30)
        e_w = jnp.exp(s_w[kh] - jnp.max(s_w[kh], axis=0, keepdims=True))
        ow_t = (_dot(vt_ref[0, LANES:2 * LANES, pl.ds(w0, WIN_KEYS)], e_w.astype(BF))
                / jnp.maximum(jnp.sum(e_w, axis=0, keepdims=True), 1e-30))

        for g in range(NSA_GROUP):
            h = kh * NSA_GROUP + g
            qs_ = slice(g * Q_BLOCK, (g + 1) * Q_BLOCK)
            o_t = (sig_t[3 * h:3 * h + 1, :] * oc_t[kh][:, qs_] + sig_t[3 * h + 1:3 * h + 2, :] * os_t[:, qs_]
                   + sig_t[3 * h + 2:3 * h + 3, :] * ow_t[:, qs_])
            o_ref[:, h * LANES:(h + 1) * LANES] = o_t.T.astype(BF)


def _nsa_prompt(q, gates, ckp, cvt, ka, vt, batch, t):
    nb = t // BLOCK
    nc = t // Q_BLOCK
    m = batch * t
    eye = ((jnp.arange(t, dtype=jnp.int32)[:, None] >> 6) == jnp.arange(nb, dtype=jnp.int32)[None, :]).astype(BF)
    return pl.pallas_call(
        functools.partial(_nsa_prompt_kernel, nb=nb),
        out_shape=jax.ShapeDtypeStruct((m, Q_PAD_W), BF),
        grid=(batch, nc),
        in_specs=[pl.BlockSpec((Q_BLOCK, Q_PAD_W), lambda b, c: (b * nc + c, 0)),
                  pl.BlockSpec((Q_BLOCK, LANES), lambda b, c: (b * nc + c, 0)),
                  pl.BlockSpec((1, NSA_KV_HEADS, nb, LANES), lambda b, c: (b, 0, 0, 0)),
                  pl.BlockSpec((1, LANES, nb), lambda b, c: (b, 0, 0)),
                  pl.BlockSpec((t, KA_W), lambda b, c: (b, 0)),
                  pl.BlockSpec((1, 2 * LANES, t), lambda b, c: (b, 0, 0)),
                  _full((t, nb))],
        out_specs=pl.BlockSpec((Q_BLOCK, Q_PAD_W), lambda b, c: (b * nc + c, 0)),
        compiler_params=_params(("parallel", "arbitrary")),
        name="nsa_prompt",
    )(q, gates, ckp, cvt, ka, vt, eye)


SAMPLE_ROWS = 16


def _nsa_sample_kernel(q_ref, gate_ref, ck_ref, cv_ref, pg_ref, new_ref, win_ref, o_ref, *, past):
    rows = SAMPLE_ROWS
    nbp = cv_ref.shape[1]
    n_pages, page = pg_ref.shape[1], pg_ref.shape[3]
    n_buf = win_ref.shape[3]
    pos = past
    q = q_ref[0].astype(BF)
    qd = q[:, 0:HEAD_DIM]
    qf = qd.astype(F32)
    row = lax.broadcasted_iota(jnp.int32, (rows, 1), 0)
    grp0 = row < NSA_GROUP
    slope = _head_slopes(rows, 1, list(range(NSA_HEADS)))
    sig = jax.nn.sigmoid(gate_ref[0])
    blk = lax.broadcasted_iota(jnp.int32, (1, nbp), 1)
    blk_f = blk.astype(F32)
    blk_end = blk * BLOCK + (BLOCK - 1)

    def by_head(f):
        return jnp.where(grp0, f(0), f(1))

    s_c = by_head(lambda kh: _dot_nt(q, ck_ref[0, kh]))
    p_c = _masked_softmax(s_c, blk_end <= pos)
    o_c2 = _dot(p_c.astype(BF), cv_ref[0])
    o_c = jnp.where(grp0, o_c2[:, 0:HEAD_DIM], o_c2[:, HEAD_DIM:2 * HEAD_DIM])
    imp0 = jnp.sum(jnp.where(grp0, p_c, 0.0), axis=0, keepdims=True)
    imp1 = jnp.sum(jnp.where((row >= NSA_GROUP) & (row < NSA_HEADS), p_c, 0.0), axis=0, keepdims=True)
    imp = jnp.where(grp0, imp0, imp1)
    cur = pos // BLOCK
    forced = (blk == 0) | ((blk <= cur) & (blk > cur - N_LOCAL_BLOCKS))
    score = jnp.where(forced, FORCE_SCORE, imp)
    score = jnp.where(blk <= cur, score, -jnp.inf)
    sel = _select_blocks(score, blk_f, N_SEL)

    def new_row(c, kh):
        o = c * LANES + kh * HEAD_DIM
        return new_ref[0, :, o:o + HEAD_DIM].astype(BF).astype(F32)

    def attend(s, valid, vt_of, s_self, valid_self, v_self):
        s = jnp.where(valid, s, NEG)
        s_self = jnp.where(valid_self, s_self, NEG)
        m = jnp.maximum(jnp.max(s, axis=-1, keepdims=True), s_self)
        e = jnp.where(valid, jnp.exp(s - m), 0.0)
        e_self = jnp.where(valid_self, jnp.exp(s_self - m), 0.0)
        den = jnp.maximum(jnp.sum(e, axis=-1, keepdims=True) + e_self, 1e-30)
        eb = e.astype(BF)
        num = by_head(lambda kh: _dot_nt(eb, vt_of(kh))) + e_self.astype(BF).astype(F32) * v_self
        return num / den

    def page_rows(c, kh):
        o = c * LANES + kh * HEAD_DIM
        return jnp.concatenate([pg_ref[0, p, o:o + HEAD_DIM, :] for p in range(n_pages)], axis=1).astype(BF)

    e_row = lax.broadcasted_iota(jnp.int32, (nbp, past), 0)
    e_col = lax.broadcasted_iota(jnp.int32, (nbp, past), 1) >> 6
    selx = _dot(sel.astype(BF), jnp.where(e_row == e_col, 1.0, 0.0).astype(BF))
    kpos = lax.broadcasted_iota(jnp.int32, (1, past), 1)
    s_s = by_head(lambda kh: _dot(qd, page_rows(0, kh))) + slope * (kpos.astype(F32) - float(pos))
    s_self = by_head(lambda kh: jnp.sum(qf * new_row(2, kh), axis=-1, keepdims=True))
    sel_self = jnp.sum(jnp.where(blk == cur, sel, 0.0), axis=-1, keepdims=True) > 0.5
    o_s = attend(s_s, selx > 0.5, lambda kh: page_rows(1, kh), s_self, sel_self,
                 by_head(lambda kh: jnp.broadcast_to(new_row(3, kh), (rows, HEAD_DIM))))

    dist = n_buf - lax.broadcasted_iota(jnp.int32, (1, n_buf), 1)
    win = lambda c, kh: win_ref[0, 0, c * LANES + kh * HEAD_DIM:c * LANES + (kh + 1) * HEAD_DIM, :].astype(BF)
    s_w = by_head(lambda kh: _dot(qd, win(0, kh))) - slope * dist.astype(F32)
    s_wn = by_head(lambda kh: jnp.sum(qf * new_row(4, kh), axis=-1, keepdims=True))
    o_w = attend(s_w, (dist >= 0) & (dist < WINDOW), lambda kh: win(1, kh), s_wn, row >= 0,
                 by_head(lambda kh: jnp.broadcast_to(new_row(5, kh), (rows, HEAD_DIM))))

    o_ref[0] = sig[:, 0:1] * o_c + sig[:, 1:2] * o_s + sig[:, 2:3] * o_w


def _nsa_sample(q, gates3, ckp, cv, pages, kv_new, win_all, layer):
    b, n_pages, _, page = pages.shape
    past = n_pages * page
    nbp = cv.shape[1]
    n_buf = win_all.shape[3]
    per3 = lambda shape: pl.BlockSpec((1,) + shape, lambda i: (i, 0, 0))
    per4 = lambda shape: pl.BlockSpec((1,) + shape, lambda i: (i, 0, 0, 0))
    return pl.pallas_call(
        functools.partial(_nsa_sample_kernel, past=past),
        out_shape=jax.ShapeDtypeStruct((b, SAMPLE_ROWS, HEAD_DIM), F32),
        grid=(b,),
        in_specs=[per3((SAMPLE_ROWS, LANES)), per3((SAMPLE_ROWS, 3)), per4((NSA_KV_HEADS, nbp, LANES)),
                  per3((nbp, LANES)), per4((n_pages, 2 * LANES, page)), per3((1, KV_W)),
                  pl.BlockSpec((1, 1, 2 * LANES, n_buf), lambda i: (layer, i, 0, 0))],
        out_specs=per3((SAMPLE_ROWS, HEAD_DIM)),
        compiler_params=_params(("parallel",)),
        name="nsa_sample",
    )(q, gates3, ckp, cv, pages, kv_new, win_all)


def _rglru_gates(xc, wa_ref, ba_ref, wx_ref, bx_ref, lam_ref):
    xb = xc.astype(BF)
    r = jax.nn.sigmoid(_dot(xb, wa_ref[...]) + ba_ref[...])
    i = jax.nn.sigmoid(_dot(xb, wx_ref[...]) + bx_ref[...])
    z = -lam_ref[...]
    softplus = jnp.maximum(z, 0.0) + jnp.log1p(jnp.exp(-jnp.abs(z)))
    log_a = -RG_C * r * softplus
    th = jnp.tanh(log_a)
    mult = jnp.sqrt(-2.0 * th / (1.0 - th))
    return jnp.exp(log_a), i, mult


def _shift_rows(x, s, fill):
    n = x.shape[0]
    if s % 8 == 0:
        return jnp.concatenate([jnp.full((s, x.shape[1]), fill, x.dtype), x[:n - s]], axis=0)
    r = lax.broadcasted_iota(jnp.int32, (n, 1), 0)
    return jnp.where(r < s, fill, pltpu.roll(x, s, 0))


def _rglru_prompt_kernel(xr_ref, gr_ref, cw_ref, cb_ref, wa_ref, ba_ref, wx_ref, bx_ref, lam_ref,
                         y_ref, hl_ref, buf_ref, h_ref):
    t = pl.program_id(1)
    tt = xr_ref.shape[0]

    @pl.when(t == 0)
    def _():
        buf_ref[0:HALO, :] = jnp.zeros((HALO, D_RNN), F32)
        h_ref[...] = jnp.zeros_like(h_ref)

    buf_ref[HALO:HALO + tt, :] = xr_ref[...]
    xc = cb_ref[...] + cw_ref[RNN_CONV - 1:RNN_CONV, :] * xr_ref[...]
    for j in range(1, RNN_CONV):
        xc = xc + cw_ref[RNN_CONV - 1 - j:RNN_CONV - j, :] * buf_ref[HALO - j:HALO - j + tt, :]
    a, i, mult = _rglru_gates(xc, wa_ref, ba_ref, wx_ref, bx_ref, lam_ref)
    row = lax.broadcasted_iota(jnp.int32, (tt, 1), 0)
    mult = jnp.where(row + t * tt == 0, 1.0, mult)
    b = xc * i * mult
    s = 1
    while s < tt:
        b = a * _shift_rows(b, s, 0.0) + b
        a = a * _shift_rows(a, s, 1.0)
        s *= 2
    h = a * h_ref[0:1, :] + b
    h_ref[0:1, :] = h[tt - 1:tt, :]
    hl_ref[0] = h[tt - 1:tt, :]
    y_ref[...] = (h * _gelu(gr_ref[...])).astype(BF)
    buf_ref[0:HALO, :] = buf_ref[tt:tt + HALO, :]


def _block_diag(w):
    n, c, d = w.shape
    eye = jnp.eye(n, dtype=w.dtype)
    return (w[:, :, None, :] * eye[:, None, :, None]).reshape(n * c, n * d)


def _rglru_prompt(xr, gr, pa, batch, t):
    tt = min(SCAN_TILE, t)
    nt = t // tt
    vec = lambda a: a.reshape(1, D_RNN)
    y, hl = pl.pallas_call(
        _rglru_prompt_kernel,
        out_shape=[jax.ShapeDtypeStruct((batch * t, D_RNN), BF), jax.ShapeDtypeStruct((batch, 1, D_RNN), F32)],
        grid=(batch, nt),
        in_specs=[pl.BlockSpec((tt, D_RNN), lambda b, i: (b * nt + i, 0)),
                  pl.BlockSpec((tt, D_RNN), lambda b, i: (b * nt + i, 0)),
                  _full((RNN_CONV, D_RNN)), _full((1, D_RNN)), _full((D_RNN, D_RNN)), _full((1, D_RNN)),
                  _full((D_RNN, D_RNN)), _full((1, D_RNN)), _full((1, D_RNN))],
        out_specs=[pl.BlockSpec((tt, D_RNN), lambda b, i: (b * nt + i, 0)),
                   pl.BlockSpec((1, 1, D_RNN), lambda b, i: (b, 0, 0))],
        scratch_shapes=[pltpu.VMEM((HALO + tt, D_RNN), F32), pltpu.VMEM((8, D_RNN), F32)],
        compiler_params=_params(("arbitrary", "arbitrary")),
        name="rglru_prompt",
    )(xr, gr, pa['conv_w'], vec(pa['conv_b']), _block_diag(pa['wa']).astype(BF), vec(pa['ba']),
      _block_diag(pa['wx']).astype(BF), vec(pa['bx']), vec(pa['lam']))
    return y, hl.reshape(batch, D_RNN)


def _rglru_sample_kernel(xr_ref, gr_ref, prev_ref, h0_ref, cw_ref, cb_ref, wa_ref, ba_ref, wx_ref, bx_ref, lam_ref,
                         y_ref, h_ref):
    xc = cb_ref[...] + cw_ref[RNN_CONV - 1:RNN_CONV, :] * xr_ref[...]
    for k in range(RNN_CONV - 1):
        xc = xc + cw_ref[k:k + 1, :] * prev_ref[k]
    a, i, mult = _rglru_gates(xc, wa_ref, ba_ref, wx_ref, bx_ref, lam_ref)
    h = xc * i * mult + a * h0_ref[...]
    h_ref[...] = h
    y_ref[...] = (h * _gelu(gr_ref[...])).astype(BF)


def _rglru_sample(xr, gr, h0, prev_t, pa):
    b = xr.shape[0]
    vec = lambda a: a.reshape(1, D_RNN)
    return pl.pallas_call(
        _rglru_sample_kernel,
        out_shape=[jax.ShapeDtypeStruct((b, D_RNN), BF), jax.ShapeDtypeStruct((b, D_RNN), F32)],
        name="rglru_sample",
    )(xr, gr, prev_t, h0, pa['conv_w'], vec(pa['conv_b']),
      _block_diag(pa['wa']).astype(BF), vec(pa['ba']), _block_diag(pa['wx']).astype(BF), vec(pa['bx']),
      vec(pa['lam']))


def _mm2_ln_kernel(a_ref, b_ref, wa_ref, wb_ref, x_ref, g_ref, beta_ref, o_ref):
    y = _dot(a_ref[...], wa_ref[...]) + _dot(b_ref[...], wb_ref[...])
    o_ref[...] = _ln(DN_ALPHA * x_ref[...] + y, g_ref[...], beta_ref[...])


def _mm2_ln(a, b, wa, wb, x, g, beta):
    m = x.shape[0]
    tm = _row_tile(m)
    ka, kb = a.shape[1], b.shape[1]
    return pl.pallas_call(
        _mm2_ln_kernel,
        out_shape=jax.ShapeDtypeStruct((m, D_MODEL), F32),
        grid=(m // tm,),
        in_specs=[pl.BlockSpec((tm, ka), lambda i: (i, 0)), pl.BlockSpec((tm, kb), lambda i: (i, 0)),
                  _full((ka, D_MODEL)), _full((kb, D_MODEL)), pl.BlockSpec((tm, D_MODEL), lambda i: (i, 0)),
                  _full((1, D_MODEL)), _full((1, D_MODEL))],
        out_specs=pl.BlockSpec((tm, D_MODEL), lambda i: (i, 0)),
        compiler_params=_params(("parallel",)),
        name="mm2_ln",
    )(a, b, wa, wb, x, g.reshape(1, -1), beta.reshape(1, -1))


def _mm_kernel(x_ref, w_ref, o_ref, ob_ref):
    y = _dot(x_ref[...].astype(BF), w_ref[0])
    o_ref[0] = y
    ob_ref[0] = y.astype(BF)


def _memory_kv(mem, wkv):
    r = mem.shape[0]
    nl, _, n = wkv.shape
    return pl.pallas_call(
        _mm_kernel,
        out_shape=[jax.ShapeDtypeStruct((nl, r, n), F32), jax.ShapeDtypeStruct((nl, r, n), BF)],
        grid=(nl,),
        in_specs=[_full((r, D_MODEL)), pl.BlockSpec((1, D_MODEL, n), lambda l: (l, 0, 0))],
        out_specs=[pl.BlockSpec((1, r, n), lambda l: (l, 0, 0)), pl.BlockSpec((1, r, n), lambda l: (l, 0, 0))],
        compiler_params=_params(("parallel",)),
        name="memory_kv",
    )(mem, wkv.astype(BF))


def _xattn_prompt_kernel(x_ref, k_ref, v_ref, wq_ref, wo_ref, g_ref, beta_ref, o_ref):
    x = x_ref[...]
    q = _dot(x.astype(BF), wq_ref[...]).astype(BF)
    outs = []
    for h in range(X_HEADS):
        hs = slice(h * X_HEAD_DIM, (h + 1) * X_HEAD_DIM)
        s = _dot_nt(q[:, hs], k_ref[0, :, hs]) * (X_HEAD_DIM ** -0.5)
        m = jnp.max(s, axis=-1, keepdims=True)
        e = jnp.exp(s - m)
        p = e / jnp.sum(e, axis=-1, keepdims=True)
        outs.append(_dot(p.astype(BF), v_ref[0, :, hs]).astype(BF))
    y = _dot(jnp.concatenate(outs, axis=1), wo_ref[...])
    o_ref[...] = _ln(DN_ALPHA * x + y, g_ref[...], beta_ref[...])


def _xattn_prompt(x, kb, vb, wq, wo, g, beta, batch, t):
    tm = _row_tile(t)
    nt = t // tm
    return pl.pallas_call(
        _xattn_prompt_kernel,
        out_shape=jax.ShapeDtypeStruct((batch * t, D_MODEL), F32),
        grid=(batch, nt),
        in_specs=[pl.BlockSpec((tm, D_MODEL), lambda b, i: (b * nt + i, 0)),
                  pl.BlockSpec((1, MEM_LEN, X_W), lambda b, i: (b, 0, 0)),
                  pl.BlockSpec((1, MEM_LEN, X_W), lambda b, i: (b, 0, 0)),
                  _full((D_MODEL, X_W)), _full((X_W, D_MODEL)), _full((1, D_MODEL)), _full((1, D_MODEL))],
        out_specs=pl.BlockSpec((tm, D_MODEL), lambda b, i: (b * nt + i, 0)),
        compiler_params=_params(("parallel", "parallel")),
        name="xattn_prompt",
    )(x, kb, vb, wq, wo, g.reshape(1, -1), beta.reshape(1, -1))


XS_SEQS = 8


def _xattn_sample_kernel(x_ref, kv_ref, wq_ref, wo_ref, g_ref, beta_ref, o_ref):
    x = x_ref[...]
    xb = x.astype(BF)
    qh = [_dot(xb, wq_ref[:, h * X_HEAD_DIM:(h + 1) * X_HEAD_DIM]).astype(BF).astype(F32) for h in range(X_HEADS)]
    rows = []
    for s_i in range(XS_SEQS):
        q4 = jnp.concatenate([qh[h][s_i:s_i + 1, :] for h in range(X_HEADS)], axis=0)
        k = kv_ref[0, s_i, :, 0].astype(BF).astype(F32)
        v = kv_ref[0, s_i, :, 1].astype(BF).astype(F32)
        s = jnp.sum(k * q4[None], axis=-1, keepdims=True) * (X_HEAD_DIM ** -0.5)
        e = jnp.exp(s - jnp.max(s, axis=0, keepdims=True))
        p = (e / jnp.sum(e, axis=0, keepdims=True)).astype(BF).astype(F32)
        o4 = jnp.sum(p * v, axis=0)
        rows.append(jnp.concatenate([o4[h:h + 1, :] for h in range(X_HEADS)], axis=1))
    o = jnp.concatenate(rows, axis=0)
    y = _dot(o.astype(BF), wo_ref[...])
    o_ref[...] = _ln(DN_ALPHA * x + y, g_ref[...], beta_ref[...])


def _xattn_sample(x, cache, layer, wq, wo, g, beta):
    b = x.shape[0]
    return pl.pallas_call(
        _xattn_sample_kernel,
        out_shape=jax.ShapeDtypeStruct((b, D_MODEL), F32),
        grid=(b // XS_SEQS,),
        in_specs=[pl.BlockSpec((XS_SEQS, D_MODEL), lambda i: (i, 0)),
                  pl.BlockSpec((1, XS_SEQS, MEM_LEN, 2, X_HEADS, X_HEAD_DIM), lambda i: (layer, i, 0, 0, 0, 0)),
                  _full((D_MODEL, X_W)), _full((X_W, D_MODEL)), _full((1, D_MODEL)), _full((1, D_MODEL))],
        out_specs=pl.BlockSpec((XS_SEQS, D_MODEL), lambda i: (i, 0)),
        compiler_params=_params(("parallel",)),
        name="xattn_sample",
    )(x, cache, wq, wo, g.reshape(1, -1), beta.reshape(1, -1))


def _swiglu_rows(xb, wg_ref, wu_ref, wd_ref, lead):
    acc = None
    for c in range(D_FF // FF_CHUNK):
        cs = slice(c * FF_CHUNK, (c + 1) * FF_CHUNK)
        hg = _dot(xb, wg_ref[lead + (slice(None), cs)])
        hu = _dot(xb, wu_ref[lead + (slice(None), cs)])
        part = _dot((_silu(hg) * hu).astype(BF), wd_ref[lead + (cs, slice(None))])
        acc = part if acc is None else acc + part
    return acc


def _swiglu_ln_kernel(x_ref, wg_ref, wu_ref, wd_ref, g_ref, beta_ref, o_ref):
    x = x_ref[...]
    y = _swiglu_rows(x.astype(BF), wg_ref, wu_ref, wd_ref, ())
    o_ref[...] = _ln(DN_ALPHA * x + y, g_ref[...], beta_ref[...])


def _swiglu_ln(x, wg, wu, wd, g, beta):
    m = x.shape[0]
    tm = _row_tile(m)
    return pl.pallas_call(
        _swiglu_ln_kernel,
        out_shape=jax.ShapeDtypeStruct((m, D_MODEL), F32),
        grid=(m // tm,),
        in_specs=[pl.BlockSpec((tm, D_MODEL), lambda i: (i, 0)), _full((D_MODEL, D_FF)), _full((D_MODEL, D_FF)),
                  _full((D_FF, D_MODEL)), _full((1, D_MODEL)), _full((1, D_MODEL))],
        out_specs=pl.BlockSpec((tm, D_MODEL), lambda i: (i, 0)),
        compiler_params=_params(("parallel",)),
        name="swiglu_ln",
    )(x, wg, wu, wd, g.reshape(1, -1), beta.reshape(1, -1))


def _router_kernel(x_ref, r_ref, idx_ref, w_ref):
    logits = jnp.dot(x_ref[...], r_ref[...], preferred_element_type=F32, precision=lax.Precision.HIGHEST)
    lane = lax.broadcasted_iota(jnp.int32, logits.shape, 1)
    lane_f = lane.astype(F32)
    logits = jnp.where(lane < N_EXPERTS, logits, -jnp.inf)
    m1 = jnp.max(logits, axis=-1, keepdims=True)
    i1 = jnp.min(jnp.where(logits == m1, lane_f, 1.0e9), axis=-1, keepdims=True)
    rest = jnp.where(lane_f == i1, -jnp.inf, logits)
    m2 = jnp.max(rest, axis=-1, keepdims=True)
    i2 = jnp.min(jnp.where(rest == m2, lane_f, 1.0e9), axis=-1, keepdims=True)
    e2 = jnp.exp(m2 - m1)
    den = 1.0 + e2
    idx_ref[...] = jnp.where(lane == 0, i1, i2).astype(jnp.int32)
    w_ref[...] = jnp.where(lane == 0, 1.0 / den, e2 / den)


def _router(x, router):
    m = x.shape[0]
    tm = _row_tile(m)
    rp = jnp.pad(router, ((0, 0), (0, LANES - N_EXPERTS)))
    idx, w = pl.pallas_call(
        _router_kernel,
        out_shape=[jax.ShapeDtypeStruct((m, LANES), jnp.int32), jax.ShapeDtypeStruct((m, LANES), F32)],
        grid=(m // tm,),
        in_specs=[pl.BlockSpec((tm, D_MODEL), lambda i: (i, 0)), _full((D_MODEL, LANES))],
        out_specs=[pl.BlockSpec((tm, LANES), lambda i: (i, 0)), pl.BlockSpec((tm, LANES), lambda i: (i, 0))],
        compiler_params=_params(("parallel",)),
        name="router",
    )(x, rp)
    return idx[:, :TOP_K], w


def _moe_kernel(te_ref, nu_ref, x_ref, wg_ref, wu_ref, wd_ref, o_ref):
    i = pl.program_id(0)

    @pl.when(i < nu_ref[0])
    def _():
        o_ref[...] = _swiglu_rows(x_ref[...], wg_ref, wu_ref, wd_ref, (0,))

    @pl.when(i >= nu_ref[0])
    def _():
        o_ref[...] = jnp.zeros_like(o_ref)


def _moe_ffn(tile_expert, n_used, xs, wg, wu, wd):
    p = xs.shape[0]
    return pl.pallas_call(
        _moe_kernel,
        out_shape=jax.ShapeDtypeStruct((p, D_MODEL), F32),
        grid_spec=pltpu.PrefetchScalarGridSpec(
            num_scalar_prefetch=2,
            grid=(p // MOE_TILE,),
            in_specs=[pl.BlockSpec((MOE_TILE, D_MODEL), lambda i, te, nu: (i, 0)),
                      pl.BlockSpec((1, D_MODEL, D_FF), lambda i, te, nu: (te[i], 0, 0)),
                      pl.BlockSpec((1, D_MODEL, D_FF), lambda i, te, nu: (te[i], 0, 0)),
                      pl.BlockSpec((1, D_FF, D_MODEL), lambda i, te, nu: (te[i], 0, 0))],
            out_specs=pl.BlockSpec((MOE_TILE, D_MODEL), lambda i, te, nu: (i, 0))),
        compiler_params=_params(("arbitrary",)),
        name="moe_ffn",
    )(tile_expert, n_used, xs, wg, wu, wd)


def _add2_ln_kernel(x_ref, y0_ref, y1_ref, w_ref, g_ref, beta_ref, o_ref):
    y = w_ref[:, 0:1] * y0_ref[...] + w_ref[:, 1:2] * y1_ref[...]
    o_ref[...] = _ln(DN_ALPHA * x_ref[...] + y, g_ref[...], beta_ref[...])


def _add2_ln(x, y2, row0, w, g, beta):
    m = x.shape[0]
    tm = _row_tile(m)
    nt = m // tm
    b0 = row0 // tm
    return pl.pallas_call(
        _add2_ln_kernel,
        out_shape=jax.ShapeDtypeStruct((m, D_MODEL), F32),
        grid=(nt,),
        in_specs=[pl.BlockSpec((tm, D_MODEL), lambda i: (i, 0)), pl.BlockSpec((tm, D_MODEL), lambda i: (i + b0, 0)),
                  pl.BlockSpec((tm, D_MODEL), lambda i: (i + b0 + nt, 0)), pl.BlockSpec((tm, LANES), lambda i: (i, 0)),
                  _full((1, D_MODEL)), _full((1, D_MODEL))],
        out_specs=pl.BlockSpec((tm, D_MODEL), lambda i: (i, 0)),
        compiler_params=_params(("parallel",)),
        name="add2_ln",
    )(x, y2, y2, w, g.reshape(1, -1), beta.reshape(1, -1))


def _moe_layer(xp, xs, router, wg, wu, wd, g, beta):
    mp, ms = xp.shape[0], xs.shape[0]
    m = mp + ms
    ip, wp = _router(xp, router)
    is_, ws = _router(xs, router)
    e_flat = jnp.concatenate([ip.T.reshape(-1), is_.T.reshape(-1)])
    tok = jnp.concatenate([jnp.tile(jnp.arange(mp, dtype=jnp.int32), TOP_K),
                           mp + jnp.tile(jnp.arange(ms, dtype=jnp.int32), TOP_K)])
    onehot = (e_flat[:, None] == jnp.arange(N_EXPERTS, dtype=jnp.int32)[None, :]).astype(jnp.int32)
    cum = jnp.cumsum(onehot, axis=0)
    rank = jnp.sum(onehot * (cum - 1), axis=1)
    counts = cum[-1]
    padded = ((counts + MOE_TILE - 1) // MOE_TILE) * MOE_TILE
    ends = jnp.cumsum(padded)
    starts = ends - padded
    pos = starts[e_flat] + rank
    n_tiles = (TOP_K * m + N_EXPERTS * (MOE_TILE - 1)) // MOE_TILE + 1
    p_rows = n_tiles * MOE_TILE
    tile_start = jnp.arange(n_tiles, dtype=jnp.int32) * MOE_TILE
    tile_expert = jnp.minimum(jnp.sum((tile_start[:, None] >= ends[None, :]).astype(jnp.int32), axis=1),
                              N_EXPERTS - 1).astype(jnp.int32)
    n_used = (ends[-1] // MOE_TILE).astype(jnp.int32).reshape(1)
    src = jnp.zeros((p_rows,), jnp.int32).at[pos].set(tok)
    x_sorted = jnp.concatenate([xp.astype(BF), xs.astype(BF)], axis=0)[src]
    y2 = _moe_ffn(tile_expert, n_used, x_sorted, wg, wu, wd)[pos]
    return _add2_ln(xp, y2, 0, wp, g, beta), _add2_ln(xs, y2, TOP_K * mp, ws, g, beta)


def _glu_kernel(x_ref, w_ref, b_ref, o_ref):
    xb = x_ref[...].astype(BF)
    a = _dot(xb, w_ref[:, 0:D_MODEL]) + b_ref[:, 0:D_MODEL]
    gate = _dot(xb, w_ref[:, D_MODEL:2 * D_MODEL]) + b_ref[:, D_MODEL:2 * D_MODEL]
    o_ref[...] = a * jax.nn.sigmoid(gate)


def _glu(x, w, b):
    m = x.shape[0]
    tm = _row_tile(m)
    return pl.pallas_call(
        _glu_kernel,
        out_shape=jax.ShapeDtypeStruct((m, D_MODEL), F32),
        grid=(m // tm,),
        in_specs=[pl.BlockSpec((tm, D_MODEL), lambda i: (i, 0)), _full((D_MODEL, 2 * D_MODEL)),
                  _full((1, 2 * D_MODEL))],
        out_specs=pl.BlockSpec((tm, D_MODEL), lambda i: (i, 0)),
        compiler_params=_params(("parallel",)),
        name="glu",
    )(x, w, b.reshape(1, -1))


CONV_ROWS = 64


def _conv_tail(c, x, cg_ref, cb_ref, wp_ref, bp_ref, g_ref, beta_ref):
    c = _ln(c, cg_ref[...], cb_ref[...])
    y = _dot(_silu(c).astype(BF), wp_ref[...]) + bp_ref[...]
    return _ln(DN_ALPHA * x + y, g_ref[...], beta_ref[...])


def _conv_prompt_kernel(gl_ref, x_ref, dw_ref, db_ref, cg_ref, cb_ref, wp_ref, bp_ref, g_ref, beta_ref,
                        o_ref, buf_ref, sh_ref):
    t = pl.program_id(1)
    tt = gl_ref.shape[0]
    n = HALO + tt

    @pl.when(t == 0)
    def _():
        buf_ref[0:HALO, :] = jnp.zeros((HALO, D_MODEL), F32)

    buf_ref[HALO:HALO + tt, :] = gl_ref[...]
    full = buf_ref[...]
    for r in range(1, SUBLANES):
        sh_ref[r - 1] = pltpu.roll(full, n - r, 0)
    off = HALO - (CONF_K - 1)
    for r0 in range(0, tt, CONV_ROWS):
        c = db_ref[...]
        for k in range(CONF_K):
            o = r0 + off + k
            r = o % SUBLANES
            a = o - r
            win = buf_ref[a:a + CONV_ROWS, :] if r == 0 else sh_ref[r - 1, a:a + CONV_ROWS, :]
            c = c + dw_ref[k:k + 1, :] * win
        o_ref[r0:r0 + CONV_ROWS, :] = _conv_tail(c, x_ref[r0:r0 + CONV_ROWS, :], cg_ref, cb_ref, wp_ref, bp_ref,
                                                 g_ref, beta_ref)
    buf_ref[0:HALO, :] = buf_ref[tt:tt + HALO, :]


def _conv_prompt(gl, x, cp, g, beta, batch, t):
    tt = min(SCAN_TILE, t)
    nt = t // tt
    vec = lambda a: a.reshape(1, -1)
    return pl.pallas_call(
        _conv_prompt_kernel,
        out_shape=jax.ShapeDtypeStruct((batch * t, D_MODEL), F32),
        grid=(batch, nt),
        in_specs=[pl.BlockSpec((tt, D_MODEL), lambda b, i: (b * nt + i, 0)),
                  pl.BlockSpec((tt, D_MODEL), lambda b, i: (b * nt + i, 0)),
                  _full((CONF_K, D_MODEL)), _full((1, D_MODEL)), _full((1, D_MODEL)), _full((1, D_MODEL)),
                  _full((D_MODEL, D_MODEL)), _full((1, D_MODEL)), _full((1, D_MODEL)), _full((1, D_MODEL))],
        out_specs=pl.BlockSpec((tt, D_MODEL), lambda b, i: (b * nt + i, 0)),
        scratch_shapes=[pltpu.VMEM((HALO + tt, D_MODEL), F32), pltpu.VMEM((SUBLANES - 1, HALO + tt, D_MODEL), F32)],
        compiler_params=_params(("arbitrary", "arbitrary")),
        name="conv_prompt",
    )(gl, x, cp['dw_w'], vec(cp['dw_b']), vec(cp['ln_g']), vec(cp['ln_b']), cp['w_pw'], vec(cp['b_pw']),
      vec(g), vec(beta))


CS_SEQS = 32


def _conv_sample_kernel(gl_ref, st_ref, x_ref, dw_ref, db_ref, cg_ref, cb_ref, wp_ref, bp_ref, g_ref, beta_ref,
                        o_ref):
    c = db_ref[...] + dw_ref[CONF_K - 1:CONF_K, :] * gl_ref[...]
    for k in range(CONF_K - 1):
        c = c + dw_ref[k:k + 1, :] * st_ref[0, k]
    o_ref[...] = _conv_tail(c, x_ref[...], cg_ref, cb_ref, wp_ref, bp_ref, g_ref, beta_ref)


def _conv_sample(gl, state_t, layer, x, cp, g, beta):
    b = x.shape[0]
    vec = lambda a: a.reshape(1, -1)
    return pl.pallas_call(
        _conv_sample_kernel,
        out_shape=jax.ShapeDtypeStruct((b, D_MODEL), F32),
        grid=(b // CS_SEQS,),
        in_specs=[pl.BlockSpec((CS_SEQS, D_MODEL), lambda i: (i, 0)),
                  pl.BlockSpec((1, CONF_K - 1, CS_SEQS, D_MODEL), lambda i: (layer, 0, i, 0)),
                  pl.BlockSpec((CS_SEQS, D_MODEL), lambda i: (i, 0)),
                  _full((CONF_K, D_MODEL)), _full((1, D_MODEL)), _full((1, D_MODEL)), _full((1, D_MODEL)),
                  _full((D_MODEL, D_MODEL)), _full((1, D_MODEL)), _full((1, D_MODEL)), _full((1, D_MODEL))],
        out_specs=pl.BlockSpec((CS_SEQS, D_MODEL), lambda i: (i, 0)),
        compiler_params=_params(("parallel",)),
        name="conv_sample",
    )(gl, state_t, x, cp['dw_w'], vec(cp['dw_b']), vec(cp['ln_g']), vec(cp['ln_b']), cp['w_pw'], vec(cp['b_pw']),
      vec(g), vec(beta))


def _flatten_blocks(kt2, lead):
    n = kt2.shape[-1] // BLOCK
    nl = len(lead)
    x = kt2.reshape(*lead, 2, NSA_KV_HEADS, HEAD_DIM, n, BLOCK)
    perm = (nl,) + tuple(range(nl)) + (nl + 3, nl + 1, nl + 2, nl + 4)
    return x.transpose(perm).reshape(2, -1, HEAD_DIM * BLOCK)


def _mixer_a_prompt(x, w_in, wo_nsa, wo_rnn, pa, g, beta, batch, t):
    nb = t // BLOCK
    q, xr, gr, gates, ka, kvt, vt = _proj_a(x, w_in, batch, t)
    cmp = _compress(_flatten_blocks(kvt[:, 0:256], (batch,)), pa['cmp_pos'], pa['cmp_w1'], pa['cmp_w2'])
    ckp = _cmp_keys(cmp[0].reshape(batch, nb, NSA_KV_HEADS, HEAD_DIM), nb)
    cvt = cmp[1].reshape(batch, nb, LANES).transpose(0, 2, 1).astype(BF)
    o_nsa = _nsa_prompt(q, gates, ckp, cvt, ka, vt, batch, t)
    o_rnn, h_last = _rglru_prompt(xr, gr, pa, batch, t)
    x_new = _mm2_ln(o_nsa, o_rnn, wo_nsa, wo_rnn, x, g, beta)
    n_keep = min(WINDOW, t)
    kv_out = kvt[:, 0:512].reshape(batch, 4, NSA_KV_HEADS, HEAD_DIM, t).transpose(0, 4, 1, 2, 3)
    win_out = kvt[:, 512:768, t - n_keep:].reshape(batch, 2, NSA_KV_HEADS, HEAD_DIM, n_keep).transpose(0, 4, 1, 2, 3)
    conv_out = xr.reshape(batch, t, D_RNN)[:, t - (RNN_CONV - 1):]
    return x_new, kv_out, win_out, h_last, conv_out


def _mixer_a_sample(x, pool_all, page_ids, win_all, layer, h0, prev_t, w_in, wo_nsa, wo_rnn, pa, g, beta):
    b = x.shape[0]
    n_pages, page = page_ids.shape[1], pool_all.shape[2]
    nb_past = n_pages * page // BLOCK
    q, xr, gr, gates, _, kvt, _, kv_row = _proj_a(x, w_in, 1, b, row_kv=True)
    pool4 = pool_all.reshape(pool_all.shape[0], 2, 2 * LANES, page)
    pg_cmp = pool4[page_ids, 0]
    pages = pool4[page_ids, 1]
    cmp = _compress_pages(pg_cmp.reshape(b * n_pages, 2 * LANES, page), pa['cmp_pos'], pa['cmp_w1'], pa['cmp_w2'])
    cmp = cmp.reshape(2, b, nb_past, NSA_KV_HEADS, HEAD_DIM)
    ckp = _cmp_keys(cmp[0], LANES)
    cv = jnp.pad(cmp[1].reshape(b, nb_past, LANES), ((0, 0), (0, LANES - nb_past), (0, 0))).astype(BF)
    q16 = jnp.pad(q.astype(F32).reshape(b, NSA_HEADS, LANES), ((0, 0), (0, SAMPLE_ROWS - NSA_HEADS), (0, 0)))
    g3 = jnp.pad(gates[:, :3 * NSA_HEADS].reshape(b, NSA_HEADS, 3), ((0, 0), (0, SAMPLE_ROWS - NSA_HEADS), (0, 0)))
    o16 = _nsa_sample(q16, g3, ckp, cv, pages, kv_row.reshape(b, 1, KV_W), win_all, layer)
    o_nsa = o16[:, :NSA_HEADS].reshape(b, NSA_HEADS * HEAD_DIM).astype(BF)
    o_rnn, h_new = _rglru_sample(xr, gr, h0, prev_t, pa)
    x_new = _mm2_ln(o_nsa, o_rnn, wo_nsa, wo_rnn, x, g, beta)
    kv_out = kvt[0, 0:512].reshape(4, NSA_KV_HEADS, HEAD_DIM, b).transpose(3, 0, 1, 2)[:, None]
    win_col = kvt[0, 512:768].T
    conv_out = jnp.concatenate([prev_t[1:], xr[None]], axis=0).transpose(1, 0, 2)
    return x_new, kv_out, win_col, h_new, conv_out


def kernel(x_prompt, x_sample, cache_nsa_kv, cache_nsa_win, state_rglru_h, state_rglru_conv, state_conv,
           cache_mem_kv, page_table, mem_prompt, ln_g, ln_b, a_w_in, a_cmp_pos, a_cmp_w1, a_cmp_w2,
           a_conv_w, a_conv_b, a_gate_a_w, a_gate_a_b, a_gate_x_w, a_gate_x_b, a_lambda, a_w_out,
           c_w_glu, c_b_glu, c_dw_w, c_dw_b, c_ln_g, c_ln_b, c_w_pw, c_b_pw, x_wq, x_wkv, x_wo,
           f_w_gu, f_w_down, m_router, m_w_gu, m_w_down):
    batch, t, _ = x_prompt.shape
    bs = x_sample.shape[0]
    xp = x_prompt.reshape(batch * t, D_MODEL)
    xs = x_sample.reshape(bs, D_MODEL)
    mkv_f, mkv_b = _memory_kv(mem_prompt.reshape(batch * MEM_LEN, D_MODEL), x_wkv.reshape(DEPTH, D_MODEL, 2 * X_W))
    n_layers_a, n_pool, page = cache_nsa_kv.shape[:3]
    pool_all = cache_nsa_kv.transpose(0, 1, 3, 4, 5, 2).reshape(n_layers_a * n_pool, 4 * LANES, page)
    n_buf = cache_nsa_win.shape[2]
    win_t = cache_nsa_win.transpose(0, 1, 3, 4, 5, 2).reshape(n_layers_a, bs, 2 * LANES, n_buf)
    rconv_t = state_rglru_conv.transpose(0, 2, 1, 3)
    sconv_t = state_conv.transpose(0, 2, 1, 3)
    pk, pw, ph, pcv, pc = [], [], [], [], []
    sk, sw, sh, scv, sc = [], [], [], [], []
    for l in range(DEPTH):
        if l % 2 == 0:
            i = l // 2
            pa = {'cmp_pos': a_cmp_pos[i], 'cmp_w1': a_cmp_w1[i], 'cmp_w2': a_cmp_w2[i],
                  'conv_w': a_conv_w[i], 'conv_b': a_conv_b[i], 'wa': a_gate_a_w[i], 'ba': a_gate_a_b[i],
                  'wx': a_gate_x_w[i], 'bx': a_gate_x_b[i], 'lam': a_lambda[i]}
            w_in = _prep_w_in(a_w_in[i])
            wo_pad, wo_cmp, wo_rnn = _prep_w_out(a_w_out[i])
            xp, kv_p, win_p, h_p, cb_p = _mixer_a_prompt(xp, w_in, wo_pad, wo_rnn, pa, ln_g[l, 0], ln_b[l, 0],
                                                         batch, t)
            xs, kv_s, win_s, h_s, cb_s = _mixer_a_sample(xs, pool_all, page_table + i * n_pool, win_t, i,
                                                         state_rglru_h[i], rconv_t[i], w_in, wo_cmp, wo_rnn, pa,
                                                         ln_g[l, 0], ln_b[l, 0])
            pk.append(kv_p); pw.append(win_p); ph.append(h_p); pcv.append(cb_p)
            sk.append(kv_s); sw.append(win_s); sh.append(h_s); scv.append(cb_s)
        else:
            j = l // 2
            cp = {'dw_w': c_dw_w[j], 'dw_b': c_dw_b[j], 'ln_g': c_ln_g[j], 'ln_b': c_ln_b[j],
                  'w_pw': c_w_pw[j].astype(BF), 'b_pw': c_b_pw[j]}
            w_glu = c_w_glu[j].astype(BF)
            gl_p = _glu(xp, w_glu, c_b_glu[j])
            gl_s = _glu(xs, w_glu, c_b_glu[j])
            xp = _conv_prompt(gl_p, xp, cp, ln_g[l, 0], ln_b[l, 0], batch, t)
            xs = _conv_sample(gl_s, sconv_t, j, xs, cp, ln_g[l, 0], ln_b[l, 0])
            pc.append(gl_p.reshape(batch, t, D_MODEL)[:, t - (CONF_K - 1):])
            sc.append(jnp.concatenate([sconv_t[j, 1:], gl_s[None]], axis=0).transpose(1, 0, 2))
        wq = x_wq[l].astype(BF)
        wo = x_wo[l].astype(BF)
        kb = mkv_b[l].reshape(batch, MEM_LEN, 2 * X_W)
        xp = _xattn_prompt(xp, kb[:, :, :X_W], kb[:, :, X_W:], wq, wo, ln_g[l, 1], ln_b[l, 1], batch, t)
        xs = _xattn_sample(xs, cache_mem_kv, l, wq, wo, ln_g[l, 1], ln_b[l, 1])
        if l % 2 == 0:
            i = l // 2
            wg = f_w_gu[i][:, 0].astype(BF)
            wu = f_w_gu[i][:, 1].astype(BF)
            wd = f_w_down[i].astype(BF)
            xp = _swiglu_ln(xp, wg, wu, wd, ln_g[l, 2], ln_b[l, 2])
            xs = _swiglu_ln(xs, wg, wu, wd, ln_g[l, 2], ln_b[l, 2])
        else:
            j = l // 2
            wg = m_w_gu[j][:, :, 0].astype(BF)
            wu = m_w_gu[j][:, :, 1].astype(BF)
            wd = m_w_down[j].astype(BF)
            xp, xs = _moe_layer(xp, xs, m_router[j], wg, wu, wd, ln_g[l, 2], ln_b[l, 2])
    p_mem = mkv_f.reshape(DEPTH, batch, MEM_LEN, 2, X_HEADS, X_HEAD_DIM)
    s_win = jnp.concatenate([win_t[:, :, :, 1:], jnp.stack(sw)[:, :, :, None]], axis=3)
    s_win = s_win.reshape(n_layers_a, bs, 2, NSA_KV_HEADS, HEAD_DIM, n_buf).transpose(0, 1, 5, 2, 3, 4)
    return (xp.reshape(batch, t, D_MODEL), xs.reshape(bs, 1, D_MODEL), jnp.stack(pk), jnp.stack(pw),
            jnp.stack(ph), jnp.stack(pcv), jnp.stack(pc), p_mem,
            jnp.stack(sk), s_win, jnp.stack(sh), jnp.stack(scv), jnp.stack(sc))
```

```python
import functools

import jax
import jax.numpy as jnp
from jax import lax
from jax.experimental import pallas as pl
from jax.experimental.pallas import tpu as pltpu

F32 = jnp.float32
BF = jnp.bfloat16

D_MODEL = 1024
NSA_HEADS = 8
NSA_KV_HEADS = 2
NSA_GROUP = NSA_HEADS // NSA_KV_HEADS
HEAD_DIM = 64
BLOCK = 64
N_SEL = 8
N_LOCAL_BLOCKS = 2
WINDOW = 512
Q_BLOCK = 128
FORCE_SCORE = 1.0e4
D_RNN = 512
RNN_CONV = 4
RG_C = 8.0
CONF_K = 31
MEM_LEN = 256
X_HEADS = 4
X_HEAD_DIM = 128
X_W = X_HEADS * X_HEAD_DIM
D_FF = 2816
N_EXPERTS = 8
TOP_K = 2
LN_EPS = 1e-5
DEPTH = 4
DN_ALPHA = (2.0 * DEPTH) ** 0.25

LANES = 128
SUBLANES = 8
SEL_KEY_TILE = 512
WIN_KEYS = WINDOW + Q_BLOCK
FF_CHUNK = 256
MOE_TILE = 256
SCAN_TILE = 256
HALO = 32
VMEM_LIMIT = 56 * 1024 * 1024
NEG = -1.0e30
MASK_BIAS = -(2.0 ** 30)
ALIBI_SLOPES = tuple(2.0 ** (-8.0 * (i + 1) / NSA_HEADS) for i in range(NSA_HEADS))


def _dot(a, b):
    return jnp.dot(a, b, preferred_element_type=F32)


def _dot_nt(a, b):
    return lax.dot_general(a, b, (((1,), (1,)), ((), ())), preferred_element_type=F32)


def _ln(z, g, b):
    mu = jnp.mean(z, axis=-1, keepdims=True)
    zc = z - mu
    var = jnp.mean(zc * zc, axis=-1, keepdims=True)
    return zc * lax.rsqrt(var + LN_EPS) * g + b


def _gelu(x):
    return 0.5 * x * (1.0 + jnp.tanh(0.7978845608028654 * (x + 0.044715 * (x * x * x))))


def _silu(x):
    return x * jax.nn.sigmoid(x)


def _params(sem):
    return pltpu.CompilerParams(dimension_semantics=sem, vmem_limit_bytes=VMEM_LIMIT)


def _row_tile(m, pref=512):
    return pref if m % pref == 0 else m


def _full(shape):
    n = len(shape)
    return pl.BlockSpec(shape, lambda *_: (0,) * n)


Q_PAD_W = NSA_HEADS * LANES
KV_W = 6 * NSA_KV_HEADS * HEAD_DIM
KA_W = 6 * LANES
ROW_W = Q_PAD_W + 2 * D_RNN + LANES + KA_W
POS_HI = HEAD_DIM
POS_LO = HEAD_DIM + 1


def _proj_a_kernel(x_ref, wrow_ref, wt_ref, qf_ref, *refs, row_kv):
    if row_kv:
        wkv_ref, q_ref, xr_ref, gr_ref, gate_ref, ka_ref, kvt_ref, vt_ref, kvrow_ref = refs
    else:
        q_ref, xr_ref, gr_ref, gate_ref, ka_ref, kvt_ref, vt_ref = refs
    i = pl.program_id(1)
    tm = x_ref.shape[0]
    xb = x_ref[...].astype(BF)
    q = _dot(xb, wrow_ref[:, 0:Q_PAD_W])
    q_ref[...] = (q * (HEAD_DIM ** -0.5) + qf_ref[...]).astype(BF)
    o = Q_PAD_W
    xr_ref[...] = _dot(xb, wrow_ref[:, o:o + D_RNN])
    o += D_RNN
    gr_ref[...] = _dot(xb, wrow_ref[:, o:o + D_RNN])
    o += D_RNN
    gate_ref[...] = _dot(xb, wrow_ref[:, o:o + LANES])
    o += LANES
    pos = i * tm + lax.broadcasted_iota(jnp.int32, (tm, KA_W), 0)
    lane_all = lax.broadcasted_iota(jnp.int32, (tm, KA_W), 1)
    lane = lane_all & (LANES - 1)
    group = lane_all >> 7
    feat = jnp.where((group == 1) | (group == 3), jnp.where(lane == (pos >> 6), 1.0, 0.0),
                     jnp.where(lane == POS_HI, (pos >> 6).astype(F32),
                               jnp.where(lane == POS_LO, (pos & (BLOCK - 1)).astype(F32), 0.0)))
    ka_ref[...] = (_dot(xb, wrow_ref[:, o:o + KA_W]) + feat).astype(BF)
    kvt = _dot_nt(wt_ref[...], xb)
    kvt_ref[0] = kvt
    vt_ref[0, 0:LANES, :] = kvt[3 * LANES:4 * LANES].astype(BF)
    vt_ref[0, LANES:2 * LANES, :] = kvt[5 * LANES:6 * LANES].astype(BF)
    if row_kv:
        kvrow_ref[...] = _dot(xb, wkv_ref[...])


def _proj_a(x, w, batch, t, row_kv=False):
    m = batch * t
    tm = _row_tile(t)
    nt = t // tm
    row = lambda w_: pl.BlockSpec((tm, w_), lambda b, i: (b * nt + i, 0))
    fm = lambda f: pl.BlockSpec((1, f, tm), lambda b, i: (b, 0, i))
    outs = [((m, Q_PAD_W), BF, row(Q_PAD_W)), ((m, D_RNN), F32, row(D_RNN)), ((m, D_RNN), F32, row(D_RNN)),
            ((m, LANES), F32, row(LANES)), ((m, KA_W), BF, row(KA_W)), ((batch, KV_W, t), F32, fm(KV_W)),
            ((batch, 2 * LANES, t), BF, fm(2 * LANES))]
    ins = [x, w['row'], w['t'], w['qf']]
    in_specs = [row(D_MODEL), _full((D_MODEL, ROW_W)), _full((KV_W, D_MODEL)), _full((1, Q_PAD_W))]
    if row_kv:
        ins.append(w['kv'])
        in_specs.append(_full((D_MODEL, KV_W)))
        outs.append(((m, KV_W), F32, row(KV_W)))
    return pl.pallas_call(
        functools.partial(_proj_a_kernel, row_kv=row_kv),
        out_shape=[jax.ShapeDtypeStruct(s, dt) for s, dt, _ in outs],
        grid=(batch, nt),
        in_specs=in_specs,
        out_specs=[sp for _, _, sp in outs],
        compiler_params=_params(("parallel", "parallel")),
        name="proj_a",
    )(*ins)


def _prep_w_in(w_in):
    wq = w_in[:, :512].reshape(D_MODEL, NSA_HEADS, HEAD_DIM)
    wq_p = jnp.concatenate([wq, jnp.zeros_like(wq)], axis=2).reshape(D_MODEL, Q_PAD_W)
    wkv = w_in[:, 512:1280]
    wg = jnp.pad(w_in[:, 1280:1304], ((0, 0), (0, LANES - 3 * NSA_HEADS)))
    wxr = w_in[:, 1304:1816]
    wgr = w_in[:, 1816:2328]
    lane = jnp.arange(Q_PAD_W) % LANES
    slope = jnp.repeat(jnp.asarray(ALIBI_SLOPES, F32), LANES)
    qf = jnp.where(lane == POS_HI, BLOCK * slope, jnp.where(lane == POS_LO, slope, 0.0)).reshape(1, Q_PAD_W)
    wkv6 = wkv.reshape(D_MODEL, 6, NSA_KV_HEADS, HEAD_DIM)
    wk = jnp.stack([wkv6[:, 2], wkv6[:, 4]], axis=1)
    wk = jnp.concatenate([wk, jnp.zeros_like(wk)], axis=3)
    z = jnp.zeros((D_MODEL, LANES), F32)
    wka = jnp.concatenate([wk[:, 0, 0], z, wk[:, 0, 1], z, wk[:, 1, 0], wk[:, 1, 1]], axis=1)
    return {'row': jnp.concatenate([wq_p, wxr, wgr, wg, wka], axis=1).astype(BF), 't': wkv.T.astype(BF),
            'kv': wkv.astype(BF), 'qf': qf}


def _prep_w_out(w_out):
    wn = w_out[:512].reshape(NSA_HEADS, HEAD_DIM, D_MODEL)
    z = jnp.zeros_like(wn)
    lo = jnp.concatenate([wn, z], axis=1)
    hi = jnp.concatenate([z, wn], axis=1)
    wn_p = jnp.concatenate([lo[:NSA_GROUP], hi[NSA_GROUP:]], axis=0).reshape(Q_PAD_W, D_MODEL)
    return wn_p.astype(BF), w_out[:512].astype(BF), w_out[512:].astype(BF)


def _compress_kernel(x_ref, pos_ref, w1_ref, w2_ref, o_ref):
    xb = (x_ref[0] + pos_ref[0]).astype(BF)
    h = _gelu(_dot(xb, w1_ref[0]))
    o_ref[0] = _dot(h.astype(BF), w2_ref[0])


def _compress(xblk, pos, w1, w2):
    r = xblk.shape[1]
    tr = _row_tile(r, 256)
    kdim = BLOCK * HEAD_DIM
    pos_t = pos.transpose(0, 2, 1).reshape(2, 1, kdim)
    w1_t = w1.reshape(2, BLOCK, HEAD_DIM, w1.shape[-1]).transpose(0, 2, 1, 3).reshape(2, kdim, w1.shape[-1])
    return pl.pallas_call(
        _compress_kernel,
        out_shape=jax.ShapeDtypeStruct((2, r, HEAD_DIM), F32),
        grid=(2, r // tr),
        in_specs=[pl.BlockSpec((1, tr, kdim), lambda c, i: (c, i, 0)),
                  pl.BlockSpec((1, 1, kdim), lambda c, i: (c, 0, 0)),
                  pl.BlockSpec((1, kdim, 128), lambda c, i: (c, 0, 0)),
                  pl.BlockSpec((1, 128, HEAD_DIM), lambda c, i: (c, 0, 0))],
        out_specs=pl.BlockSpec((1, tr, HEAD_DIM), lambda c, i: (c, i, 0)),
        compiler_params=_params(("parallel", "parallel")),
        name="compress",
    )(xblk, pos_t, w1_t.astype(BF), w2.astype(BF))


CP_SEQS = 8


def _compress_pages_kernel(pg_ref, pos_ref, w1_ref, w2_ref, o_ref, *, rows, stride):
    c = pl.program_id(1)
    outs = []
    for h in range(NSA_KV_HEADS):
        def body(d, acc, h=h):
            f = c * LANES + h * HEAD_DIM + d
            lhs = pg_ref[pl.ds(f, rows, stride=stride), :] + pos_ref[0, pl.ds(d, 1), :]
            return acc + _dot(lhs.astype(BF), w1_ref[0, d])

        acc = lax.fori_loop(0, HEAD_DIM, body, jnp.zeros((rows, 2 * 128), F32), unroll=8)
        hid = _gelu(acc).astype(BF)
        outs.append([_dot(hid[:, k * 128:(k + 1) * 128], w2_ref[0]) for k in range(2)])
    o_ref[0] = jnp.concatenate([outs[0][0], outs[1][0], outs[0][1], outs[1][1]], axis=1)


def _compress_pages(pages, pos, w1, w2):
    n, stride, page = pages.shape
    assert page == 2 * BLOCK and stride == 2 * LANES
    rows = min(n, CP_SEQS * 16)
    hidden = w1.shape[-1]
    pos_t = jnp.tile(pos.transpose(0, 2, 1), (1, 1, page // BLOCK))
    w1_t = w1.reshape(2, BLOCK, HEAD_DIM, hidden).transpose(0, 2, 1, 3)
    z = jnp.zeros_like(w1_t)
    w1_bd = jnp.concatenate([jnp.concatenate([w1_t, z], axis=3), jnp.concatenate([z, w1_t], axis=3)], axis=2)
    return pl.pallas_call(
        functools.partial(_compress_pages_kernel, rows=rows, stride=stride),
        out_shape=jax.ShapeDtypeStruct((2, n, 4 * HEAD_DIM), F32),
        grid=(n // rows, 2),
        in_specs=[pl.BlockSpec((rows * stride, page), lambda i, c: (i, 0)),
                  pl.BlockSpec((1, HEAD_DIM, page), lambda i, c: (c, 0, 0)),
                  pl.BlockSpec((1, HEAD_DIM, page, 2 * hidden), lambda i, c: (c, 0, 0, 0)),
                  pl.BlockSpec((1, hidden, HEAD_DIM), lambda i, c: (c, 0, 0))],
        out_specs=pl.BlockSpec((1, rows, 4 * HEAD_DIM), lambda i, c: (c, i, 0)),
        compiler_params=_params(("parallel", "arbitrary")),
        name="compress_pages",
    )(pages.reshape(n * stride, page), pos_t, w1_bd.astype(BF), w2.astype(BF))


def _cmp_keys(ck, nb_pad):
    batch, nb = ck.shape[0], ck.shape[1]
    ck = jnp.pad(ck.transpose(0, 2, 1, 3), ((0, 0), (0, 0), (0, nb_pad - nb), (0, 0)))
    lane = jnp.arange(HEAD_DIM)[None, :]
    n = jnp.arange(nb_pad, dtype=F32)[:, None]
    ext = jnp.where(lane == 0, n, jnp.where(lane == 1, float(BLOCK - 1), 0.0))
    ext = jnp.broadcast_to(ext[None, None], (batch, NSA_KV_HEADS, nb_pad, HEAD_DIM))
    return jnp.concatenate([ck, ext], axis=-1).astype(BF)


def _head_slopes(rows, rows_per_head, heads):
    r = lax.broadcasted_iota(jnp.int32, (rows, 1), 0)
    s = jnp.full((rows, 1), 0.0, F32)
    for g, h in enumerate(heads):
        s = jnp.where((r >= g * rows_per_head) & (r < (g + 1) * rows_per_head), ALIBI_SLOPES[h], s)
    return s


def _masked_softmax(s, valid, axis=-1):
    s = jnp.where(valid, s, NEG)
    m = jnp.max(s, axis=axis, keepdims=True)
    e = jnp.where(valid, jnp.exp(s - m), 0.0)
    return e / jnp.maximum(jnp.sum(e, axis=axis, keepdims=True), 1e-30)


def _select_blocks(score, blk_f, n_pick, axis=-1):
    sel = jnp.zeros(score.shape, F32)
    for _ in range(n_pick):
        m = jnp.max(score, axis=axis, keepdims=True)
        idx = jnp.min(jnp.where(score == m, blk_f, 1.0e9), axis=axis, keepdims=True)
        hit = blk_f == idx
        sel = jnp.where(hit & (m > -jnp.inf), 1.0, sel)
        score = jnp.where(hit, -jnp.inf, score)
    return sel


def _nsa_prompt_kernel(q_ref, gate_ref, ck_ref, cvt_ref, ka_ref, vt_ref, o_ref, *, nb):
    c = pl.program_id(1)
    start = c * Q_BLOCK
    rows = NSA_GROUP * Q_BLOCK
    sig_t = jax.nn.sigmoid(gate_ref[...]).T
    qpos_l = start + (lax.broadcasted_iota(jnp.int32, (1, rows), 1) & (Q_BLOCK - 1))
    blk = lax.broadcasted_iota(jnp.int32, (nb, 1), 0)
    blk_f = blk.astype(F32)
    cmp_valid = blk * BLOCK + (BLOCK - 1) <= qpos_l
    cur = qpos_l[:, 0:Q_BLOCK] >> 6
    forced = (blk == 0) | ((blk <= cur) & (blk > cur - N_LOCAL_BLOCKS))
    n_tiles = (start + Q_BLOCK + SEL_KEY_TILE - 1) // SEL_KEY_TILE
    krow = lax.broadcasted_iota(jnp.int32, (SEL_KEY_TILE, 1), 0)
    w0 = pl.multiple_of(jnp.maximum(start - WINDOW, 0), Q_BLOCK)
    dist = qpos_l[:, 0:Q_BLOCK] - (w0 + lax.broadcasted_iota(jnp.int32, (WIN_KEYS, 1), 0))
    wbias = jnp.where((dist >= 0) & (dist < WINDOW), 0.0, MASK_BIAS)
    wbias = jnp.concatenate([wbias] * NSA_GROUP, axis=1)

    kv_heads = range(NSA_KV_HEADS)
    qs, oc_t, qsm, first_far = [], [], [], []
    for kh in kv_heads:
        q4 = jnp.concatenate([q_ref[:, h * LANES:(h + 1) * LANES]
                              for h in range(kh * NSA_GROUP, (kh + 1) * NSA_GROUP)], axis=0)
        p_c = _masked_softmax(_dot_nt(ck_ref[0, kh], q4), cmp_valid, axis=0)
        oc_t.append(_dot(cvt_ref[0], p_c.astype(BF)))
        imp = p_c[:, 0:Q_BLOCK]
        for g in range(1, NSA_GROUP):
            imp = imp + p_c[:, g * Q_BLOCK:(g + 1) * Q_BLOCK]
        score = jnp.where(forced, FORCE_SCORE, imp)
        score = jnp.where(blk <= cur, score, -jnp.inf)
        sel = _select_blocks(score, blk_f, min(N_SEL, nb), axis=0)
        far = jnp.where((sel > 0.5) & (blk >= SEL_KEY_TILE // BLOCK), blk_f, 1.0e9)
        far = jnp.min(jnp.min(far, axis=1, keepdims=True), axis=0, keepdims=True).astype(jnp.int32)
        first_far.append(far[0, 0] // (SEL_KEY_TILE // BLOCK))
        qm = ((sel.T - 1.0) * (-MASK_BIAS)).astype(BF)
        if nb < LANES:
            qm = jnp.concatenate([qm, jnp.zeros((Q_BLOCK, LANES - nb), BF)], axis=1)
        qs.append(q4)
        qsm.append(jnp.concatenate([q4, jnp.concatenate([qm] * NSA_GROUP, axis=0)], axis=1))

    def scores(kh, k0):
        keys = ka_ref[pl.ds(k0, SEL_KEY_TILE), 2 * kh * LANES:2 * (kh + 1) * LANES]
        return _dot_nt(keys, qsm[kh])

    def update(s, carry, k0):
        m_i, l_i, acc = carry
        m_new = jnp.maximum(m_i, jnp.max(s, axis=0, keepdims=True))
        p = jnp.exp(s - m_new)
        alpha = jnp.exp(m_i - m_new)
        l_new = alpha * l_i + jnp.sum(p, axis=0, keepdims=True)
        acc = alpha * acc + _dot(vt_ref[0, 0:LANES, pl.ds(k0, SEL_KEY_TILE)], p.astype(BF))
        return m_new, l_new, acc

    def sel_step(j, carry):
        k0 = pl.multiple_of(j * SEL_KEY_TILE, SEL_KEY_TILE)
        s = [scores(kh, k0) for kh in kv_heads]
        return tuple(update(s[kh], carry[kh], k0) for kh in kv_heads)

    def solo_step(kh):
        def step(j, carry):
            k0 = pl.multiple_of(j * SEL_KEY_TILE, SEL_KEY_TILE)
            new = update(scores(kh, k0), carry[kh], k0)
            return tuple(new if h == kh else carry[h] for h in kv_heads)
        return step

    init = (jnp.full((1, rows), NEG, F32), jnp.zeros((1, rows), F32), jnp.zeros((LANES, rows), F32))
    last = n_tiles - 1
    first = jnp.minimum(1, last)
    a0, a1 = [jnp.minimum(jnp.maximum(f, first), last) for f in first_far]
    carry = lax.fori_loop(0, first, sel_step, (init,) * NSA_KV_HEADS)
    carry = lax.fori_loop(a0, a1, solo_step(0), carry)
    carry = lax.fori_loop(a1, a0, solo_step(1), carry)
    carry = lax.fori_loop(jnp.maximum(a0, a1), last, sel_step, carry)
    k0 = pl.multiple_of(last * SEL_KEY_TILE, SEL_KEY_TILE)
    causal = k0 + krow <= qpos_l
    s_d = [jnp.where(causal, scores(kh, k0), MASK_BIAS) for kh in kv_heads]
    s_w = [_dot_nt(ka_ref[pl.ds(w0, WIN_KEYS), (4 + kh) * LANES:(5 + kh) * LANES], qs[kh]) + wbias
           for kh in kv_heads]
    for kh in kv_heads:
        _, l_s, acc_s = update(s_d[kh], carry[kh], k0)
        os_t = acc_s / jnp.maximum(l_s, 1e-30)
        e_w = jnp.exp(s_w[kh] - jnp.max(s_w[kh], axis=0, keepdims=True))
        ow_t = (_dot(vt_ref[0, LANES:2 * LANES, pl.ds(w0, WIN_KEYS)], e_w.astype(BF))
                / jnp.maximum(jnp.sum(e_w, axis=0, keepdims=True), 1e-30))

        for g in range(NSA_GROUP):
            h = kh * NSA_GROUP + g
            qs_ = slice(g * Q_BLOCK, (g + 1) * Q_BLOCK)
            o_t = (sig_t[3 * h:3 * h + 1, :] * oc_t[kh][:, qs_] + sig_t[3 * h + 1:3 * h + 2, :] * os_t[:, qs_]
                   + sig_t[3 * h + 2:3 * h + 3, :] * ow_t[:, qs_])
            o_ref[:, h * LANES:(h + 1) * LANES] = o_t.T.astype(BF)


def _nsa_prompt(q, gates, ckp, cvt, ka, vt, batch, t):
    nb = t // BLOCK
    nc = t // Q_BLOCK
    m = batch * t
    assert nb <= LANES
    return pl.pallas_call(
        functools.partial(_nsa_prompt_kernel, nb=nb),
        out_shape=jax.ShapeDtypeStruct((m, Q_PAD_W), BF),
        grid=(batch, nc),
        in_specs=[pl.BlockSpec((Q_BLOCK, Q_PAD_W), lambda b, c: (b * nc + c, 0)),
                  pl.BlockSpec((Q_BLOCK, LANES), lambda b, c: (b * nc + c, 0)),
                  pl.BlockSpec((1, NSA_KV_HEADS, nb, LANES), lambda b, c: (b, 0, 0, 0)),
                  pl.BlockSpec((1, LANES, nb), lambda b, c: (b, 0, 0)),
                  pl.BlockSpec((t, KA_W), lambda b, c: (b, 0)),
                  pl.BlockSpec((1, 2 * LANES, t), lambda b, c: (b, 0, 0))],
        out_specs=pl.BlockSpec((Q_BLOCK, Q_PAD_W), lambda b, c: (b * nc + c, 0)),
        compiler_params=_params(("parallel", "arbitrary")),
        name="nsa_prompt",
    )(q, gates, ckp, cvt, ka, vt)


SAMPLE_ROWS = 16


def _nsa_sample_kernel(q_ref, gate_ref, ck_ref, cv_ref, pg_ref, new_ref, win_ref, col_ref, o_ref, wout_ref, *, past):
    rows = SAMPLE_ROWS
    nbp = cv_ref.shape[1]
    n_pages, page = pg_ref.shape[1], pg_ref.shape[3]
    n_buf = win_ref.shape[3]
    pos = past
    q = q_ref[0].astype(BF)
    qd = q[:, 0:HEAD_DIM]
    qf = qd.astype(F32)
    row = lax.broadcasted_iota(jnp.int32, (rows, 1), 0)
    grp0 = row < NSA_GROUP
    slope = _head_slopes(rows, 1, list(range(NSA_HEADS)))
    sig = jax.nn.sigmoid(gate_ref[0])
    blk = lax.broadcasted_iota(jnp.int32, (1, nbp), 1)
    blk_f = blk.astype(F32)
    blk_end = blk * BLOCK + (BLOCK - 1)

    def by_head(f):
        return jnp.where(grp0, f(0), f(1))

    s_c = by_head(lambda kh: _dot_nt(q, ck_ref[0, kh]))
    p_c = _masked_softmax(s_c, blk_end <= pos)
    o_c2 = _dot(p_c.astype(BF), cv_ref[0])
    o_c = jnp.where(grp0, o_c2[:, 0:HEAD_DIM], o_c2[:, HEAD_DIM:2 * HEAD_DIM])
    imp0 = jnp.sum(jnp.where(grp0, p_c, 0.0), axis=0, keepdims=True)
    imp1 = jnp.sum(jnp.where((row >= NSA_GROUP) & (row < NSA_HEADS), p_c, 0.0), axis=0, keepdims=True)
    imp = jnp.where(grp0, imp0, imp1)
    cur = pos // BLOCK
    forced = (blk == 0) | ((blk <= cur) & (blk > cur - N_LOCAL_BLOCKS))
    score = jnp.where(forced, FORCE_SCORE, imp)
    score = jnp.where(blk <= cur, score, -jnp.inf)
    sel = _select_blocks(score, blk_f, N_SEL)

    def new_row(c, kh):
        o = c * LANES + kh * HEAD_DIM
        return new_ref[0, :, o:o + HEAD_DIM].astype(BF).astype(F32)

    def attend(s, valid, vt_of, s_self, valid_self, v_self):
        s = jnp.where(valid, s, NEG)
        s_self = jnp.where(valid_self, s_self, NEG)
        m = jnp.maximum(jnp.max(s, axis=-1, keepdims=True), s_self)
        e = jnp.where(valid, jnp.exp(s - m), 0.0)
        e_self = jnp.where(valid_self, jnp.exp(s_self - m), 0.0)
        den = jnp.maximum(jnp.sum(e, axis=-1, keepdims=True) + e_self, 1e-30)
        eb = e.astype(BF)
        num = by_head(lambda kh: _dot_nt(eb, vt_of(kh))) + e_self.astype(BF).astype(F32) * v_self
        return num / den

    def page_rows(c, kh):
        o = c * LANES + kh * HEAD_DIM
        return jnp.concatenate([pg_ref[0, p, o:o + HEAD_DIM, :] for p in range(n_pages)], axis=1).astype(BF)

    e_row = lax.broadcasted_iota(jnp.int32, (nbp, past), 0)
    e_col = lax.broadcasted_iota(jnp.int32, (nbp, past), 1) >> 6
    selx = _dot(sel.astype(BF), jnp.where(e_row == e_col, 1.0, 0.0).astype(BF))
    kpos = lax.broadcasted_iota(jnp.int32, (1, past), 1)
    s_s = by_head(lambda kh: _dot(qd, page_rows(0, kh))) + slope * (kpos.astype(F32) - float(pos))
    s_self = by_head(lambda kh: jnp.sum(qf * new_row(2, kh), axis=-1, keepdims=True))
    sel_self = jnp.sum(jnp.where(blk == cur, sel, 0.0), axis=-1, keepdims=True) > 0.5
    o_s = attend(s_s, selx > 0.5, lambda kh: page_rows(1, kh), s_self, sel_self,
                 by_head(lambda kh: jnp.broadcast_to(new_row(3, kh), (rows, HEAD_DIM))))

    dist = n_buf - lax.broadcasted_iota(jnp.int32, (1, n_buf), 1)
    win = lambda c, kh: win_ref[0, 0, c * LANES + kh * HEAD_DIM:c * LANES + (kh + 1) * HEAD_DIM, :].astype(BF)
    s_w = by_head(lambda kh: _dot(qd, win(0, kh))) - slope * dist.astype(F32)
    s_wn = by_head(lambda kh: jnp.sum(qf * new_row(4, kh), axis=-1, keepdims=True))
    o_w = attend(s_w, (dist >= 0) & (dist < WINDOW), lambda kh: win(1, kh), s_wn, row >= 0,
                 by_head(lambda kh: jnp.broadcast_to(new_row(5, kh), (rows, HEAD_DIM))))

    o_ref[0] = sig[:, 0:1] * o_c + sig[:, 1:2] * o_s + sig[:, 2:3] * o_w

    lane = lax.broadcasted_iota(jnp.int32, (1, n_buf), 1)
    wout_ref[0] = jnp.where(lane == n_buf - 1, col_ref[0], pltpu.roll(win_ref[0, 0], n_buf - 1, 1))


def _nsa_sample(q, gates3, ckp, cv, pages, kv_new, win_all, layer, win_col):
    b, n_pages, _, page = pages.shape
    past = n_pages * page
    nbp = cv.shape[1]
    n_buf = win_all.shape[3]
    per3 = lambda shape: pl.BlockSpec((1,) + shape, lambda i: (i, 0, 0))
    per4 = lambda shape: pl.BlockSpec((1,) + shape, lambda i: (i, 0, 0, 0))
    return pl.pallas_call(
        functools.partial(_nsa_sample_kernel, past=past),
        out_shape=[jax.ShapeDtypeStruct((b, SAMPLE_ROWS, HEAD_DIM), F32),
                   jax.ShapeDtypeStruct((b, 2 * LANES, n_buf), F32)],
        grid=(b,),
        in_specs=[per3((SAMPLE_ROWS, LANES)), per3((SAMPLE_ROWS, 3)), per4((NSA_KV_HEADS, nbp, LANES)),
                  per3((nbp, LANES)), per4((n_pages, 2 * LANES, page)), per3((1, KV_W)),
                  pl.BlockSpec((1, 1, 2 * LANES, n_buf), lambda i: (layer, i, 0, 0)), per3((2 * LANES, 1))],
        out_specs=[per3((SAMPLE_ROWS, HEAD_DIM)), per3((2 * LANES, n_buf))],
        compiler_params=_params(("parallel",)),
        name="nsa_sample",
    )(q, gates3, ckp, cv, pages, kv_new, win_all, win_col)


def _rglru_gates(xc, wa_ref, ba_ref, wx_ref, bx_ref, lam_ref):
    xb = xc.astype(BF)
    r = jax.nn.sigmoid(_dot(xb, wa_ref[...]) + ba_ref[...])
    i = jax.nn.sigmoid(_dot(xb, wx_ref[...]) + bx_ref[...])
    z = -lam_ref[...]
    softplus = jnp.maximum(z, 0.0) + jnp.log1p(jnp.exp(-jnp.abs(z)))
    log_a = -RG_C * r * softplus
    th = jnp.tanh(log_a)
    mult = jnp.sqrt(-2.0 * th / (1.0 - th))
    return jnp.exp(log_a), i, mult


def _shift_rows(x, s, fill):
    n = x.shape[0]
    if s % 8 == 0:
        return jnp.concatenate([jnp.full((s, x.shape[1]), fill, x.dtype), x[:n - s]], axis=0)
    r = lax.broadcasted_iota(jnp.int32, (n, 1), 0)
    return jnp.where(r < s, fill, pltpu.roll(x, s, 0))


def _rglru_prompt_kernel(xr_ref, gr_ref, cw_ref, cb_ref, wa_ref, ba_ref, wx_ref, bx_ref, lam_ref,
                         y_ref, hl_ref, buf_ref, h_ref):
    t = pl.program_id(1)
    tt = xr_ref.shape[0]

    @pl.when(t == 0)
    def _():
        buf_ref[0:HALO, :] = jnp.zeros((HALO, D_RNN), F32)
        h_ref[...] = jnp.zeros_like(h_ref)

    buf_ref[HALO:HALO + tt, :] = xr_ref[...]
    xc = cb_ref[...] + cw_ref[RNN_CONV - 1:RNN_CONV, :] * xr_ref[...]
    for j in range(1, RNN_CONV):
        xc = xc + cw_ref[RNN_CONV - 1 - j:RNN_CONV - j, :] * buf_ref[HALO - j:HALO - j + tt, :]
    a, i, mult = _rglru_gates(xc, wa_ref, ba_ref, wx_ref, bx_ref, lam_ref)
    row = lax.broadcasted_iota(jnp.int32, (tt, 1), 0)
    mult = jnp.where(row + t * tt == 0, 1.0, mult)
    b = xc * i * mult
    s = 1
    while s < tt:
        b = a * _shift_rows(b, s, 0.0) + b
        a = a * _shift_rows(a, s, 1.0)
        s *= 2
    h = a * h_ref[0:1, :] + b
    h_ref[0:1, :] = h[tt - 1:tt, :]
    hl_ref[0] = h[tt - 1:tt, :]
    y_ref[...] = (h * _gelu(gr_ref[...])).astype(BF)
    buf_ref[0:HALO, :] = buf_ref[tt:tt + HALO, :]


def _block_diag(w):
    n, c, d = w.shape
    eye = jnp.eye(n, dtype=w.dtype)
    return (w[:, :, None, :] * eye[:, None, :, None]).reshape(n * c, n * d)


def _rglru_prompt(xr, gr, pa, batch, t):
    tt = min(SCAN_TILE, t)
    nt = t // tt
    vec = lambda a: a.reshape(1, D_RNN)
    y, hl = pl.pallas_call(
        _rglru_prompt_kernel,
        out_shape=[jax.ShapeDtypeStruct((batch * t, D_RNN), BF), jax.ShapeDtypeStruct((batch, 1, D_RNN), F32)],
        grid=(batch, nt),
        in_specs=[pl.BlockSpec((tt, D_RNN), lambda b, i: (b * nt + i, 0)),
                  pl.BlockSpec((tt, D_RNN), lambda b, i: (b * nt + i, 0)),
                  _full((RNN_CONV, D_RNN)), _full((1, D_RNN)), _full((D_RNN, D_RNN)), _full((1, D_RNN)),
                  _full((D_RNN, D_RNN)), _full((1, D_RNN)), _full((1, D_RNN))],
        out_specs=[pl.BlockSpec((tt, D_RNN), lambda b, i: (b * nt + i, 0)),
                   pl.BlockSpec((1, 1, D_RNN), lambda b, i: (b, 0, 0))],
        scratch_shapes=[pltpu.VMEM((HALO + tt, D_RNN), F32), pltpu.VMEM((8, D_RNN), F32)],
        compiler_params=_params(("arbitrary", "arbitrary")),
        name="rglru_prompt",
    )(xr, gr, pa['conv_w'], vec(pa['conv_b']), _block_diag(pa['wa']).astype(BF), vec(pa['ba']),
      _block_diag(pa['wx']).astype(BF), vec(pa['bx']), vec(pa['lam']))
    return y, hl.reshape(batch, D_RNN)


def _rglru_sample_kernel(xr_ref, gr_ref, prev_ref, h0_ref, cw_ref, cb_ref, wa_ref, ba_ref, wx_ref, bx_ref, lam_ref,
                         y_ref, h_ref):
    xc = cb_ref[...] + cw_ref[RNN_CONV - 1:RNN_CONV, :] * xr_ref[...]
    for k in range(RNN_CONV - 1):
        xc = xc + cw_ref[k:k + 1, :] * prev_ref[k]
    a, i, mult = _rglru_gates(xc, wa_ref, ba_ref, wx_ref, bx_ref, lam_ref)
    h = xc * i * mult + a * h0_ref[...]
    h_ref[...] = h
    y_ref[...] = (h * _gelu(gr_ref[...])).astype(BF)


def _rglru_sample(xr, gr, h0, prev_t, pa):
    b = xr.shape[0]
    vec = lambda a: a.reshape(1, D_RNN)
    return pl.pallas_call(
        _rglru_sample_kernel,
        out_shape=[jax.ShapeDtypeStruct((b, D_RNN), BF), jax.ShapeDtypeStruct((b, D_RNN), F32)],
        name="rglru_sample",
    )(xr, gr, prev_t, h0, pa['conv_w'], vec(pa['conv_b']),
      _block_diag(pa['wa']).astype(BF), vec(pa['ba']), _block_diag(pa['wx']).astype(BF), vec(pa['bx']),
      vec(pa['lam']))


def _mm2_ln_kernel(a_ref, b_ref, wa_ref, wb_ref, x_ref, g_ref, beta_ref, o_ref):
    y = _dot(a_ref[...], wa_ref[...]) + _dot(b_ref[...], wb_ref[...])
    o_ref[...] = _ln(DN_ALPHA * x_ref[...] + y, g_ref[...], beta_ref[...])


def _mm2_ln(a, b, wa, wb, x, g, beta):
    m = x.shape[0]
    tm = _row_tile(m)
    ka, kb = a.shape[1], b.shape[1]
    return pl.pallas_call(
        _mm2_ln_kernel,
        out_shape=jax.ShapeDtypeStruct((m, D_MODEL), F32),
        grid=(m // tm,),
        in_specs=[pl.BlockSpec((tm, ka), lambda i: (i, 0)), pl.BlockSpec((tm, kb), lambda i: (i, 0)),
                  _full((ka, D_MODEL)), _full((kb, D_MODEL)), pl.BlockSpec((tm, D_MODEL), lambda i: (i, 0)),
                  _full((1, D_MODEL)), _full((1, D_MODEL))],
        out_specs=pl.BlockSpec((tm, D_MODEL), lambda i: (i, 0)),
        compiler_params=_params(("parallel",)),
        name="mm2_ln",
    )(a, b, wa, wb, x, g.reshape(1, -1), beta.reshape(1, -1))


def _mm_kernel(x_ref, w_ref, o_ref, ob_ref):
    y = _dot(x_ref[...].astype(BF), w_ref[0])
    o_ref[0] = y
    ob_ref[0] = y.astype(BF)


def _memory_kv(mem, wkv):
    r = mem.shape[0]
    nl, _, n = wkv.shape
    return pl.pallas_call(
        _mm_kernel,
        out_shape=[jax.ShapeDtypeStruct((nl, r, n), F32), jax.ShapeDtypeStruct((nl, r, n), BF)],
        grid=(nl,),
        in_specs=[_full((r, D_MODEL)), pl.BlockSpec((1, D_MODEL, n), lambda l: (l, 0, 0))],
        out_specs=[pl.BlockSpec((1, r, n), lambda l: (l, 0, 0)), pl.BlockSpec((1, r, n), lambda l: (l, 0, 0))],
        compiler_params=_params(("parallel",)),
        name="memory_kv",
    )(mem, wkv.astype(BF))


def _xattn_prompt_kernel(x_ref, k_ref, v_ref, wq_ref, wo_ref, g_ref, beta_ref, o_ref):
    x = x_ref[...]
    q = _dot(x.astype(BF), wq_ref[...]).astype(BF)
    outs = []
    for h in range(X_HEADS):
        hs = slice(h * X_HEAD_DIM, (h + 1) * X_HEAD_DIM)
        s = _dot_nt(q[:, hs], k_ref[0, :, hs]) * (X_HEAD_DIM ** -0.5)
        m = jnp.max(s, axis=-1, keepdims=True)
        e = jnp.exp(s - m)
        p = e / jnp.sum(e, axis=-1, keepdims=True)
        outs.append(_dot(p.astype(BF), v_ref[0, :, hs]).astype(BF))
    y = _dot(jnp.concatenate(outs, axis=1), wo_ref[...])
    o_ref[...] = _ln(DN_ALPHA * x + y, g_ref[...], beta_ref[...])


def _xattn_prompt(x, kb, vb, wq, wo, g, beta, batch, t):
    tm = _row_tile(t)
    nt = t // tm
    return pl.pallas_call(
        _xattn_prompt_kernel,
        out_shape=jax.ShapeDtypeStruct((batch * t, D_MODEL), F32),
        grid=(batch, nt),
        in_specs=[pl.BlockSpec((tm, D_MODEL), lambda b, i: (b * nt + i, 0)),
                  pl.BlockSpec((1, MEM_LEN, X_W), lambda b, i: (b, 0, 0)),
                  pl.BlockSpec((1, MEM_LEN, X_W), lambda b, i: (b, 0, 0)),
                  _full((D_MODEL, X_W)), _full((X_W, D_MODEL)), _full((1, D_MODEL)), _full((1, D_MODEL))],
        out_specs=pl.BlockSpec((tm, D_MODEL), lambda b, i: (b * nt + i, 0)),
        compiler_params=_params(("parallel", "parallel")),
        name="xattn_prompt",
    )(x, kb, vb, wq, wo, g.reshape(1, -1), beta.reshape(1, -1))


XS_SEQS = 8


def _xattn_sample_kernel(x_ref, kv_ref, wq_ref, wo_ref, g_ref, beta_ref, o_ref):
    x = x_ref[...]
    xb = x.astype(BF)
    qh = [_dot(xb, wq_ref[:, h * X_HEAD_DIM:(h + 1) * X_HEAD_DIM]).astype(BF).astype(F32) for h in range(X_HEADS)]
    rows = []
    for s_i in range(XS_SEQS):
        q4 = jnp.concatenate([qh[h][s_i:s_i + 1, :] for h in range(X_HEADS)], axis=0)
        k = kv_ref[0, s_i, :, 0].astype(BF).astype(F32)
        v = kv_ref[0, s_i, :, 1].astype(BF).astype(F32)
        s = jnp.sum(k * q4[None], axis=-1, keepdims=True) * (X_HEAD_DIM ** -0.5)
        e = jnp.exp(s - jnp.max(s, axis=0, keepdims=True))
        p = (e / jnp.sum(e, axis=0, keepdims=True)).astype(BF).astype(F32)
        o4 = jnp.sum(p * v, axis=0)
        rows.append(jnp.concatenate([o4[h:h + 1, :] for h in range(X_HEADS)], axis=1))
    o = jnp.concatenate(rows, axis=0)
    y = _dot(o.astype(BF), wo_ref[...])
    o_ref[...] = _ln(DN_ALPHA * x + y, g_ref[...], beta_ref[...])


def _xattn_sample(x, cache, layer, wq, wo, g, beta):
    b = x.shape[0]
    return pl.pallas_call(
        _xattn_sample_kernel,
        out_shape=jax.ShapeDtypeStruct((b, D_MODEL), F32),
        grid=(b // XS_SEQS,),
        in_specs=[pl.BlockSpec((XS_SEQS, D_MODEL), lambda i: (i, 0)),
                  pl.BlockSpec((1, XS_SEQS, MEM_LEN, 2, X_HEADS, X_HEAD_DIM), lambda i: (layer, i, 0, 0, 0, 0)),
                  _full((D_MODEL, X_W)), _full((X_W, D_MODEL)), _full((1, D_MODEL)), _full((1, D_MODEL))],
        out_specs=pl.BlockSpec((XS_SEQS, D_MODEL), lambda i: (i, 0)),
        compiler_params=_params(("parallel",)),
        name="xattn_sample",
    )(x, cache, wq, wo, g.reshape(1, -1), beta.reshape(1, -1))


def _swiglu_rows(xb, wg_ref, wu_ref, wd_ref, lead):
    acc = None
    for c in range(D_FF // FF_CHUNK):
        cs = slice(c * FF_CHUNK, (c + 1) * FF_CHUNK)
        hg = _dot(xb, wg_ref[lead + (slice(None), cs)])
        hu = _dot(xb, wu_ref[lead + (slice(None), cs)])
        part = _dot((_silu(hg) * hu).astype(BF), wd_ref[lead + (cs, slice(None))])
        acc = part if acc is None else acc + part
    return acc


def _swiglu_ln_kernel(x_ref, wg_ref, wu_ref, wd_ref, g_ref, beta_ref, o_ref):
    x = x_ref[...]
    y = _swiglu_rows(x.astype(BF), wg_ref, wu_ref, wd_ref, ())
    o_ref[...] = _ln(DN_ALPHA * x + y, g_ref[...], beta_ref[...])


def _swiglu_ln(x, wg, wu, wd, g, beta):
    m = x.shape[0]
    tm = _row_tile(m)
    return pl.pallas_call(
        _swiglu_ln_kernel,
        out_shape=jax.ShapeDtypeStruct((m, D_MODEL), F32),
        grid=(m // tm,),
        in_specs=[pl.BlockSpec((tm, D_MODEL), lambda i: (i, 0)), _full((D_MODEL, D_FF)), _full((D_MODEL, D_FF)),
                  _full((D_FF, D_MODEL)), _full((1, D_MODEL)), _full((1, D_MODEL))],
        out_specs=pl.BlockSpec((tm, D_MODEL), lambda i: (i, 0)),
        compiler_params=_params(("parallel",)),
        name="swiglu_ln",
    )(x, wg, wu, wd, g.reshape(1, -1), beta.reshape(1, -1))


def _router_kernel(x_ref, r_ref, idx_ref, w_ref):
    logits = _dot(x_ref[...].astype(BF), r_ref[...].astype(BF))
    lane = lax.broadcasted_iota(jnp.int32, logits.shape, 1)
    lane_f = lane.astype(F32)
    logits = jnp.where(lane < N_EXPERTS, logits, -jnp.inf)
    m1 = jnp.max(logits, axis=-1, keepdims=True)
    i1 = jnp.min(jnp.where(logits == m1, lane_f, 1.0e9), axis=-1, keepdims=True)
    rest = jnp.where(lane_f == i1, -jnp.inf, logits)
    m2 = jnp.max(rest, axis=-1, keepdims=True)
    i2 = jnp.min(jnp.where(rest == m2, lane_f, 1.0e9), axis=-1, keepdims=True)
    e2 = jnp.exp(m2 - m1)
    den = 1.0 + e2
    idx_ref[...] = jnp.where(lane == 0, i1, i2).astype(jnp.int32)
    w_ref[...] = jnp.where(lane == 0, 1.0 / den, e2 / den)


def _router(x, router):
    m = x.shape[0]
    tm = _row_tile(m)
    rp = jnp.pad(router, ((0, 0), (0, LANES - N_EXPERTS)))
    idx, w = pl.pallas_call(
        _router_kernel,
        out_shape=[jax.ShapeDtypeStruct((m, LANES), jnp.int32), jax.ShapeDtypeStruct((m, LANES), F32)],
        grid=(m // tm,),
        in_specs=[pl.BlockSpec((tm, D_MODEL), lambda i: (i, 0)), _full((D_MODEL, LANES))],
        out_specs=[pl.BlockSpec((tm, LANES), lambda i: (i, 0)), pl.BlockSpec((tm, LANES), lambda i: (i, 0))],
        compiler_params=_params(("parallel",)),
        name="router",
    )(x, rp)
    return idx[:, :TOP_K], w


def _moe_kernel(te_ref, nu_ref, x_ref, wg_ref, wu_ref, wd_ref, o_ref):
    i = pl.program_id(0)

    @pl.when(i < nu_ref[0])
    def _():
        o_ref[...] = _swiglu_rows(x_ref[...], wg_ref, wu_ref, wd_ref, (0,))

    @pl.when(i >= nu_ref[0])
    def _():
        o_ref[...] = jnp.zeros_like(o_ref)


def _moe_ffn(tile_expert, n_used, xs, wg, wu, wd):
    p = xs.shape[0]
    return pl.pallas_call(
        _moe_kernel,
        out_shape=jax.ShapeDtypeStruct((p, D_MODEL), F32),
        grid_spec=pltpu.PrefetchScalarGridSpec(
            num_scalar_prefetch=2,
            grid=(p // MOE_TILE,),
            in_specs=[pl.BlockSpec((MOE_TILE, D_MODEL), lambda i, te, nu: (i, 0)),
                      pl.BlockSpec((1, D_MODEL, D_FF), lambda i, te, nu: (te[i], 0, 0)),
                      pl.BlockSpec((1, D_MODEL, D_FF), lambda i, te, nu: (te[i], 0, 0)),
                      pl.BlockSpec((1, D_FF, D_MODEL), lambda i, te, nu: (te[i], 0, 0))],
            out_specs=pl.BlockSpec((MOE_TILE, D_MODEL), lambda i, te, nu: (i, 0))),
        compiler_params=_params(("arbitrary",)),
        name="moe_ffn",
    )(tile_expert, n_used, xs, wg, wu, wd)


def _add2_ln_kernel(x_ref, y0_ref, y1_ref, w_ref, g_ref, beta_ref, o_ref):
    y = w_ref[:, 0:1] * y0_ref[...] + w_ref[:, 1:2] * y1_ref[...]
    o_ref[...] = _ln(DN_ALPHA * x_ref[...] + y, g_ref[...], beta_ref[...])


def _add2_ln(x, y2, row0, w, g, beta):
    m = x.shape[0]
    tm = _row_tile(m)
    nt = m // tm
    b0 = row0 // tm
    return pl.pallas_call(
        _add2_ln_kernel,
        out_shape=jax.ShapeDtypeStruct((m, D_MODEL), F32),
        grid=(nt,),
        in_specs=[pl.BlockSpec((tm, D_MODEL), lambda i: (i, 0)), pl.BlockSpec((tm, D_MODEL), lambda i: (i + b0, 0)),
                  pl.BlockSpec((tm, D_MODEL), lambda i: (i + b0 + nt, 0)), pl.BlockSpec((tm, LANES), lambda i: (i, 0)),
                  _full((1, D_MODEL)), _full((1, D_MODEL))],
        out_specs=pl.BlockSpec((tm, D_MODEL), lambda i: (i, 0)),
        compiler_params=_params(("parallel",)),
        name="add2_ln",
    )(x, y2, y2, w, g.reshape(1, -1), beta.reshape(1, -1))


def _moe_layer(xp, xs, router, wg, wu, wd, g, beta):
    mp, ms = xp.shape[0], xs.shape[0]
    m = mp + ms
    ip, wp = _router(xp, router)
    is_, ws = _router(xs, router)
    e_flat = jnp.concatenate([ip.T.reshape(-1), is_.T.reshape(-1)])
    tok = jnp.concatenate([jnp.tile(jnp.arange(mp, dtype=jnp.int32), TOP_K),
                           mp + jnp.tile(jnp.arange(ms, dtype=jnp.int32), TOP_K)])
    onehot = (e_flat[:, None] == jnp.arange(N_EXPERTS, dtype=jnp.int32)[None, :]).astype(jnp.int32)
    cum = jnp.cumsum(onehot, axis=0)
    rank = jnp.sum(onehot * (cum - 1), axis=1)
    counts = cum[-1]
    padded = ((counts + MOE_TILE - 1) // MOE_TILE) * MOE_TILE
    ends = jnp.cumsum(padded)
    starts = ends - padded
    pos = starts[e_flat] + rank
    n_tiles = (TOP_K * m + N_EXPERTS * (MOE_TILE - 1)) // MOE_TILE + 1
    p_rows = n_tiles * MOE_TILE
    tile_start = jnp.arange(n_tiles, dtype=jnp.int32) * MOE_TILE
    tile_expert = jnp.minimum(jnp.sum((tile_start[:, None] >= ends[None, :]).astype(jnp.int32), axis=1),
                              N_EXPERTS - 1).astype(jnp.int32)
    n_used = (ends[-1] // MOE_TILE).astype(jnp.int32).reshape(1)
    src = jnp.zeros((p_rows,), jnp.int32).at[pos].set(tok)
    x_sorted = jnp.concatenate([xp.astype(BF), xs.astype(BF)], axis=0)[src]
    y2 = _moe_ffn(tile_expert, n_used, x_sorted, wg, wu, wd)[pos]
    return _add2_ln(xp, y2, 0, wp, g, beta), _add2_ln(xs, y2, TOP_K * mp, ws, g, beta)


def _glu_kernel(x_ref, w_ref, b_ref, o_ref):
    xb = x_ref[...].astype(BF)
    a = _dot(xb, w_ref[:, 0:D_MODEL]) + b_ref[:, 0:D_MODEL]
    gate = _dot(xb, w_ref[:, D_MODEL:2 * D_MODEL]) + b_ref[:, D_MODEL:2 * D_MODEL]
    o_ref[...] = a * jax.nn.sigmoid(gate)


def _glu(x, w, b):
    m = x.shape[0]
    tm = _row_tile(m)
    return pl.pallas_call(
        _glu_kernel,
        out_shape=jax.ShapeDtypeStruct((m, D_MODEL), F32),
        grid=(m // tm,),
        in_specs=[pl.BlockSpec((tm, D_MODEL), lambda i: (i, 0)), _full((D_MODEL, 2 * D_MODEL)),
                  _full((1, 2 * D_MODEL))],
        out_specs=pl.BlockSpec((tm, D_MODEL), lambda i: (i, 0)),
        compiler_params=_params(("parallel",)),
        name="glu",
    )(x, w, b.reshape(1, -1))


CONV_ROWS = 64


def _conv_tail(c, x, cg_ref, cb_ref, wp_ref, bp_ref, g_ref, beta_ref):
    c = _ln(c, cg_ref[...], cb_ref[...])
    y = _dot(_silu(c).astype(BF), wp_ref[...]) + bp_ref[...]
    return _ln(DN_ALPHA * x + y, g_ref[...], beta_ref[...])


def _conv_prompt_kernel(gl_ref, x_ref, dw_ref, db_ref, cg_ref, cb_ref, wp_ref, bp_ref, g_ref, beta_ref,
                        o_ref, buf_ref, sh_ref):
    t = pl.program_id(1)
    tt = gl_ref.shape[0]
    n = HALO + tt

    @pl.when(t == 0)
    def _():
        buf_ref[0:HALO, :] = jnp.zeros((HALO, D_MODEL), F32)

    buf_ref[HALO:HALO + tt, :] = gl_ref[...]
    full = buf_ref[...]
    for r in range(1, SUBLANES):
        sh_ref[r - 1] = pltpu.roll(full, n - r, 0)
    off = HALO - (CONF_K - 1)
    for r0 in range(0, tt, CONV_ROWS):
        c = db_ref[...]
        for k in range(CONF_K):
            o = r0 + off + k
            r = o % SUBLANES
            a = o - r
            win = buf_ref[a:a + CONV_ROWS, :] if r == 0 else sh_ref[r - 1, a:a + CONV_ROWS, :]
            c = c + dw_ref[k:k + 1, :] * win
        o_ref[r0:r0 + CONV_ROWS, :] = _conv_tail(c, x_ref[r0:r0 + CONV_ROWS, :], cg_ref, cb_ref, wp_ref, bp_ref,
                                                 g_ref, beta_ref)
    buf_ref[0:HALO, :] = buf_ref[tt:tt + HALO, :]


def _conv_prompt(gl, x, cp, g, beta, batch, t):
    tt = min(SCAN_TILE, t)
    nt = t // tt
    vec = lambda a: a.reshape(1, -1)
    return pl.pallas_call(
        _conv_prompt_kernel,
        out_shape=jax.ShapeDtypeStruct((batch * t, D_MODEL), F32),
        grid=(batch, nt),
        in_specs=[pl.BlockSpec((tt, D_MODEL), lambda b, i: (b * nt + i, 0)),
                  pl.BlockSpec((tt, D_MODEL), lambda b, i: (b * nt + i, 0)),
                  _full((CONF_K, D_MODEL)), _full((1, D_MODEL)), _full((1, D_MODEL)), _full((1, D_MODEL)),
                  _full((D_MODEL, D_MODEL)), _full((1, D_MODEL)), _full((1, D_MODEL)), _full((1, D_MODEL))],
        out_specs=pl.BlockSpec((tt, D_MODEL), lambda b, i: (b * nt + i, 0)),
        scratch_shapes=[pltpu.VMEM((HALO + tt, D_MODEL), F32), pltpu.VMEM((SUBLANES - 1, HALO + tt, D_MODEL), F32)],
        compiler_params=_params(("arbitrary", "arbitrary")),
        name="conv_prompt",
    )(gl, x, cp['dw_w'], vec(cp['dw_b']), vec(cp['ln_g']), vec(cp['ln_b']), cp['w_pw'], vec(cp['b_pw']),
      vec(g), vec(beta))


CS_SEQS = 32


def _conv_sample_kernel(gl_ref, st_ref, x_ref, dw_ref, db_ref, cg_ref, cb_ref, wp_ref, bp_ref, g_ref, beta_ref,
                        o_ref):
    c = db_ref[...] + dw_ref[CONF_K - 1:CONF_K, :] * gl_ref[...]
    for k in range(CONF_K - 1):
        c = c + dw_ref[k:k + 1, :] * st_ref[0, k]
    o_ref[...] = _conv_tail(c, x_ref[...], cg_ref, cb_ref, wp_ref, bp_ref, g_ref, beta_ref)


def _conv_sample(gl, state_t, layer, x, cp, g, beta):
    b = x.shape[0]
    vec = lambda a: a.reshape(1, -1)
    return pl.pallas_call(
        _conv_sample_kernel,
        out_shape=jax.ShapeDtypeStruct((b, D_MODEL), F32),
        grid=(b // CS_SEQS,),
        in_specs=[pl.BlockSpec((CS_SEQS, D_MODEL), lambda i: (i, 0)),
                  pl.BlockSpec((1, CONF_K - 1, CS_SEQS, D_MODEL), lambda i: (layer, 0, i, 0)),
                  pl.BlockSpec((CS_SEQS, D_MODEL), lambda i: (i, 0)),
                  _full((CONF_K, D_MODEL)), _full((1, D_MODEL)), _full((1, D_MODEL)), _full((1, D_MODEL)),
                  _full((D_MODEL, D_MODEL)), _full((1, D_MODEL)), _full((1, D_MODEL)), _full((1, D_MODEL))],
        out_specs=pl.BlockSpec((CS_SEQS, D_MODEL), lambda i: (i, 0)),
        compiler_params=_params(("parallel",)),
        name="conv_sample",
    )(gl, state_t, x, cp['dw_w'], vec(cp['dw_b']), vec(cp['ln_g']), vec(cp['ln_b']), cp['w_pw'], vec(cp['b_pw']),
      vec(g), vec(beta))


def _flatten_blocks(kt2, lead):
    n = kt2.shape[-1] // BLOCK
    nl = len(lead)
    x = kt2.reshape(*lead, 2, NSA_KV_HEADS, HEAD_DIM, n, BLOCK)
    perm = (nl,) + tuple(range(nl)) + (nl + 3, nl + 1, nl + 2, nl + 4)
    return x.transpose(perm).reshape(2, -1, HEAD_DIM * BLOCK)


def _mixer_a_prompt(x, w_in, wo_nsa, wo_rnn, pa, g, beta, batch, t):
    nb = t // BLOCK
    q, xr, gr, gates, ka, kvt, vt = _proj_a(x, w_in, batch, t)
    cmp = _compress(_flatten_blocks(kvt[:, 0:256], (batch,)), pa['cmp_pos'], pa['cmp_w1'], pa['cmp_w2'])
    ckp = _cmp_keys(cmp[0].reshape(batch, nb, NSA_KV_HEADS, HEAD_DIM), nb)
    cvt = cmp[1].reshape(batch, nb, LANES).transpose(0, 2, 1).astype(BF)
    o_nsa = _nsa_prompt(q, gates, ckp, cvt, ka, vt, batch, t)
    o_rnn, h_last = _rglru_prompt(xr, gr, pa, batch, t)
    x_new = _mm2_ln(o_nsa, o_rnn, wo_nsa, wo_rnn, x, g, beta)
    n_keep = min(WINDOW, t)
    kv_out = kvt[:, 0:512].reshape(batch, 4, NSA_KV_HEADS, HEAD_DIM, t).transpose(0, 4, 1, 2, 3)
    win_out = kvt[:, 512:768, t - n_keep:].reshape(batch, 2, NSA_KV_HEADS, HEAD_DIM, n_keep).transpose(0, 4, 1, 2, 3)
    conv_out = xr.reshape(batch, t, D_RNN)[:, t - (RNN_CONV - 1):]
    return x_new, kv_out, win_out, h_last, conv_out


def _mixer_a_sample(x, pool_all, page_ids, win_all, layer, h0, prev_t, w_in, wo_nsa, wo_rnn, pa, g, beta):
    b = x.shape[0]
    n_pages, page = page_ids.shape[1], pool_all.shape[2]
    nb_past = n_pages * page // BLOCK
    q, xr, gr, gates, _, kvt, _, kv_row = _proj_a(x, w_in, 1, b, row_kv=True)
    pool4 = pool_all.reshape(pool_all.shape[0], 2, 2 * LANES, page)
    pg_cmp = pool4[page_ids, 0]
    pages = pool4[page_ids, 1]
    cmp = _compress_pages(pg_cmp.reshape(b * n_pages, 2 * LANES, page), pa['cmp_pos'], pa['cmp_w1'], pa['cmp_w2'])
    cmp = cmp.reshape(2, b, nb_past, NSA_KV_HEADS, HEAD_DIM)
    ckp = _cmp_keys(cmp[0], LANES)
    cv = jnp.pad(cmp[1].reshape(b, nb_past, LANES), ((0, 0), (0, LANES - nb_past), (0, 0))).astype(BF)
    q16 = jnp.pad(q.astype(F32).reshape(b, NSA_HEADS, LANES), ((0, 0), (0, SAMPLE_ROWS - NSA_HEADS), (0, 0)))
    g3 = jnp.pad(gates[:, :3 * NSA_HEADS].reshape(b, NSA_HEADS, 3), ((0, 0), (0, SAMPLE_ROWS - NSA_HEADS), (0, 0)))
    win_col = kvt[0, 512:768].T[:, :, None]
    o16, win_new = _nsa_sample(q16, g3, ckp, cv, pages, kv_row.reshape(b, 1, KV_W), win_all, layer, win_col)
    o_nsa = o16[:, :NSA_HEADS].reshape(b, NSA_HEADS * HEAD_DIM).astype(BF)
    o_rnn, h_new = _rglru_sample(xr, gr, h0, prev_t, pa)
    x_new = _mm2_ln(o_nsa, o_rnn, wo_nsa, wo_rnn, x, g, beta)
    kv_out = kvt[0, 0:512].reshape(4, NSA_KV_HEADS, HEAD_DIM, b).transpose(3, 0, 1, 2)[:, None]
    conv_out = jnp.concatenate([prev_t[1:], xr[None]], axis=0).transpose(1, 0, 2)
    return x_new, kv_out, win_new, h_new, conv_out


def kernel(x_prompt, x_sample, cache_nsa_kv, cache_nsa_win, state_rglru_h, state_rglru_conv, state_conv,
           cache_mem_kv, page_table, mem_prompt, ln_g, ln_b, a_w_in, a_cmp_pos, a_cmp_w1, a_cmp_w2,
           a_conv_w, a_conv_b, a_gate_a_w, a_gate_a_b, a_gate_x_w, a_gate_x_b, a_lambda, a_w_out,
           c_w_glu, c_b_glu, c_dw_w, c_dw_b, c_ln_g, c_ln_b, c_w_pw, c_b_pw, x_wq, x_wkv, x_wo,
           f_w_gu, f_w_down, m_router, m_w_gu, m_w_down):
    batch, t, _ = x_prompt.shape
    bs = x_sample.shape[0]
    xp = x_prompt.reshape(batch * t, D_MODEL)
    xs = x_sample.reshape(bs, D_MODEL)
    mkv_f, mkv_b = _memory_kv(mem_prompt.reshape(batch * MEM_LEN, D_MODEL), x_wkv.reshape(DEPTH, D_MODEL, 2 * X_W))
    n_layers_a, n_pool, page = cache_nsa_kv.shape[:3]
    pool_all = cache_nsa_kv.transpose(0, 1, 3, 4, 5, 2).reshape(n_layers_a * n_pool, 4 * LANES, page)
    n_buf = cache_nsa_win.shape[2]
    win_t = cache_nsa_win.transpose(0, 1, 3, 4, 5, 2).reshape(n_layers_a, bs, 2 * LANES, n_buf)
    rconv_t = state_rglru_conv.transpose(0, 2, 1, 3)
    sconv_t = state_conv.transpose(0, 2, 1, 3)
    pk, pw, ph, pcv, pc = [], [], [], [], []
    sk, sw, sh, scv, sc = [], [], [], [], []
    for l in range(DEPTH):
        if l % 2 == 0:
            i = l // 2
            pa = {'cmp_pos': a_cmp_pos[i], 'cmp_w1': a_cmp_w1[i], 'cmp_w2': a_cmp_w2[i],
                  'conv_w': a_conv_w[i], 'conv_b': a_conv_b[i], 'wa': a_gate_a_w[i], 'ba': a_gate_a_b[i],
                  'wx': a_gate_x_w[i], 'bx': a_gate_x_b[i], 'lam': a_lambda[i]}
            w_in = _prep_w_in(a_w_in[i])
            wo_pad, wo_cmp, wo_rnn = _prep_w_out(a_w_out[i])
            xp, kv_p, win_p, h_p, cb_p = _mixer_a_prompt(xp, w_in, wo_pad, wo_rnn, pa, ln_g[l, 0], ln_b[l, 0],
                                                         batch, t)
            xs, kv_s, win_s, h_s, cb_s = _mixer_a_sample(xs, pool_all, page_table + i * n_pool, win_t, i,
                                                         state_rglru_h[i], rconv_t[i], w_in, wo_cmp, wo_rnn, pa,
                                                         ln_g[l, 0], ln_b[l, 0])
            pk.append(kv_p); pw.append(win_p); ph.append(h_p); pcv.append(cb_p)
            sk.append(kv_s); sw.append(win_s); sh.append(h_s); scv.append(cb_s)
        else:
            j = l // 2
            cp = {'dw_w': c_dw_w[j], 'dw_b': c_dw_b[j], 'ln_g': c_ln_g[j], 'ln_b': c_ln_b[j],
                  'w_pw': c_w_pw[j].astype(BF), 'b_pw': c_b_pw[j]}
            w_glu = c_w_glu[j].astype(BF)
            gl_p = _glu(xp, w_glu, c_b_glu[j])
            gl_s = _glu(xs, w_glu, c_b_glu[j])
            xp = _conv_prompt(gl_p, xp, cp, ln_g[l, 0], ln_b[l, 0], batch, t)
            xs = _conv_sample(gl_s, sconv_t, j, xs, cp, ln_g[l, 0], ln_b[l, 0])
            pc.append(gl_p.reshape(batch, t, D_MODEL)[:, t - (CONF_K - 1):])
            sc.append(jnp.concatenate([sconv_t[j, 1:], gl_s[None]], axis=0).transpose(1, 0, 2))
        wq = x_wq[l].astype(BF)
        wo = x_wo[l].astype(BF)
        kb = mkv_b[l].reshape(batch, MEM_LEN, 2 * X_W)
        xp = _xattn_prompt(xp, kb[:, :, :X_W], kb[:, :, X_W:], wq, wo, ln_g[l, 1], ln_b[l, 1], batch, t)
        xs = _xattn_sample(xs, cache_mem_kv, l, wq, wo, ln_g[l, 1], ln_b[l, 1])
        if l % 2 == 0:
            i = l // 2
            wg = f_w_gu[i][:, 0].astype(BF)
            wu = f_w_gu[i][:, 1].astype(BF)
            wd = f_w_down[i].astype(BF)
            xp = _swiglu_ln(xp, wg, wu, wd, ln_g[l, 2], ln_b[l, 2])
            xs = _swiglu_ln(xs, wg, wu, wd, ln_g[l, 2], ln_b[l, 2])
        else:
            j = l // 2
            wg = m_w_gu[j][:, :, 0].astype(BF)
            wu = m_w_gu[j][:, :, 1].astype(BF)
            wd = m_w_down[j].astype(BF)
            xp, xs = _moe_layer(xp, xs, m_router[j], wg, wu, wd, ln_g[l, 2], ln_b[l, 2])
    p_mem = mkv_f.reshape(DEPTH, batch, MEM_LEN, 2, X_HEADS, X_HEAD_DIM)
    s_win = jnp.stack(sw).reshape(n_layers_a, bs, 2, NSA_KV_HEADS, HEAD_DIM, n_buf).transpose(0, 1, 5, 2, 3, 4)
    return (xp.reshape(batch, t, D_MODEL), xs.reshape(bs, 1, D_MODEL), jnp.stack(pk), jnp.stack(pw),
            jnp.stack(ph), jnp.stack(pcv), jnp.stack(pc), p_mem,
            jnp.stack(sk), s_win, jnp.stack(sh), jnp.stack(scv), jnp.stack(sc))
```

```python
import functools

import jax
import jax.numpy as jnp
from jax import lax
from jax.experimental import pallas as pl
from jax.experimental.pallas import tpu as pltpu

F32 = jnp.float32
BF = jnp.bfloat16

D_MODEL = 1024
NSA_HEADS = 8
NSA_KV_HEADS = 2
NSA_GROUP = NSA_HEADS // NSA_KV_HEADS
HEAD_DIM = 64
BLOCK = 64
N_SEL = 8
N_LOCAL_BLOCKS = 2
WINDOW = 512
Q_BLOCK = 128
FORCE_SCORE = 1.0e4
D_RNN = 512
RNN_CONV = 4
RG_C = 8.0
CONF_K = 31
MEM_LEN = 256
X_HEADS = 4
X_HEAD_DIM = 128
X_W = X_HEADS * X_HEAD_DIM
D_FF = 2816
N_EXPERTS = 8
TOP_K = 2
LN_EPS = 1e-5
DEPTH = 4
DN_ALPHA = (2.0 * DEPTH) ** 0.25

LANES = 128
SUBLANES = 8
SEL_KEY_TILE = 512
WIN_KEYS = WINDOW + Q_BLOCK
FF_CHUNK = 256
MOE_TILE = 256
SCAN_TILE = 256
HALO = 32
VMEM_LIMIT = 56 * 1024 * 1024
NEG = -1.0e30
MASK_BIAS = -(2.0 ** 30)
ALIBI_SLOPES = tuple(2.0 ** (-8.0 * (i + 1) / NSA_HEADS) for i in range(NSA_HEADS))


def _dot(a, b):
    return jnp.dot(a, b, preferred_element_type=F32)


def _dot_nt(a, b):
    return lax.dot_general(a, b, (((1,), (1,)), ((), ())), preferred_element_type=F32)


def _ln(z, g, b):
    mu = jnp.mean(z, axis=-1, keepdims=True)
    zc = z - mu
    var = jnp.mean(zc * zc, axis=-1, keepdims=True)
    return zc * lax.rsqrt(var + LN_EPS) * g + b


def _gelu(x):
    return 0.5 * x * (1.0 + jnp.tanh(0.7978845608028654 * (x + 0.044715 * (x * x * x))))


def _silu(x):
    return x * jax.nn.sigmoid(x)


def _params(sem):
    return pltpu.CompilerParams(dimension_semantics=sem, vmem_limit_bytes=VMEM_LIMIT)


def _row_tile(m, pref=512):
    return pref if m % pref == 0 else m


def _full(shape):
    n = len(shape)
    return pl.BlockSpec(shape, lambda *_: (0,) * n)


Q_PAD_W = NSA_HEADS * LANES
KV_W = 6 * NSA_KV_HEADS * HEAD_DIM
KA_W = 6 * LANES
ROW_W = Q_PAD_W + 2 * D_RNN + LANES + KA_W
POS_HI = HEAD_DIM
POS_LO = HEAD_DIM + 1


def _proj_a_kernel(x_ref, wrow_ref, wt_ref, qf_ref, *refs, row_kv):
    if row_kv:
        wkv_ref, q_ref, xr_ref, gr_ref, gate_ref, ka_ref, kvt_ref, vt_ref, kvrow_ref = refs
    else:
        q_ref, xr_ref, gr_ref, gate_ref, ka_ref, kvt_ref, vt_ref = refs
    i = pl.program_id(1)
    tm = x_ref.shape[0]
    xb = x_ref[...].astype(BF)
    q = _dot(xb, wrow_ref[:, 0:Q_PAD_W])
    q_ref[...] = (q * (HEAD_DIM ** -0.5) + qf_ref[...]).astype(BF)
    o = Q_PAD_W
    xr_ref[...] = _dot(xb, wrow_ref[:, o:o + D_RNN])
    o += D_RNN
    gr_ref[...] = _dot(xb, wrow_ref[:, o:o + D_RNN])
    o += D_RNN
    gate_ref[...] = _dot(xb, wrow_ref[:, o:o + LANES])
    o += LANES
    pos = i * tm + lax.broadcasted_iota(jnp.int32, (tm, KA_W), 0)
    lane_all = lax.broadcasted_iota(jnp.int32, (tm, KA_W), 1)
    lane = lane_all & (LANES - 1)
    group = lane_all >> 7
    feat = jnp.where((group == 1) | (group == 3), jnp.where(lane == (pos >> 6), 1.0, 0.0),
                     jnp.where(lane == POS_HI, (pos >> 6).astype(F32),
                               jnp.where(lane == POS_LO, (pos & (BLOCK - 1)).astype(F32), 0.0)))
    ka_ref[...] = (_dot(xb, wrow_ref[:, o:o + KA_W]) + feat).astype(BF)
    kvt = _dot_nt(wt_ref[...], xb)
    kvt_ref[0] = kvt
    vt_ref[0, 0:LANES, :] = kvt[3 * LANES:4 * LANES].astype(BF)
    vt_ref[0, LANES:2 * LANES, :] = kvt[5 * LANES:6 * LANES].astype(BF)
    if row_kv:
        kvrow_ref[...] = _dot(xb, wkv_ref[...])


def _proj_a(x, w, batch, t, row_kv=False):
    m = batch * t
    tm = _row_tile(t)
    nt = t // tm
    row = lambda w_: pl.BlockSpec((tm, w_), lambda b, i: (b * nt + i, 0))
    fm = lambda f: pl.BlockSpec((1, f, tm), lambda b, i: (b, 0, i))
    outs = [((m, Q_PAD_W), BF, row(Q_PAD_W)), ((m, D_RNN), F32, row(D_RNN)), ((m, D_RNN), F32, row(D_RNN)),
            ((m, LANES), F32, row(LANES)), ((m, KA_W), BF, row(KA_W)), ((batch, KV_W, t), F32, fm(KV_W)),
            ((batch, 2 * LANES, t), BF, fm(2 * LANES))]
    ins = [x, w['row'], w['t'], w['qf']]
    in_specs = [row(D_MODEL), _full((D_MODEL, ROW_W)), _full((KV_W, D_MODEL)), _full((1, Q_PAD_W))]
    if row_kv:
        ins.append(w['kv'])
        in_specs.append(_full((D_MODEL, KV_W)))
        outs.append(((m, KV_W), F32, row(KV_W)))
    return pl.pallas_call(
        functools.partial(_proj_a_kernel, row_kv=row_kv),
        out_shape=[jax.ShapeDtypeStruct(s, dt) for s, dt, _ in outs],
        grid=(batch, nt),
        in_specs=in_specs,
        out_specs=[sp for _, _, sp in outs],
        compiler_params=_params(("parallel", "parallel")),
        name="proj_a",
    )(*ins)


def _prep_w_in(w_in):
    wq = w_in[:, :512].reshape(D_MODEL, NSA_HEADS, HEAD_DIM)
    wq_p = jnp.concatenate([wq, jnp.zeros_like(wq)], axis=2).reshape(D_MODEL, Q_PAD_W)
    wkv = w_in[:, 512:1280]
    wg = jnp.pad(w_in[:, 1280:1304], ((0, 0), (0, LANES - 3 * NSA_HEADS)))
    wxr = w_in[:, 1304:1816]
    wgr = w_in[:, 1816:2328]
    lane = jnp.arange(Q_PAD_W) % LANES
    slope = jnp.repeat(jnp.asarray(ALIBI_SLOPES, F32), LANES)
    qf = jnp.where(lane == POS_HI, BLOCK * slope, jnp.where(lane == POS_LO, slope, 0.0)).reshape(1, Q_PAD_W)
    wkv6 = wkv.reshape(D_MODEL, 6, NSA_KV_HEADS, HEAD_DIM)
    wk = jnp.stack([wkv6[:, 2], wkv6[:, 4]], axis=1)
    wk = jnp.concatenate([wk, jnp.zeros_like(wk)], axis=3)
    z = jnp.zeros((D_MODEL, LANES), F32)
    wka = jnp.concatenate([wk[:, 0, 0], z, wk[:, 0, 1], z, wk[:, 1, 0], wk[:, 1, 1]], axis=1)
    return {'row': jnp.concatenate([wq_p, wxr, wgr, wg, wka], axis=1).astype(BF), 't': wkv.T.astype(BF),
            'kv': wkv.astype(BF), 'qf': qf}


def _prep_w_out(w_out):
    wn = w_out[:512].reshape(NSA_HEADS, HEAD_DIM, D_MODEL)
    z = jnp.zeros_like(wn)
    lo = jnp.concatenate([wn, z], axis=1)
    hi = jnp.concatenate([z, wn], axis=1)
    wn_p = jnp.concatenate([lo[:NSA_GROUP], hi[NSA_GROUP:]], axis=0).reshape(Q_PAD_W, D_MODEL)
    return wn_p.astype(BF), w_out[:512].astype(BF), w_out[512:].astype(BF)


def _compress_kernel(x_ref, pos_ref, w1_ref, w2_ref, o_ref):
    xb = (x_ref[0] + pos_ref[0]).astype(BF)
    h = _gelu(_dot(xb, w1_ref[0]))
    o_ref[0] = _dot(h.astype(BF), w2_ref[0])


def _compress(xblk, pos, w1, w2):
    r = xblk.shape[1]
    tr = _row_tile(r, 256)
    kdim = BLOCK * HEAD_DIM
    pos_t = pos.transpose(0, 2, 1).reshape(2, 1, kdim)
    w1_t = w1.reshape(2, BLOCK, HEAD_DIM, w1.shape[-1]).transpose(0, 2, 1, 3).reshape(2, kdim, w1.shape[-1])
    return pl.pallas_call(
        _compress_kernel,
        out_shape=jax.ShapeDtypeStruct((2, r, HEAD_DIM), F32),
        grid=(2, r // tr),
        in_specs=[pl.BlockSpec((1, tr, kdim), lambda c, i: (c, i, 0)),
                  pl.BlockSpec((1, 1, kdim), lambda c, i: (c, 0, 0)),
                  pl.BlockSpec((1, kdim, 128), lambda c, i: (c, 0, 0)),
                  pl.BlockSpec((1, 128, HEAD_DIM), lambda c, i: (c, 0, 0))],
        out_specs=pl.BlockSpec((1, tr, HEAD_DIM), lambda c, i: (c, i, 0)),
        compiler_params=_params(("parallel", "parallel")),
        name="compress",
    )(xblk, pos_t, w1_t.astype(BF), w2.astype(BF))


CP_SEQS = 8


def _compress_pages_kernel(pg_ref, pos_ref, w1_ref, w2_ref, o_ref, *, rows, stride):
    c = pl.program_id(1)
    outs = []
    for h in range(NSA_KV_HEADS):
        def body(d, acc, h=h):
            f = c * LANES + h * HEAD_DIM + d
            lhs = pg_ref[pl.ds(f, rows, stride=stride), :] + pos_ref[0, pl.ds(d, 1), :]
            return acc + _dot(lhs.astype(BF), w1_ref[0, d])

        acc = lax.fori_loop(0, HEAD_DIM, body, jnp.zeros((rows, 2 * 128), F32), unroll=8)
        hid = _gelu(acc).astype(BF)
        outs.append([_dot(hid[:, k * 128:(k + 1) * 128], w2_ref[0]) for k in range(2)])
    o_ref[0] = jnp.concatenate([outs[0][0], outs[1][0], outs[0][1], outs[1][1]], axis=1)


def _compress_pages(pages, pos, w1, w2):
    n, stride, page = pages.shape
    assert page == 2 * BLOCK and stride == 2 * LANES
    rows = min(n, CP_SEQS * 16)
    hidden = w1.shape[-1]
    pos_t = jnp.tile(pos.transpose(0, 2, 1), (1, 1, page // BLOCK))
    w1_t = w1.reshape(2, BLOCK, HEAD_DIM, hidden).transpose(0, 2, 1, 3)
    z = jnp.zeros_like(w1_t)
    w1_bd = jnp.concatenate([jnp.concatenate([w1_t, z], axis=3), jnp.concatenate([z, w1_t], axis=3)], axis=2)
    return pl.pallas_call(
        functools.partial(_compress_pages_kernel, rows=rows, stride=stride),
        out_shape=jax.ShapeDtypeStruct((2, n, 4 * HEAD_DIM), F32),
        grid=(n // rows, 2),
        in_specs=[pl.BlockSpec((rows * stride, page), lambda i, c: (i, 0)),
                  pl.BlockSpec((1, HEAD_DIM, page), lambda i, c: (c, 0, 0)),
                  pl.BlockSpec((1, HEAD_DIM, page, 2 * hidden), lambda i, c: (c, 0, 0, 0)),
                  pl.BlockSpec((1, hidden, HEAD_DIM), lambda i, c: (c, 0, 0))],
        out_specs=pl.BlockSpec((1, rows, 4 * HEAD_DIM), lambda i, c: (c, i, 0)),
        compiler_params=_params(("parallel", "arbitrary")),
        name="compress_pages",
    )(pages.reshape(n * stride, page), pos_t, w1_bd.astype(BF), w2.astype(BF))


def _cmp_keys(ck, nb_pad):
    batch, nb = ck.shape[0], ck.shape[1]
    ck = jnp.pad(ck.transpose(0, 2, 1, 3), ((0, 0), (0, 0), (0, nb_pad - nb), (0, 0)))
    lane = jnp.arange(HEAD_DIM)[None, :]
    n = jnp.arange(nb_pad, dtype=F32)[:, None]
    ext = jnp.where(lane == 0, n, jnp.where(lane == 1, float(BLOCK - 1), 0.0))
    ext = jnp.broadcast_to(ext[None, None], (batch, NSA_KV_HEADS, nb_pad, HEAD_DIM))
    return jnp.concatenate([ck, ext], axis=-1).astype(BF)


def _head_slopes(rows, rows_per_head, heads):
    r = lax.broadcasted_iota(jnp.int32, (rows, 1), 0)
    s = jnp.full((rows, 1), 0.0, F32)
    for g, h in enumerate(heads):
        s = jnp.where((r >= g * rows_per_head) & (r < (g + 1) * rows_per_head), ALIBI_SLOPES[h], s)
    return s


def _masked_softmax(s, valid, axis=-1):
    s = jnp.where(valid, s, NEG)
    m = jnp.max(s, axis=axis, keepdims=True)
    e = jnp.where(valid, jnp.exp(s - m), 0.0)
    return e / jnp.maximum(jnp.sum(e, axis=axis, keepdims=True), 1e-30)


def _select_blocks(score, blk_f, n_pick, axis=-1):
    sel = jnp.zeros(score.shape, F32)
    for _ in range(n_pick):
        m = jnp.max(score, axis=axis, keepdims=True)
        idx = jnp.min(jnp.where(score == m, blk_f, 1.0e9), axis=axis, keepdims=True)
        hit = blk_f == idx
        sel = jnp.where(hit & (m > -jnp.inf), 1.0, sel)
        score = jnp.where(hit, -jnp.inf, score)
    return sel


def _nsa_prompt_kernel(q_ref, gate_ref, ck_ref, cvt_ref, ka_ref, vt_ref, o_ref, *, nb):
    c = pl.program_id(1)
    start = c * Q_BLOCK
    rows = NSA_GROUP * Q_BLOCK
    sig_t = jax.nn.sigmoid(gate_ref[...]).T
    qpos_l = start + (lax.broadcasted_iota(jnp.int32, (1, rows), 1) & (Q_BLOCK - 1))
    blk = lax.broadcasted_iota(jnp.int32, (nb, 1), 0)
    blk_f = blk.astype(F32)
    cmp_valid = blk * BLOCK + (BLOCK - 1) <= qpos_l
    cur = qpos_l[:, 0:Q_BLOCK] >> 6
    forced = (blk == 0) | ((blk <= cur) & (blk > cur - N_LOCAL_BLOCKS))
    n_tiles = (start + Q_BLOCK + SEL_KEY_TILE - 1) // SEL_KEY_TILE
    krow = lax.broadcasted_iota(jnp.int32, (SEL_KEY_TILE, 1), 0)
    w0 = pl.multiple_of(jnp.maximum(start - WINDOW, 0), Q_BLOCK)
    dist = qpos_l[:, 0:Q_BLOCK] - (w0 + lax.broadcasted_iota(jnp.int32, (WIN_KEYS, 1), 0))
    wbias = jnp.where((dist >= 0) & (dist < WINDOW), 0.0, MASK_BIAS)
    wbias = jnp.concatenate([wbias] * NSA_GROUP, axis=1)

    kv_heads = range(NSA_KV_HEADS)
    qs, oc_t, qsm, first_far = [], [], [], []
    for kh in kv_heads:
        q4 = jnp.concatenate([q_ref[:, h * LANES:(h + 1) * LANES]
                              for h in range(kh * NSA_GROUP, (kh + 1) * NSA_GROUP)], axis=0)
        p_c = _masked_softmax(_dot_nt(ck_ref[0, kh], q4), cmp_valid, axis=0)
        oc_t.append(_dot(cvt_ref[0], p_c.astype(BF)))
        imp = p_c[:, 0:Q_BLOCK]
        for g in range(1, NSA_GROUP):
            imp = imp + p_c[:, g * Q_BLOCK:(g + 1) * Q_BLOCK]
        score = jnp.where(forced, FORCE_SCORE, imp)
        score = jnp.where(blk <= cur, score, -jnp.inf)
        sel = _select_blocks(score, blk_f, min(N_SEL, nb), axis=0)
        far = jnp.where((sel > 0.5) & (blk >= SEL_KEY_TILE // BLOCK), blk_f, 1.0e9)
        far = jnp.min(jnp.min(far, axis=1, keepdims=True), axis=0, keepdims=True).astype(jnp.int32)
        first_far.append(far[0, 0] // (SEL_KEY_TILE // BLOCK))
        qm = ((sel.T - 1.0) * (-MASK_BIAS)).astype(BF)
        if nb < LANES:
            qm = jnp.concatenate([qm, jnp.zeros((Q_BLOCK, LANES - nb), BF)], axis=1)
        qs.append(q4)
        qsm.append(jnp.concatenate([q4, jnp.concatenate([qm] * NSA_GROUP, axis=0)], axis=1))

    def scores(kh, k0):
        keys = ka_ref[pl.ds(k0, SEL_KEY_TILE), 2 * kh * LANES:2 * (kh + 1) * LANES]
        return _dot_nt(keys, qsm[kh])

    def update(s, carry, k0):
        m_i, l_i, acc = carry
        m_new = jnp.maximum(m_i, jnp.max(s, axis=0, keepdims=True))
        p = jnp.exp(s - m_new)
        alpha = jnp.exp(m_i - m_new)
        l_new = alpha * l_i + jnp.sum(p, axis=0, keepdims=True)
        acc = alpha * acc + _dot(vt_ref[0, 0:LANES, pl.ds(k0, SEL_KEY_TILE)], p.astype(BF))
        return m_new, l_new, acc

    def sel_step(j, carry):
        k0 = pl.multiple_of(j * SEL_KEY_TILE, SEL_KEY_TILE)
        s = [scores(kh, k0) for kh in kv_heads]
        return tuple(update(s[kh], carry[kh], k0) for kh in kv_heads)

    def solo_step(kh):
        def step(j, carry):
            k0 = pl.multiple_of(j * SEL_KEY_TILE, SEL_KEY_TILE)
            new = update(scores(kh, k0), carry[kh], k0)
            return tuple(new if h == kh else carry[h] for h in kv_heads)
        return step

    init = (jnp.full((1, rows), NEG, F32), jnp.zeros((1, rows), F32), jnp.zeros((LANES, rows), F32))
    last = n_tiles - 1
    first = jnp.minimum(1, last)
    a0, a1 = [jnp.minimum(jnp.maximum(f, first), last) for f in first_far]
    carry = lax.fori_loop(0, first, sel_step, (init,) * NSA_KV_HEADS)
    carry = lax.fori_loop(a0, a1, solo_step(0), carry)
    carry = lax.fori_loop(a1, a0, solo_step(1), carry)
    carry = lax.fori_loop(jnp.maximum(a0, a1), last, sel_step, carry)
    k0 = pl.multiple_of(last * SEL_KEY_TILE, SEL_KEY_TILE)
    causal = k0 + krow <= qpos_l
    s_d = [jnp.where(causal, scores(kh, k0), MASK_BIAS) for kh in kv_heads]
    s_w = [_dot_nt(ka_ref[pl.ds(w0, WIN_KEYS), (4 + kh) * LANES:(5 + kh) * LANES], qs[kh]) + wbias
           for kh in kv_heads]
    for kh in kv_heads:
        _, l_s, acc_s = update(s_d[kh], carry[kh], k0)
        os_t = acc_s / jnp.maximum(l_s, 1e-30)
        e_w = jnp.exp(s_w[kh] - jnp.max(s_w[kh], axis=0, keepdims=True))
        ow_t = (_dot(vt_ref[0, LANES:2 * LANES, pl.ds(w0, WIN_KEYS)], e_w.astype(BF))
                / jnp.maximum(jnp.sum(e_w, axis=0, keepdims=True), 1e-30))

        for g in range(NSA_GROUP):
            h = kh * NSA_GROUP + g
            qs_ = slice(g * Q_BLOCK, (g + 1) * Q_BLOCK)
            o_t = (sig_t[3 * h:3 * h + 1, :] * oc_t[kh][:, qs_] + sig_t[3 * h + 1:3 * h + 2, :] * os_t[:, qs_]
                   + sig_t[3 * h + 2:3 * h + 3, :] * ow_t[:, qs_])
            o_ref[:, h * LANES:(h + 1) * LANES] = o_t.T.astype(BF)


def _nsa_prompt(q, gates, ckp, cvt, ka, vt, batch, t):
    nb = t // BLOCK
    nc = t // Q_BLOCK
    m = batch * t
    assert nb <= LANES
    return pl.pallas_call(
        functools.partial(_nsa_prompt_kernel, nb=nb),
        out_shape=jax.ShapeDtypeStruct((m, Q_PAD_W), BF),
        grid=(batch, nc),
        in_specs=[pl.BlockSpec((Q_BLOCK, Q_PAD_W), lambda b, c: (b * nc + c, 0)),
                  pl.BlockSpec((Q_BLOCK, LANES), lambda b, c: (b * nc + c, 0)),
                  pl.BlockSpec((1, NSA_KV_HEADS, nb, LANES), lambda b, c: (b, 0, 0, 0)),
                  pl.BlockSpec((1, LANES, nb), lambda b, c: (b, 0, 0)),
                  pl.BlockSpec((t, KA_W), lambda b, c: (b, 0)),
                  pl.BlockSpec((1, 2 * LANES, t), lambda b, c: (b, 0, 0))],
        out_specs=pl.BlockSpec((Q_BLOCK, Q_PAD_W), lambda b, c: (b * nc + c, 0)),
        compiler_params=_params(("parallel", "arbitrary")),
        name="nsa_prompt",
    )(q, gates, ckp, cvt, ka, vt)


SAMPLE_ROWS = 16


def _nsa_sample_kernel(q_ref, gate_ref, ck_ref, cv_ref, pg_ref, new_ref, win_ref, col_ref, o_ref, wout_ref, *, past):
    rows = SAMPLE_ROWS
    nbp = cv_ref.shape[1]
    n_pages, page = pg_ref.shape[1], pg_ref.shape[3]
    n_buf = win_ref.shape[3]
    pos = past
    q = q_ref[0].astype(BF)
    qd = q[:, 0:HEAD_DIM]
    qf = qd.astype(F32)
    row = lax.broadcasted_iota(jnp.int32, (rows, 1), 0)
    grp0 = row < NSA_GROUP
    slope = _head_slopes(rows, 1, list(range(NSA_HEADS)))
    sig = jax.nn.sigmoid(gate_ref[0])
    blk = lax.broadcasted_iota(jnp.int32, (1, nbp), 1)
    blk_f = blk.astype(F32)
    blk_end = blk * BLOCK + (BLOCK - 1)

    def by_head(f):
        return jnp.where(grp0, f(0), f(1))

    s_c = by_head(lambda kh: _dot_nt(q, ck_ref[0, kh]))
    p_c = _masked_softmax(s_c, blk_end <= pos)
    o_c2 = _dot(p_c.astype(BF), cv_ref[0])
    o_c = jnp.where(grp0, o_c2[:, 0:HEAD_DIM], o_c2[:, HEAD_DIM:2 * HEAD_DIM])
    imp0 = jnp.sum(jnp.where(grp0, p_c, 0.0), axis=0, keepdims=True)
    imp1 = jnp.sum(jnp.where((row >= NSA_GROUP) & (row < NSA_HEADS), p_c, 0.0), axis=0, keepdims=True)
    imp = jnp.where(grp0, imp0, imp1)
    cur = pos // BLOCK
    forced = (blk == 0) | ((blk <= cur) & (blk > cur - N_LOCAL_BLOCKS))
    score = jnp.where(forced, FORCE_SCORE, imp)
    score = jnp.where(blk <= cur, score, -jnp.inf)
    sel = _select_blocks(score, blk_f, N_SEL)

    def new_row(c, kh):
        o = c * LANES + kh * HEAD_DIM
        return new_ref[0, :, o:o + HEAD_DIM].astype(BF).astype(F32)

    def attend(s, valid, vt_of, s_self, valid_self, v_self):
        s = jnp.where(valid, s, NEG)
        s_self = jnp.where(valid_self, s_self, NEG)
        m = jnp.maximum(jnp.max(s, axis=-1, keepdims=True), s_self)
        e = jnp.where(valid, jnp.exp(s - m), 0.0)
        e_self = jnp.where(valid_self, jnp.exp(s_self - m), 0.0)
        den = jnp.maximum(jnp.sum(e, axis=-1, keepdims=True) + e_self, 1e-30)
        eb = e.astype(BF)
        num = by_head(lambda kh: _dot_nt(eb, vt_of(kh))) + e_self.astype(BF).astype(F32) * v_self
        return num / den

    def page_rows(c, kh):
        o = c * LANES + kh * HEAD_DIM
        return jnp.concatenate([pg_ref[0, p, o:o + HEAD_DIM, :] for p in range(n_pages)], axis=1).astype(BF)

    e_row = lax.broadcasted_iota(jnp.int32, (nbp, past), 0)
    e_col = lax.broadcasted_iota(jnp.int32, (nbp, past), 1) >> 6
    selx = _dot(sel.astype(BF), jnp.where(e_row == e_col, 1.0, 0.0).astype(BF))
    kpos = lax.broadcasted_iota(jnp.int32, (1, past), 1)
    s_s = by_head(lambda kh: _dot(qd, page_rows(0, kh))) + slope * (kpos.astype(F32) - float(pos))
    s_self = by_head(lambda kh: jnp.sum(qf * new_row(2, kh), axis=-1, keepdims=True))
    sel_self = jnp.sum(jnp.where(blk == cur, sel, 0.0), axis=-1, keepdims=True) > 0.5
    o_s = attend(s_s, selx > 0.5, lambda kh: page_rows(1, kh), s_self, sel_self,
                 by_head(lambda kh: jnp.broadcast_to(new_row(3, kh), (rows, HEAD_DIM))))

    dist = n_buf - lax.broadcasted_iota(jnp.int32, (1, n_buf), 1)
    win = lambda c, kh: win_ref[0, 0, c * LANES + kh * HEAD_DIM:c * LANES + (kh + 1) * HEAD_DIM, :].astype(BF)
    s_w = by_head(lambda kh: _dot(qd, win(0, kh))) - slope * dist.astype(F32)
    s_wn = by_head(lambda kh: jnp.sum(qf * new_row(4, kh), axis=-1, keepdims=True))
    o_w = attend(s_w, (dist >= 0) & (dist < WINDOW), lambda kh: win(1, kh), s_wn, row >= 0,
                 by_head(lambda kh: jnp.broadcast_to(new_row(5, kh), (rows, HEAD_DIM))))

    o_ref[0] = sig[:, 0:1] * o_c + sig[:, 1:2] * o_s + sig[:, 2:3] * o_w

    lane = lax.broadcasted_iota(jnp.int32, (1, n_buf), 1)
    wout_ref[0] = jnp.where(lane == n_buf - 1, col_ref[0], pltpu.roll(win_ref[0, 0], n_buf - 1, 1))


def _nsa_sample(q, gates3, ckp, cv, pages, kv_new, win_all, layer, win_col):
    b, n_pages, _, page = pages.shape
    past = n_pages * page
    nbp = cv.shape[1]
    n_buf = win_all.shape[3]
    per3 = lambda shape: pl.BlockSpec((1,) + shape, lambda i: (i, 0, 0))
    per4 = lambda shape: pl.BlockSpec((1,) + shape, lambda i: (i, 0, 0, 0))
    return pl.pallas_call(
        functools.partial(_nsa_sample_kernel, past=past),
        out_shape=[jax.ShapeDtypeStruct((b, SAMPLE_ROWS, HEAD_DIM), F32),
                   jax.ShapeDtypeStruct((b, 2 * LANES, n_buf), F32)],
        grid=(b,),
        in_specs=[per3((SAMPLE_ROWS, LANES)), per3((SAMPLE_ROWS, 3)), per4((NSA_KV_HEADS, nbp, LANES)),
                  per3((nbp, LANES)), per4((n_pages, 2 * LANES, page)), per3((1, KV_W)),
                  pl.BlockSpec((1, 1, 2 * LANES, n_buf), lambda i: (layer, i, 0, 0)), per3((2 * LANES, 1))],
        out_specs=[per3((SAMPLE_ROWS, HEAD_DIM)), per3((2 * LANES, n_buf))],
        compiler_params=_params(("parallel",)),
        name="nsa_sample",
    )(q, gates3, ckp, cv, pages, kv_new, win_all, win_col)


def _rglru_gates(xc, wa_ref, ba_ref, wx_ref, bx_ref, lam_ref):
    xb = xc.astype(BF)
    r = jax.nn.sigmoid(_dot(xb, wa_ref[...]) + ba_ref[...])
    i = jax.nn.sigmoid(_dot(xb, wx_ref[...]) + bx_ref[...])
    z = -lam_ref[...]
    softplus = jnp.maximum(z, 0.0) + jnp.log1p(jnp.exp(-jnp.abs(z)))
    log_a = -RG_C * r * softplus
    th = jnp.tanh(log_a)
    mult = jnp.sqrt(-2.0 * th / (1.0 - th))
    return jnp.exp(log_a), i, mult


def _shift_rows(x, s, fill):
    n = x.shape[0]
    if s % 8 == 0:
        return jnp.concatenate([jnp.full((s, x.shape[1]), fill, x.dtype), x[:n - s]], axis=0)
    r = lax.broadcasted_iota(jnp.int32, (n, 1), 0)
    return jnp.where(r < s, fill, pltpu.roll(x, s, 0))


def _rglru_prompt_kernel(xr_ref, gr_ref, cw_ref, cb_ref, wa_ref, ba_ref, wx_ref, bx_ref, lam_ref,
                         y_ref, hl_ref, buf_ref, h_ref):
    t = pl.program_id(1)
    tt = xr_ref.shape[0]

    @pl.when(t == 0)
    def _():
        buf_ref[0:HALO, :] = jnp.zeros((HALO, D_RNN), F32)
        h_ref[...] = jnp.zeros_like(h_ref)

    buf_ref[HALO:HALO + tt, :] = xr_ref[...]
    xc = cb_ref[...] + cw_ref[RNN_CONV - 1:RNN_CONV, :] * xr_ref[...]
    for j in range(1, RNN_CONV):
        xc = xc + cw_ref[RNN_CONV - 1 - j:RNN_CONV - j, :] * buf_ref[HALO - j:HALO - j + tt, :]
    a, i, mult = _rglru_gates(xc, wa_ref, ba_ref, wx_ref, bx_ref, lam_ref)
    row = lax.broadcasted_iota(jnp.int32, (tt, 1), 0)
    mult = jnp.where(row + t * tt == 0, 1.0, mult)
    b = xc * i * mult
    s = 1
    while s < tt:
        b = a * _shift_rows(b, s, 0.0) + b
        a = a * _shift_rows(a, s, 1.0)
        s *= 2
    h = a * h_ref[0:1, :] + b
    h_ref[0:1, :] = h[tt - 1:tt, :]
    hl_ref[0] = h[tt - 1:tt, :]
    y_ref[...] = (h * _gelu(gr_ref[...])).astype(BF)
    buf_ref[0:HALO, :] = buf_ref[tt:tt + HALO, :]


def _block_diag(w):
    n, c, d = w.shape
    eye = jnp.eye(n, dtype=w.dtype)
    return (w[:, :, None, :] * eye[:, None, :, None]).reshape(n * c, n * d)


def _rglru_prompt(xr, gr, pa, batch, t):
    tt = min(SCAN_TILE, t)
    nt = t // tt
    vec = lambda a: a.reshape(1, D_RNN)
    y, hl = pl.pallas_call(
        _rglru_prompt_kernel,
        out_shape=[jax.ShapeDtypeStruct((batch * t, D_RNN), BF), jax.ShapeDtypeStruct((batch, 1, D_RNN), F32)],
        grid=(batch, nt),
        in_specs=[pl.BlockSpec((tt, D_RNN), lambda b, i: (b * nt + i, 0)),
                  pl.BlockSpec((tt, D_RNN), lambda b, i: (b * nt + i, 0)),
                  _full((RNN_CONV, D_RNN)), _full((1, D_RNN)), _full((D_RNN, D_RNN)), _full((1, D_RNN)),
                  _full((D_RNN, D_RNN)), _full((1, D_RNN)), _full((1, D_RNN))],
        out_specs=[pl.BlockSpec((tt, D_RNN), lambda b, i: (b * nt + i, 0)),
                   pl.BlockSpec((1, 1, D_RNN), lambda b, i: (b, 0, 0))],
        scratch_shapes=[pltpu.VMEM((HALO + tt, D_RNN), F32), pltpu.VMEM((8, D_RNN), F32)],
        compiler_params=_params(("arbitrary", "arbitrary")),
        name="rglru_prompt",
    )(xr, gr, pa['conv_w'], vec(pa['conv_b']), _block_diag(pa['wa']).astype(BF), vec(pa['ba']),
      _block_diag(pa['wx']).astype(BF), vec(pa['bx']), vec(pa['lam']))
    return y, hl.reshape(batch, D_RNN)


def _rglru_sample_kernel(xr_ref, gr_ref, prev_ref, h0_ref, cw_ref, cb_ref, wa_ref, ba_ref, wx_ref, bx_ref, lam_ref,
                         y_ref, h_ref):
    xc = cb_ref[...] + cw_ref[RNN_CONV - 1:RNN_CONV, :] * xr_ref[...]
    for k in range(RNN_CONV - 1):
        xc = xc + cw_ref[k:k + 1, :] * prev_ref[k]
    a, i, mult = _rglru_gates(xc, wa_ref, ba_ref, wx_ref, bx_ref, lam_ref)
    h = xc * i * mult + a * h0_ref[...]
    h_ref[...] = h
    y_ref[...] = (h * _gelu(gr_ref[...])).astype(BF)


def _rglru_sample(xr, gr, h0, prev_t, pa):
    b = xr.shape[0]
    vec = lambda a: a.reshape(1, D_RNN)
    return pl.pallas_call(
        _rglru_sample_kernel,
        out_shape=[jax.ShapeDtypeStruct((b, D_RNN), BF), jax.ShapeDtypeStruct((b, D_RNN), F32)],
        name="rglru_sample",
    )(xr, gr, prev_t, h0, pa['conv_w'], vec(pa['conv_b']),
      _block_diag(pa['wa']).astype(BF), vec(pa['ba']), _block_diag(pa['wx']).astype(BF), vec(pa['bx']),
      vec(pa['lam']))


def _mm2_ln_kernel(a_ref, b_ref, wa_ref, wb_ref, x_ref, g_ref, beta_ref, o_ref):
    y = _dot(a_ref[...], wa_ref[...]) + _dot(b_ref[...], wb_ref[...])
    o_ref[...] = _ln(DN_ALPHA * x_ref[...] + y, g_ref[...], beta_ref[...])


def _mm2_ln(a, b, wa, wb, x, g, beta):
    m = x.shape[0]
    tm = _row_tile(m)
    ka, kb = a.shape[1], b.shape[1]
    return pl.pallas_call(
        _mm2_ln_kernel,
        out_shape=jax.ShapeDtypeStruct((m, D_MODEL), F32),
        grid=(m // tm,),
        in_specs=[pl.BlockSpec((tm, ka), lambda i: (i, 0)), pl.BlockSpec((tm, kb), lambda i: (i, 0)),
                  _full((ka, D_MODEL)), _full((kb, D_MODEL)), pl.BlockSpec((tm, D_MODEL), lambda i: (i, 0)),
                  _full((1, D_MODEL)), _full((1, D_MODEL))],
        out_specs=pl.BlockSpec((tm, D_MODEL), lambda i: (i, 0)),
        compiler_params=_params(("parallel",)),
        name="mm2_ln",
    )(a, b, wa, wb, x, g.reshape(1, -1), beta.reshape(1, -1))


def _mm_kernel(x_ref, w_ref, o_ref, ob_ref):
    y = _dot(x_ref[...].astype(BF), w_ref[0])
    o_ref[0] = y
    ob_ref[0] = y.astype(BF)


def _memory_kv(mem, wkv):
    r = mem.shape[0]
    nl, _, n = wkv.shape
    return pl.pallas_call(
        _mm_kernel,
        out_shape=[jax.ShapeDtypeStruct((nl, r, n), F32), jax.ShapeDtypeStruct((nl, r, n), BF)],
        grid=(nl,),
        in_specs=[_full((r, D_MODEL)), pl.BlockSpec((1, D_MODEL, n), lambda l: (l, 0, 0))],
        out_specs=[pl.BlockSpec((1, r, n), lambda l: (l, 0, 0)), pl.BlockSpec((1, r, n), lambda l: (l, 0, 0))],
        compiler_params=_params(("parallel",)),
        name="memory_kv",
    )(mem, wkv.astype(BF))


def _xattn_prompt_kernel(x_ref, k_ref, v_ref, wq_ref, wo_ref, g_ref, beta_ref, o_ref):
    x = x_ref[...]
    q = _dot(x.astype(BF), wq_ref[...]).astype(BF)
    outs = []
    for h in range(X_HEADS):
        hs = slice(h * X_HEAD_DIM, (h + 1) * X_HEAD_DIM)
        s = _dot_nt(q[:, hs], k_ref[0, :, hs]) * (X_HEAD_DIM ** -0.5)
        m = jnp.max(s, axis=-1, keepdims=True)
        e = jnp.exp(s - m)
        p = e / jnp.sum(e, axis=-1, keepdims=True)
        outs.append(_dot(p.astype(BF), v_ref[0, :, hs]).astype(BF))
    y = _dot(jnp.concatenate(outs, axis=1), wo_ref[...])
    o_ref[...] = _ln(DN_ALPHA * x + y, g_ref[...], beta_ref[...])


def _xattn_prompt(x, kb, vb, wq, wo, g, beta, batch, t):
    tm = _row_tile(t)
    nt = t // tm
    return pl.pallas_call(
        _xattn_prompt_kernel,
        out_shape=jax.ShapeDtypeStruct((batch * t, D_MODEL), F32),
        grid=(batch, nt),
        in_specs=[pl.BlockSpec((tm, D_MODEL), lambda b, i: (b * nt + i, 0)),
                  pl.BlockSpec((1, MEM_LEN, X_W), lambda b, i: (b, 0, 0)),
                  pl.BlockSpec((1, MEM_LEN, X_W), lambda b, i: (b, 0, 0)),
                  _full((D_MODEL, X_W)), _full((X_W, D_MODEL)), _full((1, D_MODEL)), _full((1, D_MODEL))],
        out_specs=pl.BlockSpec((tm, D_MODEL), lambda b, i: (b * nt + i, 0)),
        compiler_params=_params(("parallel", "parallel")),
        name="xattn_prompt",
    )(x, kb, vb, wq, wo, g.reshape(1, -1), beta.reshape(1, -1))


XS_SEQS = 8


def _xattn_sample_kernel(x_ref, kv_ref, wq_ref, wo_ref, g_ref, beta_ref, o_ref):
    x = x_ref[...]
    xb = x.astype(BF)
    qh = [_dot(xb, wq_ref[:, h * X_HEAD_DIM:(h + 1) * X_HEAD_DIM]).astype(BF).astype(F32) for h in range(X_HEADS)]
    rows = []
    for s_i in range(XS_SEQS):
        q4 = jnp.concatenate([qh[h][s_i:s_i + 1, :] for h in range(X_HEADS)], axis=0)
        k = kv_ref[0, s_i, :, 0].astype(BF).astype(F32)
        v = kv_ref[0, s_i, :, 1].astype(BF).astype(F32)
        s = jnp.sum(k * q4[None], axis=-1, keepdims=True) * (X_HEAD_DIM ** -0.5)
        e = jnp.exp(s - jnp.max(s, axis=0, keepdims=True))
        p = (e / jnp.sum(e, axis=0, keepdims=True)).astype(BF).astype(F32)
        o4 = jnp.sum(p * v, axis=0)
        rows.append(jnp.concatenate([o4[h:h + 1, :] for h in range(X_HEADS)], axis=1))
    o = jnp.concatenate(rows, axis=0)
    y = _dot(o.astype(BF), wo_ref[...])
    o_ref[...] = _ln(DN_ALPHA * x + y, g_ref[...], beta_ref[...])


def _xattn_sample(x, cache, layer, wq, wo, g, beta):
    b = x.shape[0]
    return pl.pallas_call(
        _xattn_sample_kernel,
        out_shape=jax.ShapeDtypeStruct((b, D_MODEL), F32),
        grid=(b // XS_SEQS,),
        in_specs=[pl.BlockSpec((XS_SEQS, D_MODEL), lambda i: (i, 0)),
                  pl.BlockSpec((1, XS_SEQS, MEM_LEN, 2, X_HEADS, X_HEAD_DIM), lambda i: (layer, i, 0, 0, 0, 0)),
                  _full((D_MODEL, X_W)), _full((X_W, D_MODEL)), _full((1, D_MODEL)), _full((1, D_MODEL))],
        out_specs=pl.BlockSpec((XS_SEQS, D_MODEL), lambda i: (i, 0)),
        compiler_params=_params(("parallel",)),
        name="xattn_sample",
    )(x, cache, wq, wo, g.reshape(1, -1), beta.reshape(1, -1))


def _swiglu_rows(xb, wgu_ref, wd_ref, lead):
    acc = None
    for c in range(D_FF // FF_CHUNK):
        cs = slice(c * FF_CHUNK, (c + 1) * FF_CHUNK)
        us = slice(D_FF + c * FF_CHUNK, D_FF + (c + 1) * FF_CHUNK)
        hg = _dot(xb, wgu_ref[lead + (slice(None), cs)])
        hu = _dot(xb, wgu_ref[lead + (slice(None), us)])
        part = _dot((_silu(hg) * hu).astype(BF), wd_ref[lead + (cs, slice(None))])
        acc = part if acc is None else acc + part
    return acc


def _swiglu_ln_kernel(x_ref, wgu_ref, wd_ref, g_ref, beta_ref, o_ref):
    x = x_ref[...]
    y = _swiglu_rows(x.astype(BF), wgu_ref, wd_ref, ())
    o_ref[...] = _ln(DN_ALPHA * x + y, g_ref[...], beta_ref[...])


def _swiglu_ln(x, wgu, wd, g, beta):
    m = x.shape[0]
    tm = _row_tile(m)
    return pl.pallas_call(
        _swiglu_ln_kernel,
        out_shape=jax.ShapeDtypeStruct((m, D_MODEL), F32),
        grid=(m // tm,),
        in_specs=[pl.BlockSpec((tm, D_MODEL), lambda i: (i, 0)), _full((D_MODEL, 2 * D_FF)),
                  _full((D_FF, D_MODEL)), _full((1, D_MODEL)), _full((1, D_MODEL))],
        out_specs=pl.BlockSpec((tm, D_MODEL), lambda i: (i, 0)),
        compiler_params=_params(("parallel",)),
        name="swiglu_ln",
    )(x, wgu, wd, g.reshape(1, -1), beta.reshape(1, -1))


def _router_kernel(x_ref, r_ref, idx_ref, w_ref):
    logits = _dot(x_ref[...].astype(BF), r_ref[...].astype(BF))
    lane = lax.broadcasted_iota(jnp.int32, logits.shape, 1)
    lane_f = lane.astype(F32)
    logits = jnp.where(lane < N_EXPERTS, logits, -jnp.inf)
    m1 = jnp.max(logits, axis=-1, keepdims=True)
    i1 = jnp.min(jnp.where(logits == m1, lane_f, 1.0e9), axis=-1, keepdims=True)
    rest = jnp.where(lane_f == i1, -jnp.inf, logits)
    m2 = jnp.max(rest, axis=-1, keepdims=True)
    i2 = jnp.min(jnp.where(rest == m2, lane_f, 1.0e9), axis=-1, keepdims=True)
    e2 = jnp.exp(m2 - m1)
    den = 1.0 + e2
    idx_ref[...] = jnp.where(lane == 0, i1, i2).astype(jnp.int32)
    w_ref[...] = jnp.where(lane == 0, 1.0 / den, e2 / den)


def _router(x, router):
    m = x.shape[0]
    tm = _row_tile(m)
    rp = jnp.pad(router, ((0, 0), (0, LANES - N_EXPERTS)))
    idx, w = pl.pallas_call(
        _router_kernel,
        out_shape=[jax.ShapeDtypeStruct((m, LANES), jnp.int32), jax.ShapeDtypeStruct((m, LANES), F32)],
        grid=(m // tm,),
        in_specs=[pl.BlockSpec((tm, D_MODEL), lambda i: (i, 0)), _full((D_MODEL, LANES))],
        out_specs=[pl.BlockSpec((tm, LANES), lambda i: (i, 0)), pl.BlockSpec((tm, LANES), lambda i: (i, 0))],
        compiler_params=_params(("parallel",)),
        name="router",
    )(x, rp)
    return idx[:, :TOP_K], w


def _moe_kernel(te_ref, nu_ref, x_ref, wgu_ref, wd_ref, o_ref):
    i = pl.program_id(0)

    @pl.when(i < nu_ref[0])
    def _():
        o_ref[...] = _swiglu_rows(x_ref[...], wgu_ref, wd_ref, (0,))

    @pl.when(i >= nu_ref[0])
    def _():
        o_ref[...] = jnp.zeros_like(o_ref)


def _moe_ffn(tile_expert, n_used, xs, wgu, wd):
    p = xs.shape[0]
    return pl.pallas_call(
        _moe_kernel,
        out_shape=jax.ShapeDtypeStruct((p, D_MODEL), F32),
        grid_spec=pltpu.PrefetchScalarGridSpec(
            num_scalar_prefetch=2,
            grid=(p // MOE_TILE,),
            in_specs=[pl.BlockSpec((MOE_TILE, D_MODEL), lambda i, te, nu: (i, 0)),
                      pl.BlockSpec((1, D_MODEL, 2 * D_FF), lambda i, te, nu: (te[i], 0, 0)),
                      pl.BlockSpec((1, D_FF, D_MODEL), lambda i, te, nu: (te[i], 0, 0))],
            out_specs=pl.BlockSpec((MOE_TILE, D_MODEL), lambda i, te, nu: (i, 0))),
        compiler_params=_params(("arbitrary",)),
        name="moe_ffn",
    )(tile_expert, n_used, xs, wgu, wd)


def _add2_ln_kernel(x_ref, y0_ref, y1_ref, w_ref, g_ref, beta_ref, o_ref):
    y = w_ref[:, 0:1] * y0_ref[...] + w_ref[:, 1:2] * y1_ref[...]
    o_ref[...] = _ln(DN_ALPHA * x_ref[...] + y, g_ref[...], beta_ref[...])


def _add2_ln(x, y2, row0, w, g, beta):
    m = x.shape[0]
    tm = _row_tile(m)
    nt = m // tm
    b0 = row0 // tm
    return pl.pallas_call(
        _add2_ln_kernel,
        out_shape=jax.ShapeDtypeStruct((m, D_MODEL), F32),
        grid=(nt,),
        in_specs=[pl.BlockSpec((tm, D_MODEL), lambda i: (i, 0)), pl.BlockSpec((tm, D_MODEL), lambda i: (i + b0, 0)),
                  pl.BlockSpec((tm, D_MODEL), lambda i: (i + b0 + nt, 0)), pl.BlockSpec((tm, LANES), lambda i: (i, 0)),
                  _full((1, D_MODEL)), _full((1, D_MODEL))],
        out_specs=pl.BlockSpec((tm, D_MODEL), lambda i: (i, 0)),
        compiler_params=_params(("parallel",)),
        name="add2_ln",
    )(x, y2, y2, w, g.reshape(1, -1), beta.reshape(1, -1))


def _moe_layer(xp, xs, router, wgu, wd, g, beta):
    mp, ms = xp.shape[0], xs.shape[0]
    m = mp + ms
    ip, wp = _router(xp, router)
    is_, ws = _router(xs, router)
    e_flat = jnp.concatenate([ip.T.reshape(-1), is_.T.reshape(-1)])
    tok = jnp.concatenate([jnp.tile(jnp.arange(mp, dtype=jnp.int32), TOP_K),
                           mp + jnp.tile(jnp.arange(ms, dtype=jnp.int32), TOP_K)])
    onehot = (e_flat[:, None] == jnp.arange(N_EXPERTS, dtype=jnp.int32)[None, :]).astype(jnp.int32)
    cum = jnp.cumsum(onehot, axis=0)
    rank = jnp.sum(onehot * (cum - 1), axis=1)
    counts = cum[-1]
    padded = ((counts + MOE_TILE - 1) // MOE_TILE) * MOE_TILE
    ends = jnp.cumsum(padded)
    starts = ends - padded
    pos = starts[e_flat] + rank
    n_tiles = (TOP_K * m + N_EXPERTS * (MOE_TILE - 1)) // MOE_TILE + 1
    p_rows = n_tiles * MOE_TILE
    tile_start = jnp.arange(n_tiles, dtype=jnp.int32) * MOE_TILE
    tile_expert = jnp.minimum(jnp.sum((tile_start[:, None] >= ends[None, :]).astype(jnp.int32), axis=1),
                              N_EXPERTS - 1).astype(jnp.int32)
    n_used = (ends[-1] // MOE_TILE).astype(jnp.int32).reshape(1)
    src = jnp.zeros((p_rows,), jnp.int32).at[pos].set(tok)
    x_sorted = jnp.concatenate([xp.astype(BF), xs.astype(BF)], axis=0)[src]
    y2 = _moe_ffn(tile_expert, n_used, x_sorted, wgu, wd)[pos]
    return _add2_ln(xp, y2, 0, wp, g, beta), _add2_ln(xs, y2, TOP_K * mp, ws, g, beta)


def _glu_kernel(x_ref, w_ref, b_ref, o_ref):
    xb = x_ref[...].astype(BF)
    a = _dot(xb, w_ref[:, 0:D_MODEL]) + b_ref[:, 0:D_MODEL]
    gate = _dot(xb, w_ref[:, D_MODEL:2 * D_MODEL]) + b_ref[:, D_MODEL:2 * D_MODEL]
    o_ref[...] = a * jax.nn.sigmoid(gate)


def _glu(x, w, b):
    m = x.shape[0]
    tm = _row_tile(m)
    return pl.pallas_call(
        _glu_kernel,
        out_shape=jax.ShapeDtypeStruct((m, D_MODEL), F32),
        grid=(m // tm,),
        in_specs=[pl.BlockSpec((tm, D_MODEL), lambda i: (i, 0)), _full((D_MODEL, 2 * D_MODEL)),
                  _full((1, 2 * D_MODEL))],
        out_specs=pl.BlockSpec((tm, D_MODEL), lambda i: (i, 0)),
        compiler_params=_params(("parallel",)),
        name="glu",
    )(x, w, b.reshape(1, -1))


CONV_ROWS = 64


def _conv_tail(c, x, cg_ref, cb_ref, wp_ref, bp_ref, g_ref, beta_ref):
    c = _ln(c, cg_ref[...], cb_ref[...])
    y = _dot(_silu(c).astype(BF), wp_ref[...]) + bp_ref[...]
    return _ln(DN_ALPHA * x + y, g_ref[...], beta_ref[...])


def _conv_prompt_kernel(gl_ref, x_ref, dw_ref, db_ref, cg_ref, cb_ref, wp_ref, bp_ref, g_ref, beta_ref,
                        o_ref, buf_ref, sh_ref):
    t = pl.program_id(1)
    tt = gl_ref.shape[0]
    n = HALO + tt

    @pl.when(t == 0)
    def _():
        buf_ref[0:HALO, :] = jnp.zeros((HALO, D_MODEL), F32)

    buf_ref[HALO:HALO + tt, :] = gl_ref[...]
    full = buf_ref[...]
    for r in range(1, SUBLANES):
        sh_ref[r - 1] = pltpu.roll(full, n - r, 0)
    off = HALO - (CONF_K - 1)
    for r0 in range(0, tt, CONV_ROWS):
        c = db_ref[...]
        for k in range(CONF_K):
            o = r0 + off + k
            r = o % SUBLANES
            a = o - r
            win = buf_ref[a:a + CONV_ROWS, :] if r == 0 else sh_ref[r - 1, a:a + CONV_ROWS, :]
            c = c + dw_ref[k:k + 1, :] * win
        o_ref[r0:r0 + CONV_ROWS, :] = _conv_tail(c, x_ref[r0:r0 + CONV_ROWS, :], cg_ref, cb_ref, wp_ref, bp_ref,
                                                 g_ref, beta_ref)
    buf_ref[0:HALO, :] = buf_ref[tt:tt + HALO, :]


def _conv_prompt(gl, x, cp, g, beta, batch, t):
    tt = min(SCAN_TILE, t)
    nt = t // tt
    vec = lambda a: a.reshape(1, -1)
    return pl.pallas_call(
        _conv_prompt_kernel,
        out_shape=jax.ShapeDtypeStruct((batch * t, D_MODEL), F32),
        grid=(batch, nt),
        in_specs=[pl.BlockSpec((tt, D_MODEL), lambda b, i: (b * nt + i, 0)),
                  pl.BlockSpec((tt, D_MODEL), lambda b, i: (b * nt + i, 0)),
                  _full((CONF_K, D_MODEL)), _full((1, D_MODEL)), _full((1, D_MODEL)), _full((1, D_MODEL)),
                  _full((D_MODEL, D_MODEL)), _full((1, D_MODEL)), _full((1, D_MODEL)), _full((1, D_MODEL))],
        out_specs=pl.BlockSpec((tt, D_MODEL), lambda b, i: (b * nt + i, 0)),
        scratch_shapes=[pltpu.VMEM((HALO + tt, D_MODEL), F32), pltpu.VMEM((SUBLANES - 1, HALO + tt, D_MODEL), F32)],
        compiler_params=_params(("arbitrary", "arbitrary")),
        name="conv_prompt",
    )(gl, x, cp['dw_w'], vec(cp['dw_b']), vec(cp['ln_g']), vec(cp['ln_b']), cp['w_pw'], vec(cp['b_pw']),
      vec(g), vec(beta))


CS_SEQS = 32


def _conv_sample_kernel(gl_ref, st_ref, x_ref, dw_ref, db_ref, cg_ref, cb_ref, wp_ref, bp_ref, g_ref, beta_ref,
                        o_ref):
    c = db_ref[...] + dw_ref[CONF_K - 1:CONF_K, :] * gl_ref[...]
    for k in range(CONF_K - 1):
        c = c + dw_ref[k:k + 1, :] * st_ref[0, k]
    o_ref[...] = _conv_tail(c, x_ref[...], cg_ref, cb_ref, wp_ref, bp_ref, g_ref, beta_ref)


def _conv_sample(gl, state_t, layer, x, cp, g, beta):
    b = x.shape[0]
    vec = lambda a: a.reshape(1, -1)
    return pl.pallas_call(
        _conv_sample_kernel,
        out_shape=jax.ShapeDtypeStruct((b, D_MODEL), F32),
        grid=(b // CS_SEQS,),
        in_specs=[pl.BlockSpec((CS_SEQS, D_MODEL), lambda i: (i, 0)),
                  pl.BlockSpec((1, CONF_K - 1, CS_SEQS, D_MODEL), lambda i: (layer, 0, i, 0)),
                  pl.BlockSpec((CS_SEQS, D_MODEL), lambda i: (i, 0)),
                  _full((CONF_K, D_MODEL)), _full((1, D_MODEL)), _full((1, D_MODEL)), _full((1, D_MODEL)),
                  _full((D_MODEL, D_MODEL)), _full((1, D_MODEL)), _full((1, D_MODEL)), _full((1, D_MODEL))],
        out_specs=pl.BlockSpec((CS_SEQS, D_MODEL), lambda i: (i, 0)),
        compiler_params=_params(("parallel",)),
        name="conv_sample",
    )(gl, state_t, x, cp['dw_w'], vec(cp['dw_b']), vec(cp['ln_g']), vec(cp['ln_b']), cp['w_pw'], vec(cp['b_pw']),
      vec(g), vec(beta))


def _flatten_blocks(kt2, lead):
    n = kt2.shape[-1] // BLOCK
    nl = len(lead)
    x = kt2.reshape(*lead, 2, NSA_KV_HEADS, HEAD_DIM, n, BLOCK)
    perm = (nl,) + tuple(range(nl)) + (nl + 3, nl + 1, nl + 2, nl + 4)
    return x.transpose(perm).reshape(2, -1, HEAD_DIM * BLOCK)


def _mixer_a_prompt(x, w_in, wo_nsa, wo_rnn, pa, g, beta, batch, t):
    nb = t // BLOCK
    q, xr, gr, gates, ka, kvt, vt = _proj_a(x, w_in, batch, t)
    cmp = _compress(_flatten_blocks(kvt[:, 0:256], (batch,)), pa['cmp_pos'], pa['cmp_w1'], pa['cmp_w2'])
    ckp = _cmp_keys(cmp[0].reshape(batch, nb, NSA_KV_HEADS, HEAD_DIM), nb)
    cvt = cmp[1].reshape(batch, nb, LANES).transpose(0, 2, 1).astype(BF)
    o_nsa = _nsa_prompt(q, gates, ckp, cvt, ka, vt, batch, t)
    o_rnn, h_last = _rglru_prompt(xr, gr, pa, batch, t)
    x_new = _mm2_ln(o_nsa, o_rnn, wo_nsa, wo_rnn, x, g, beta)
    n_keep = min(WINDOW, t)
    kv_out = kvt[:, 0:512].reshape(batch, 4, NSA_KV_HEADS, HEAD_DIM, t).transpose(0, 4, 1, 2, 3)
    win_out = kvt[:, 512:768, t - n_keep:].reshape(batch, 2, NSA_KV_HEADS, HEAD_DIM, n_keep).transpose(0, 4, 1, 2, 3)
    conv_out = xr.reshape(batch, t, D_RNN)[:, t - (RNN_CONV - 1):]
    return x_new, kv_out, win_out, h_last, conv_out


def _mixer_a_sample(x, pool_all, page_ids, win_all, layer, h0, prev_t, w_in, wo_nsa, wo_rnn, pa, g, beta):
    b = x.shape[0]
    n_pages, page = page_ids.shape[1], pool_all.shape[2]
    nb_past = n_pages * page // BLOCK
    q, xr, gr, gates, _, kvt, _, kv_row = _proj_a(x, w_in, 1, b, row_kv=True)
    pool4 = pool_all.reshape(pool_all.shape[0], 2, 2 * LANES, page)
    pg_cmp = pool4[page_ids, 0]
    pages = pool4[page_ids, 1]
    cmp = _compress_pages(pg_cmp.reshape(b * n_pages, 2 * LANES, page), pa['cmp_pos'], pa['cmp_w1'], pa['cmp_w2'])
    cmp = cmp.reshape(2, b, nb_past, NSA_KV_HEADS, HEAD_DIM)
    ckp = _cmp_keys(cmp[0], LANES)
    cv = jnp.pad(cmp[1].reshape(b, nb_past, LANES), ((0, 0), (0, LANES - nb_past), (0, 0))).astype(BF)
    q16 = jnp.pad(q.astype(F32).reshape(b, NSA_HEADS, LANES), ((0, 0), (0, SAMPLE_ROWS - NSA_HEADS), (0, 0)))
    g3 = jnp.pad(gates[:, :3 * NSA_HEADS].reshape(b, NSA_HEADS, 3), ((0, 0), (0, SAMPLE_ROWS - NSA_HEADS), (0, 0)))
    win_col = kvt[0, 512:768].T[:, :, None]
    o16, win_new = _nsa_sample(q16, g3, ckp, cv, pages, kv_row.reshape(b, 1, KV_W), win_all, layer, win_col)
    o_nsa = o16[:, :NSA_HEADS].reshape(b, NSA_HEADS * HEAD_DIM).astype(BF)
    o_rnn, h_new = _rglru_sample(xr, gr, h0, prev_t, pa)
    x_new = _mm2_ln(o_nsa, o_rnn, wo_nsa, wo_rnn, x, g, beta)
    kv_out = kvt[0, 0:512].reshape(4, NSA_KV_HEADS, HEAD_DIM, b).transpose(3, 0, 1, 2)[:, None]
    conv_out = jnp.concatenate([prev_t[1:], xr[None]], axis=0).transpose(1, 0, 2)
    return x_new, kv_out, win_new, h_new, conv_out


def kernel(x_prompt, x_sample, cache_nsa_kv, cache_nsa_win, state_rglru_h, state_rglru_conv, state_conv,
           cache_mem_kv, page_table, mem_prompt, ln_g, ln_b, a_w_in, a_cmp_pos, a_cmp_w1, a_cmp_w2,
           a_conv_w, a_conv_b, a_gate_a_w, a_gate_a_b, a_gate_x_w, a_gate_x_b, a_lambda, a_w_out,
           c_w_glu, c_b_glu, c_dw_w, c_dw_b, c_ln_g, c_ln_b, c_w_pw, c_b_pw, x_wq, x_wkv, x_wo,
           f_w_gu, f_w_down, m_router, m_w_gu, m_w_down):
    batch, t, _ = x_prompt.shape
    bs = x_sample.shape[0]
    xp = x_prompt.reshape(batch * t, D_MODEL)
    xs = x_sample.reshape(bs, D_MODEL)
    mkv_f, mkv_b = _memory_kv(mem_prompt.reshape(batch * MEM_LEN, D_MODEL), x_wkv.reshape(DEPTH, D_MODEL, 2 * X_W))
    n_layers_a, n_pool, page = cache_nsa_kv.shape[:3]
    pool_all = cache_nsa_kv.transpose(0, 1, 3, 4, 5, 2).reshape(n_layers_a * n_pool, 4 * LANES, page)
    n_buf = cache_nsa_win.shape[2]
    win_t = cache_nsa_win.transpose(0, 1, 3, 4, 5, 2).reshape(n_layers_a, bs, 2 * LANES, n_buf)
    rconv_t = state_rglru_conv.transpose(0, 2, 1, 3)
    sconv_t = state_conv.transpose(0, 2, 1, 3)
    pk, pw, ph, pcv, pc = [], [], [], [], []
    sk, sw, sh, scv, sc = [], [], [], [], []
    for l in range(DEPTH):
        if l % 2 == 0:
            i = l // 2
            pa = {'cmp_pos': a_cmp_pos[i], 'cmp_w1': a_cmp_w1[i], 'cmp_w2': a_cmp_w2[i],
                  'conv_w': a_conv_w[i], 'conv_b': a_conv_b[i], 'wa': a_gate_a_w[i], 'ba': a_gate_a_b[i],
                  'wx': a_gate_x_w[i], 'bx': a_gate_x_b[i], 'lam': a_lambda[i]}
            w_in = _prep_w_in(a_w_in[i])
            wo_pad, wo_cmp, wo_rnn = _prep_w_out(a_w_out[i])
            xp, kv_p, win_p, h_p, cb_p = _mixer_a_prompt(xp, w_in, wo_pad, wo_rnn, pa, ln_g[l, 0], ln_b[l, 0],
                                                         batch, t)
            xs, kv_s, win_s, h_s, cb_s = _mixer_a_sample(xs, pool_all, page_table + i * n_pool, win_t, i,
                                                         state_rglru_h[i], rconv_t[i], w_in, wo_cmp, wo_rnn, pa,
                                                         ln_g[l, 0], ln_b[l, 0])
            pk.append(kv_p); pw.append(win_p); ph.append(h_p); pcv.append(cb_p)
            sk.append(kv_s); sw.append(win_s); sh.append(h_s); scv.append(cb_s)
        else:
            j = l // 2
            cp = {'dw_w': c_dw_w[j], 'dw_b': c_dw_b[j], 'ln_g': c_ln_g[j], 'ln_b': c_ln_b[j],
                  'w_pw': c_w_pw[j].astype(BF), 'b_pw': c_b_pw[j]}
            w_glu = c_w_glu[j].astype(BF)
            gl_p = _glu(xp, w_glu, c_b_glu[j])
            gl_s = _glu(xs, w_glu, c_b_glu[j])
            xp = _conv_prompt(gl_p, xp, cp, ln_g[l, 0], ln_b[l, 0], batch, t)
            xs = _conv_sample(gl_s, sconv_t, j, xs, cp, ln_g[l, 0], ln_b[l, 0])
            pc.append(gl_p.reshape(batch, t, D_MODEL)[:, t - (CONF_K - 1):])
            sc.append(jnp.concatenate([sconv_t[j, 1:], gl_s[None]], axis=0).transpose(1, 0, 2))
        wq = x_wq[l].astype(BF)
        wo = x_wo[l].astype(BF)
        kb = mkv_b[l].reshape(batch, MEM_LEN, 2 * X_W)
        xp = _xattn_prompt(xp, kb[:, :, :X_W], kb[:, :, X_W:], wq, wo, ln_g[l, 1], ln_b[l, 1], batch, t)
        xs = _xattn_sample(xs, cache_mem_kv, l, wq, wo, ln_g[l, 1], ln_b[l, 1])
        if l % 2 == 0:
            i = l // 2
            wgu = f_w_gu[i].reshape(D_MODEL, 2 * D_FF).astype(BF)
            wd = f_w_down[i].astype(BF)
            xp = _swiglu_ln(xp, wgu, wd, ln_g[l, 2], ln_b[l, 2])
            xs = _swiglu_ln(xs, wgu, wd, ln_g[l, 2], ln_b[l, 2])
        else:
            j = l // 2
            wgu = m_w_gu[j].reshape(N_EXPERTS, D_MODEL, 2 * D_FF).astype(BF)
            wd = m_w_down[j].astype(BF)
            xp, xs = _moe_layer(xp, xs, m_router[j], wgu, wd, ln_g[l, 2], ln_b[l, 2])
    p_mem = mkv_f.reshape(DEPTH, batch, MEM_LEN, 2, X_HEADS, X_HEAD_DIM)
    s_win = jnp.stack(sw).reshape(n_layers_a, bs, 2, NSA_KV_HEADS, HEAD_DIM, n_buf).transpose(0, 1, 5, 2, 3, 4)
    return (xp.reshape(batch, t, D_MODEL), xs.reshape(bs, 1, D_MODEL), jnp.stack(pk), jnp.stack(pw),
            jnp.stack(ph), jnp.stack(pcv), jnp.stack(pc), p_mem,
            jnp.stack(sk), s_win, jnp.stack(sh), jnp.stack(scv), jnp.stack(sc))
```
